```python
import jax, jax.numpy as jnp
from jax import lax
import numpy as np

D_MODEL = 2048
BATCH = 4
SEQ = 4096
DEPTH = 1

SGU_CHUNK = 128
SGU_GROUPS = 8
SGU_WIDTH = D_MODEL // 2
MOBA_HEADS = 16
MOBA_HEAD_DIM = D_MODEL // MOBA_HEADS
MOBA_WIDTH = MOBA_HEADS * MOBA_HEAD_DIM
MOBA_BLOCK = 256
MOBA_TOPK = 3
MOBA_Q_CHUNK = 128
ROPE_THETA = 10000.0
MOE_GROUPS = 4
MOE_EXPERTS_PER_GROUP = 8
MOE_EXPERTS = MOE_GROUPS * MOE_EXPERTS_PER_GROUP
MOE_TOPK = 2
MOE_D_FF = D_MODEL // 4
MOE_ROW_BLOCK = 128
EPS = 1e-6
NEG_INF = -1e30
IN_COLS = 2 * SGU_WIDTH + 3 * MOBA_WIDTH + 2 * D_MODEL

kernel_name = 'hybrid_sgu_moba_hiermoe_block'


def rms_norm(x, g):
    xf = x.astype(jnp.float32)
    y = xf * lax.rsqrt(jnp.mean(xf * xf, axis=-1, keepdims=True) + EPS)
    return (y * g.astype(jnp.float32)).astype(x.dtype)


def layer_norm(x, g, b):
    xf = x.astype(jnp.float32)
    mu = jnp.mean(xf, axis=-1, keepdims=True)
    var = jnp.mean(jnp.square(xf - mu), axis=-1, keepdims=True)
    y = (xf - mu) * lax.rsqrt(var + EPS)
    return (y * g.astype(jnp.float32) + b.astype(jnp.float32)).astype(x.dtype)


def modulate(n, shift, scale):
    return n * (1.0 + scale[:, None, :]) + shift[:, None, :]


def rotary_tables(positions, dim):
    inv_freq = ROPE_THETA ** (-jnp.arange(0, dim, 2, dtype=jnp.float32) / dim)
    ang = positions.astype(jnp.float32)[..., None] * inv_freq
    return jnp.cos(ang), jnp.sin(ang)


def apply_rotary(x, cos, sin):
    x1, x2 = jnp.split(x, 2, axis=-1)
    c = cos[:, :, None, :].astype(x.dtype)
    s = sin[:, :, None, :].astype(x.dtype)
    return jnp.concatenate([x1 * c - x2 * s, x2 * c + x1 * s], axis=-1)


def spatial_gating(z, ln_g, ln_b, w_s, b_s):
    B, S, _ = z.shape
    u, v = jnp.split(z, 2, axis=-1)
    v = layer_norm(v, ln_g, ln_b)
    nc = S // SGU_CHUNK
    cg = SGU_WIDTH // SGU_GROUPS
    v = v.reshape(B, nc, SGU_CHUNK, SGU_GROUPS, cg)
    causal = jnp.tril(jnp.ones((SGU_CHUNK, SGU_CHUNK), dtype=bool))
    w = jnp.where(causal[None], w_s, 0)
    sv = jnp.einsum('gij,bcjgd->bcigd', w, v) + b_s.T[None, None, :, :, None]
    return u * sv.reshape(B, S, SGU_WIDTH)


def moba_attention(q, k, v):
    B, S, H, dh = q.shape
    s_pad = -(-S // MOBA_BLOCK) * MOBA_BLOCK
    pad = ((0, 0), (0, s_pad - S), (0, 0), (0, 0))
    q, k, v = jnp.pad(q, pad), jnp.pad(k, pad), jnp.pad(v, pad)
    nb = s_pad // MOBA_BLOCK
    n_top = min(MOBA_TOPK, nb)
    nq = s_pad // MOBA_Q_CHUNK
    scale = dh ** -0.5

    def heads_first(t):
        return t.transpose(0, 2, 1, 3).reshape(B * H, s_pad, dh)

    qh, kh, vh = heads_first(q), heads_first(k), heads_first(v)
    kb = kh.reshape(B * H, nb, MOBA_BLOCK, dh)
    vb = vh.reshape(B * H, nb, MOBA_BLOCK, dh)
    k_mean = jnp.mean(kb.astype(jnp.float32), axis=2)
    gate = jnp.einsum('zsd,znd->zsn', qh.astype(jnp.float32), k_mean)
    q_blk = jnp.arange(s_pad) // MOBA_BLOCK
    past = jnp.arange(nb)[None, :] < q_blk[:, None]
    gate = jnp.where(past[None], gate, NEG_INF)
    _, top_idx = lax.top_k(gate, n_top)

    def one_head(args):
        q_z, idx_z, kb_z, vb_z = args

        def one_chunk(cargs):
            q_c, idx_c, ci = cargs
            start = ci * MOBA_Q_CHUNK
            q_pos = start + jnp.arange(MOBA_Q_CHUNK)
            blk = start // MOBA_BLOCK
            k_own = lax.dynamic_index_in_dim(kb_z, blk, 0, keepdims=False)
            v_own = lax.dynamic_index_in_dim(vb_z, blk, 0, keepdims=False)
            k_sel = kb_z[idx_c]
            v_sel = vb_z[idx_c]
            s_own = jnp.einsum('qd,kd->qk', q_c, k_own).astype(jnp.float32) * scale
            k_pos = blk * MOBA_BLOCK + jnp.arange(MOBA_BLOCK)
            s_own = jnp.where(k_pos[None, :] <= q_pos[:, None], s_own, NEG_INF)
            s_sel = jnp.einsum('qd,qnkd->qnk', q_c, k_sel).astype(jnp.float32) * scale
            s_sel = jnp.where((idx_c < blk)[:, :, None], s_sel, NEG_INF)
            s_all = jnp.concatenate([s_own, s_sel.reshape(MOBA_Q_CHUNK, n_top * MOBA_BLOCK)], axis=-1)
            p = jax.nn.softmax(s_all, axis=-1).astype(q_c.dtype)
            p_own = p[:, :MOBA_BLOCK]
            p_sel = p[:, MOBA_BLOCK:].reshape(MOBA_Q_CHUNK, n_top, MOBA_BLOCK)
            return jnp.einsum('qk,kd->qd', p_own, v_own) + jnp.einsum('qnk,qnkd->qd', p_sel, v_sel)

        return lax.map(one_chunk, (q_z, idx_z, jnp.arange(nq)))

    out = lax.map(one_head, (qh.reshape(B * H, nq, MOBA_Q_CHUNK, dh),
                             top_idx.reshape(B * H, nq, MOBA_Q_CHUNK, n_top), kb, vb))
    out = out.reshape(B, H, s_pad, dh)[:, :, :S].transpose(0, 2, 1, 3)
    return out.reshape(B, S, H * dh)


def hierarchical_moe(n, w_rg, b_rg, w_re, b_re, w_gate, w_up, w_down):
    B, S, D = n.shape
    T = B * S
    xt = n.reshape(T, D)
    tok = jnp.arange(T)
    g_logits = (xt @ w_rg + b_rg).astype(jnp.float32)
    g_prob = jax.nn.softmax(g_logits, axis=-1)
    g_sel = jnp.argmax(g_logits, axis=-1).astype(jnp.int32)
    p_group = g_prob[tok, g_sel][:, None]
    e_logits = ((xt @ w_re).reshape(T, MOE_GROUPS, MOE_EXPERTS_PER_GROUP) + b_re).astype(jnp.float32)
    e_logits = e_logits[tok, g_sel]
    top_val, top_loc = lax.top_k(e_logits, MOE_TOPK)
    w_tok = jax.nn.softmax(top_val, axis=-1) * p_group
    expert_id = (g_sel[:, None] * MOE_EXPERTS_PER_GROUP + top_loc).reshape(-1)
    token_id = jnp.repeat(tok, MOE_TOPK)
    weight = w_tok.reshape(-1)
    A = T * MOE_TOPK
    cap = -(-(A + MOE_EXPERTS * MOE_ROW_BLOCK) // MOE_ROW_BLOCK) * MOE_ROW_BLOCK
    n_blocks = cap // MOE_ROW_BLOCK
    order = jnp.argsort(expert_id)
    e_sorted = expert_id[order]
    counts = jnp.bincount(expert_id, length=MOE_EXPERTS)
    padded = (counts + MOE_ROW_BLOCK - 1) // MOE_ROW_BLOCK * MOE_ROW_BLOCK
    start = jnp.cumsum(counts) - counts
    end_padded = jnp.cumsum(padded)
    start_padded = end_padded - padded
    dest = start_padded[e_sorted] + (jnp.arange(A) - start[e_sorted])
    row_token = jnp.full((cap,), T, dtype=jnp.int32).at[dest].set(token_id[order].astype(jnp.int32))
    row_weight = jnp.zeros((cap,), weight.dtype).at[dest].set(weight[order])
    block_expert = jnp.minimum(
        jnp.searchsorted(end_padded, jnp.arange(n_blocks) * MOE_ROW_BLOCK, side='right'),
        MOE_EXPERTS - 1)
    x_pad = jnp.concatenate([xt, jnp.zeros((1, D), xt.dtype)], axis=0)
    rows = x_pad[row_token].reshape(n_blocks, MOE_ROW_BLOCK, D)

    def expert_block(args):
        xb, e = args
        hid = jax.nn.silu(xb @ w_gate[e]) * (xb @ w_up[e])
        return hid @ w_down[e]

    y_rows = lax.map(expert_block, (rows, block_expert)).reshape(cap, D)
    y = jnp.zeros((T + 1, D), y_rows.dtype).at[row_token].add(
        y_rows * row_weight[:, None].astype(y_rows.dtype))
    return y[:T].reshape(B, S, D)


def setup_inputs(seed: int = 0) -> dict:
    key = jax.random.key(seed)
    ks = jax.random.split(key, 32)
    f32 = jnp.float32
    L, D = DEPTH, D_MODEL

    def nrm(k, shape, scale):
        return jax.random.normal(k, shape, f32) * scale

    x = nrm(ks[0], (BATCH, SEQ, D), 1.0)
    c = nrm(ks[1], (BATCH, D), 1.0)
    offset = jax.random.randint(ks[2], (BATCH, 1), 0, 1024, dtype=jnp.int32)
    positions = offset + jnp.arange(SEQ, dtype=jnp.int32)[None, :]
    return {
        'x': x,
        'c': c,
        'positions': positions,
        'w_ada': nrm(ks[3], (L, D, 6 * D), 0.5 * D ** -0.5),
        'b_ada': nrm(ks[4], (L, 6 * D), 0.02),
        'norm_mix_g': 1.0 + nrm(ks[5], (L, D), 0.02),
        'w_in': nrm(ks[6], (L, D, IN_COLS), D ** -0.5),
        'sgu_ln_g': 1.0 + nrm(ks[7], (L, SGU_WIDTH), 0.02),
        'sgu_ln_b': nrm(ks[8], (L, SGU_WIDTH), 0.02),
        'sgu_w_s': nrm(ks[9], (L, SGU_GROUPS, SGU_CHUNK, SGU_CHUNK), SGU_CHUNK ** -0.5),
        'sgu_b_s': 1.0 + nrm(ks[10], (L, SGU_GROUPS, SGU_CHUNK), 0.02),
        'w_sgu_out': nrm(ks[11], (L, SGU_WIDTH, D), SGU_WIDTH ** -0.5),
        'w_moba_out': nrm(ks[12], (L, MOBA_WIDTH, D), MOBA_WIDTH ** -0.5),
        'w_out': nrm(ks[13], (L, D, D), D ** -0.5),
        'norm_ffn_g': 1.0 + nrm(ks[14], (L, D), 0.02),
        'w_route_group': nrm(ks[15], (L, D, MOE_GROUPS), D ** -0.5),
        'b_route_group': nrm(ks[16], (L, MOE_GROUPS), 0.01),
        'w_route_expert': nrm(ks[17], (L, D, MOE_EXPERTS), D ** -0.5),
        'b_route_expert': nrm(ks[18], (L, MOE_GROUPS, MOE_EXPERTS_PER_GROUP), 0.01),
        'w_exp_gate': nrm(ks[19], (L, MOE_EXPERTS, D, MOE_D_FF), D ** -0.5),
        'w_exp_up': nrm(ks[20], (L, MOE_EXPERTS, D, MOE_D_FF), D ** -0.5),
        'w_exp_down': nrm(ks[21], (L, MOE_EXPERTS, MOE_D_FF, D), MOE_D_FF ** -0.5),
        'norm_final_g': 1.0 + nrm(ks[22], (D,), 0.02),
    }


def reference(x, c, positions, w_ada, b_ada, norm_mix_g, w_in, sgu_ln_g, sgu_ln_b, sgu_w_s,
              sgu_b_s, w_sgu_out, w_moba_out, w_out, norm_ffn_g, w_route_group, b_route_group,
              w_route_expert, b_route_expert, w_exp_gate, w_exp_up, w_exp_down, norm_final_g):
    B, S, D = x.shape
    cos, sin = rotary_tables(positions, MOBA_HEAD_DIM)
    c_act = jax.nn.silu(c)
    splits = [2 * SGU_WIDTH, 2 * SGU_WIDTH + MOBA_WIDTH, 2 * SGU_WIDTH + 2 * MOBA_WIDTH,
              2 * SGU_WIDTH + 3 * MOBA_WIDTH, 2 * SGU_WIDTH + 3 * MOBA_WIDTH + D_MODEL]
    h = x
    for l in range(DEPTH):
        ada = c_act @ w_ada[l] + b_ada[l]
        sh_m, sc_m, g_m, sh_f, sc_f, g_f = jnp.split(ada, 6, axis=-1)
        n = modulate(rms_norm(h, norm_mix_g[l]), sh_m, sc_m)
        proj = n @ w_in[l]
        z_sgu, q, k, v, gate_a, gate_b = jnp.split(proj, splits, axis=-1)
        y_a = spatial_gating(jax.nn.gelu(z_sgu), sgu_ln_g[l], sgu_ln_b[l],
                             sgu_w_s[l], sgu_b_s[l]) @ w_sgu_out[l]
        q = apply_rotary(q.reshape(B, S, MOBA_HEADS, MOBA_HEAD_DIM), cos, sin)
        k = apply_rotary(k.reshape(B, S, MOBA_HEADS, MOBA_HEAD_DIM), cos, sin)
        v = v.reshape(B, S, MOBA_HEADS, MOBA_HEAD_DIM)
        y_b = moba_attention(q, k, v) @ w_moba_out[l]
        merged = jax.nn.sigmoid(gate_a) * y_a + jax.nn.sigmoid(gate_b) * y_b
        h = h + g_m[:, None, :] * (merged @ w_out[l])
        n2 = modulate(rms_norm(h, norm_ffn_g[l]), sh_f, sc_f)
        h = h + g_f[:, None, :] * hierarchical_moe(
            n2, w_route_group[l], b_route_group[l], w_route_expert[l], b_route_expert[l],
            w_exp_gate[l], w_exp_up[l], w_exp_down[l])
    return rms_norm(h, norm_final_g)
```

```python
import functools

import jax
import jax.numpy as jnp
from jax import lax
from jax.experimental import pallas as pl
from jax.experimental.pallas import tpu as pltpu

MOBA_HEADS = 16
MOBA_BLOCK = 256
MOBA_TOPK = 3
SGU_CHUNK = 128
ROPE_THETA = 10000.0
MOE_GROUPS = 4
MOE_TOPK = 2
EPS = 1e-6
NEG_INF = -1e30

LANES = 128
EXPERT_ROW_BLOCK = 256
ROUTER_ROWS = 40
VMEM_LIMIT = 56 * 1024 * 1024

_F32 = jnp.float32
_BF16 = jnp.bfloat16
_NT = (((1,), (1,)), ((), ()))


def _params(semantics, vmem=VMEM_LIMIT):
    return pltpu.CompilerParams(dimension_semantics=semantics, vmem_limit_bytes=vmem)


def _split_bf16(a):
    hi = a.astype(_BF16)
    lo = (a - hi.astype(_F32)).astype(_BF16)
    return hi, lo


def _ada_kernel(c_ref, w_ref, b_ref, o_ref):
    ca_hi, ca_lo = _split_bf16(jax.nn.silu(c_ref[...]))
    w_hi, w_lo = _split_bf16(w_ref[...])
    acc = jnp.dot(ca_hi, w_hi, preferred_element_type=_F32)
    acc += jnp.dot(ca_hi, w_lo, preferred_element_type=_F32)
    acc += jnp.dot(ca_lo, w_hi, preferred_element_type=_F32)
    o_ref[...] = acc + b_ref[...]


def _ada(c_pad, w_ada, b_ada, tn):
    rows, d = c_pad.shape
    n = w_ada.shape[1]
    return pl.pallas_call(
        _ada_kernel,
        grid=(n // tn,),
        in_specs=[pl.BlockSpec((rows, d), lambda j: (0, 0)),
                  pl.BlockSpec((d, tn), lambda j: (0, j)),
                  pl.BlockSpec((1, tn), lambda j: (0, j))],
        out_specs=pl.BlockSpec((rows, tn), lambda j: (0, j)),
        out_shape=jax.ShapeDtypeStruct((rows, n), _F32),
        compiler_params=_params(("arbitrary",)),
        name="ada",
    )(c_pad, w_ada, b_ada.reshape(1, n))


def _rms_modulate(x, g, sc, sh):
    y = x * lax.rsqrt(jnp.mean(x * x, axis=-1, keepdims=True) + EPS) * g
    return y * (1.0 + sc) + sh


def _inproj_kernel(x_ref, g_ref, sc_ref, sh_ref, w_ref, cos_ref, sin_ref, lng_ref, lnb_ref,
                   o_ref, n_scr, *, q0, k0, v0, dh, scale):
    j = pl.program_id(1)

    @pl.when(j == 0)
    def _():
        n_scr[...] = _rms_modulate(x_ref[...], g_ref[...], sc_ref[0], sh_ref[0]).astype(_BF16)

    acc = jnp.dot(n_scr[...], w_ref[...], preferred_element_type=_F32)

    @pl.when(j == 0)
    def _():
        o_ref[...] = jax.nn.gelu(acc).astype(_BF16)

    @pl.when(j == 1)
    def _():
        z = jax.nn.gelu(acc)
        mu = jnp.mean(z, axis=-1, keepdims=True)
        zc = z - mu
        var = jnp.mean(zc * zc, axis=-1, keepdims=True)
        o_ref[...] = (zc * lax.rsqrt(var + EPS) * lng_ref[...] + lnb_ref[...]).astype(_BF16)

    def rope(mult):
        cos = cos_ref[...]
        sin = sin_ref[...]
        for h in range(acc.shape[1] // dh):
            xh = acc[:, h * dh:(h + 1) * dh]
            rot = xh * cos + pltpu.roll(xh, dh // 2, 1) * sin
            o_ref[:, h * dh:(h + 1) * dh] = (rot * mult).astype(_BF16)

    @pl.when((j >= q0) & (j < k0))
    def _():
        rope(scale)

    @pl.when((j >= k0) & (j < v0))
    def _():
        rope(1.0)

    @pl.when(j >= v0)
    def _():
        o_ref[...] = acc.astype(_BF16)


def _inproj(x2, ada3, norm_g, w_in, cos, sin, ln_g, ln_b, *, seq, sw, mw, dh, tm):
    t, d = x2.shape
    n = w_in.shape[1]
    tn = sw
    per_b = seq // tm
    q0 = 2
    k0 = q0 + mw // tn
    v0 = k0 + mw // tn
    kern = functools.partial(_inproj_kernel, q0=q0, k0=k0, v0=v0, dh=dh, scale=dh ** -0.5)
    return pl.pallas_call(
        kern,
        grid=(t // tm, n // tn),
        in_specs=[pl.BlockSpec((tm, d), lambda i, j: (i, 0)),
                  pl.BlockSpec((1, d), lambda i, j: (0, 0)),
                  pl.BlockSpec((1, 1, d), lambda i, j: ((i // per_b) * 6 + 1, 0, 0)),
                  pl.BlockSpec((1, 1, d), lambda i, j: ((i // per_b) * 6 + 0, 0, 0)),
                  pl.BlockSpec((d, tn), lambda i, j: (0, j)),
                  pl.BlockSpec((tm, dh), lambda i, j: (i, 0)),
                  pl.BlockSpec((tm, dh), lambda i, j: (i, 0)),
                  pl.BlockSpec((1, sw), lambda i, j: (0, 0)),
                  pl.BlockSpec((1, sw), lambda i, j: (0, 0))],
        out_specs=pl.BlockSpec((tm, tn), lambda i, j: (i, j)),
        out_shape=jax.ShapeDtypeStruct((t, n), _BF16),
        scratch_shapes=[pltpu.VMEM((tm, d), _BF16)],
        compiler_params=_params(("arbitrary", "arbitrary")),
        name="inproj",
    )(x2, norm_g.reshape(1, d), ada3, ada3, w_in, cos, sin, ln_g.reshape(1, sw), ln_b.reshape(1, sw))


def _sgu_kernel(u_ref, v_ref, ga_ref, ws_ref, bs_ref, wo_ref, o_ref, gated_scr, *, groups, cg):
    c = SGU_CHUNK
    tri = lax.broadcasted_iota(jnp.int32, (c, c), 0) >= lax.broadcasted_iota(jnp.int32, (c, c), 1)
    for g in range(groups):
        wg = jnp.where(tri, ws_ref[g], 0.0).astype(_BF16)
        cols = slice(g * cg, (g + 1) * cg)
        for ci in range(u_ref.shape[0] // c):
            rows = slice(ci * c, (ci + 1) * c)
            sv = jnp.dot(wg, v_ref[rows, cols], preferred_element_type=_F32) + bs_ref[:, cols]
            gated_scr[rows, cols] = (u_ref[rows, cols].astype(_F32) * sv).astype(_BF16)
    ya = jnp.dot(gated_scr[...], wo_ref[...], preferred_element_type=_F32)
    o_ref[...] = (jax.nn.sigmoid(ga_ref[...].astype(_F32)) * ya).astype(_BF16)


def _sgu(proj, w_s, bs_wide, w_sgu_out, *, sw, mw, tm):
    t = proj.shape[0]
    d = w_sgu_out.shape[1]
    groups = w_s.shape[0]
    ga_blk = (2 * sw + 3 * mw) // d
    kern = functools.partial(_sgu_kernel, groups=groups, cg=sw // groups)
    return pl.pallas_call(
        kern,
        grid=(t // tm,),
        in_specs=[pl.BlockSpec((tm, sw), lambda i: (i, 0)),
                  pl.BlockSpec((tm, sw), lambda i: (i, 1)),
                  pl.BlockSpec((tm, d), lambda i: (i, ga_blk)),
                  pl.BlockSpec(w_s.shape, lambda i: (0, 0, 0)),
                  pl.BlockSpec(bs_wide.shape, lambda i: (0, 0)),
                  pl.BlockSpec(w_sgu_out.shape, lambda i: (0, 0))],
        out_specs=pl.BlockSpec((tm, d), lambda i: (i, 0)),
        out_shape=jax.ShapeDtypeStruct((t, d), _BF16),
        scratch_shapes=[pltpu.VMEM((tm, sw), _BF16)],
        compiler_params=_params(("arbitrary",)),
        name="sgu",
    )(proj, proj, proj, w_s, bs_wide, w_sgu_out)


def _moba_kernel(q_ref, k_ref, v_ref, o_ref, kmh_scr, kml_scr, vt_scr, sel_scr, *, nb, topk):
    qi = pl.program_id(2)
    blk = MOBA_BLOCK
    tq = q_ref.shape[0]
    dh = q_ref.shape[1]

    @pl.when(qi == 0)
    def _():
        kf = k_ref[...].astype(_F32).reshape(nb, blk, dh)
        km_hi, km_lo = _split_bf16(jnp.mean(kf, axis=1))
        kmh_scr[...] = km_hi
        kml_scr[...] = km_lo
        for c in range(nb):
            vt_scr[:, c * blk:(c + 1) * blk] = v_ref[c * blk:(c + 1) * blk, :].astype(_F32).T.astype(_BF16)

    q = q_ref[...]

    gate = (lax.dot_general(kmh_scr[...], q, _NT, preferred_element_type=_F32)
            + lax.dot_general(kml_scr[...], q, _NT, preferred_element_type=_F32))
    row = lax.broadcasted_iota(jnp.int32, (nb, tq), 0)
    past = row < qi
    gm = jnp.where(past, gate, NEG_INF)
    beaten = jnp.zeros((nb, tq), jnp.int32)
    for jp in range(nb):
        other = gm[jp:jp + 1, :]
        wins = (other > gm) | ((other == gm) & (jp < row))
        beaten += wins.astype(jnp.int32)
    sel_scr[...] = (past & (beaten < topk)).astype(_F32)

    own = pl.multiple_of(qi * blk, blk)
    s = lax.dot_general(k_ref[pl.ds(own, blk), :], q, _NT, preferred_element_type=_F32)
    kpos = lax.broadcasted_iota(jnp.int32, (blk, tq), 0)
    qpos = lax.broadcasted_iota(jnp.int32, (blk, tq), 1)
    s = jnp.where(kpos <= qpos, s, NEG_INF)
    m0 = jnp.max(s, axis=0, keepdims=True)
    p = jnp.exp(s - m0)
    l0 = jnp.sum(p, axis=0, keepdims=True)
    acc0 = jnp.dot(vt_scr[:, pl.ds(own, blk)], p.astype(_BF16), preferred_element_type=_F32)

    def body(j, carry):
        m, l, acc = carry
        start = pl.multiple_of(j * blk, blk)
        sj = lax.dot_general(k_ref[pl.ds(start, blk), :], q, _NT, preferred_element_type=_F32)
        sj = jnp.where(sel_scr[pl.ds(j, 1), :] > 0.0, sj, NEG_INF)
        m_new = jnp.maximum(m, jnp.max(sj, axis=0, keepdims=True))
        alpha = jnp.exp(m - m_new)
        pj = jnp.exp(sj - m_new)
        l = alpha * l + jnp.sum(pj, axis=0, keepdims=True)
        acc = alpha * acc + jnp.dot(vt_scr[:, pl.ds(start, blk)], pj.astype(_BF16),
                                    preferred_element_type=_F32)
        return m_new, l, acc

    _, l, acc = lax.fori_loop(0, qi, body, (m0, l0, acc0))
    o_ref[...] = (acc / l).T.astype(_BF16)


def _moba(proj, *, batch, seq, sw, mw, dh):
    t = proj.shape[0]
    heads = mw // dh
    nb = seq // MOBA_BLOCK
    tq = MOBA_BLOCK
    nq = seq // tq
    qc = 2 * sw // dh
    kc = qc + heads
    vc = kc + heads
    kern = functools.partial(_moba_kernel, nb=nb, topk=MOBA_TOPK)
    return pl.pallas_call(
        kern,
        grid=(batch, heads, nq),
        in_specs=[pl.BlockSpec((tq, dh), lambda b, h, i: (b * nq + i, qc + h)),
                  pl.BlockSpec((seq, dh), lambda b, h, i: (b, kc + h)),
                  pl.BlockSpec((seq, dh), lambda b, h, i: (b, vc + h))],
        out_specs=pl.BlockSpec((tq, dh), lambda b, h, i: (b * nq + i, h)),
        out_shape=jax.ShapeDtypeStruct((t, mw), _BF16),
        scratch_shapes=[pltpu.VMEM((nb, dh), _BF16), pltpu.VMEM((nb, dh), _BF16),
                        pltpu.VMEM((dh, seq), _BF16), pltpu.VMEM((nb, tq), _F32)],
        compiler_params=_params(("arbitrary", "arbitrary", "arbitrary")),
        name="moba",
    )(proj, proj, proj)


def _mixout_kernel(attn_ref, ma_ref, gb_ref, x_ref, gm_ref, shf_ref, scf_ref, ng_ref,
                   wmo_ref, wo_ref, wrh_ref, wrl_ref, br_ref,
                   h_ref, n2_ref, eid_ref, wt_ref, *, groups, epg):
    yb = jnp.dot(attn_ref[...], wmo_ref[...], preferred_element_type=_F32)
    merged = ma_ref[...].astype(_F32) + jax.nn.sigmoid(gb_ref[...].astype(_F32)) * yb
    mix = jnp.dot(merged.astype(_BF16), wo_ref[...], preferred_element_type=_F32)
    h = x_ref[...] + gm_ref[0] * mix
    h_ref[...] = h
    n2 = _rms_modulate(h, ng_ref[...], scf_ref[0], shf_ref[0])
    n2_ref[...] = n2

    n_hi, n_lo = _split_bf16(n2)
    lg = (lax.dot_general(wrh_ref[...], n_hi, _NT, preferred_element_type=_F32)
          + lax.dot_general(wrh_ref[...], n_lo, _NT, preferred_element_type=_F32)
          + lax.dot_general(wrl_ref[...], n_hi, _NT, preferred_element_type=_F32)) + br_ref[:, 0:1]

    gl = [lg[g:g + 1, :] for g in range(groups)]
    gmax = functools.reduce(jnp.maximum, gl)
    denom = functools.reduce(jnp.add, [jnp.exp(v - gmax) for v in gl])
    p_group = 1.0 / denom
    g_sel = jnp.full(gmax.shape, groups - 1, jnp.int32)
    for g in range(groups - 2, -1, -1):
        g_sel = jnp.where(gl[g] == gmax, g, g_sel)

    el = []
    for e in range(epg):
        v = lg[groups + e:groups + e + 1, :]
        for g in range(1, groups):
            r = groups + g * epg + e
            v = jnp.where(g_sel == g, lg[r:r + 1, :], v)
        el.append(v)

    def top1(vals):
        vmax = functools.reduce(jnp.maximum, vals)
        idx = jnp.full(vmax.shape, epg - 1, jnp.int32)
        for e in range(epg - 2, -1, -1):
            idx = jnp.where(vals[e] == vmax, e, idx)
        return vmax, idx

    v1, i1 = top1(el)
    v2, i2 = top1([jnp.where(i1 == e, -jnp.inf, el[e]) for e in range(epg)])
    b = jnp.exp(v2 - v1)
    eid_ref[0:1, :] = g_sel * epg + i1
    eid_ref[1:2, :] = g_sel * epg + i2
    wt_ref[0:1, :] = (1.0 / (1.0 + b)) * p_group
    wt_ref[1:2, :] = (b / (1.0 + b)) * p_group


def _mixout(attn, m_a, proj, x2, ada3, norm_g, w_moba_out, w_out, wr_hi, wr_lo, br,
            *, seq, sw, mw, groups, epg, tm):
    t, d = x2.shape
    per_b = seq // tm
    gb_blk = (2 * sw + 3 * mw) // d + 1
    once = pl.Buffered(1)
    kern = functools.partial(_mixout_kernel, groups=groups, epg=epg)

    def ada_spec(k):
        return pl.BlockSpec((1, 1, d), lambda i: ((i // per_b) * 6 + k, 0, 0))

    return pl.pallas_call(
        kern,
        grid=(t // tm,),
        in_specs=[pl.BlockSpec((tm, mw), lambda i: (i, 0)),
                  pl.BlockSpec((tm, d), lambda i: (i, 0)),
                  pl.BlockSpec((tm, d), lambda i: (i, gb_blk)),
                  pl.BlockSpec((tm, d), lambda i: (i, 0)),
                  ada_spec(2), ada_spec(3), ada_spec(4),
                  pl.BlockSpec((1, d), lambda i: (0, 0)),
                  pl.BlockSpec((mw, d), lambda i: (0, 0), pipeline_mode=once),
                  pl.BlockSpec((d, d), lambda i: (0, 0), pipeline_mode=once),
                  pl.BlockSpec((ROUTER_ROWS, d), lambda i: (0, 0)),
                  pl.BlockSpec((ROUTER_ROWS, d), lambda i: (0, 0)),
                  pl.BlockSpec((ROUTER_ROWS, LANES), lambda i: (0, 0))],
        out_specs=[pl.BlockSpec((tm, d), lambda i: (i, 0)),
                   pl.BlockSpec((tm, d), lambda i: (i, 0)),
                   pl.BlockSpec((MOE_TOPK, tm), lambda i: (0, i)),
                   pl.BlockSpec((MOE_TOPK, tm), lambda i: (0, i))],
        out_shape=[jax.ShapeDtypeStruct((t, d), _F32),
                   jax.ShapeDtypeStruct((t, d), _F32),
                   jax.ShapeDtypeStruct((MOE_TOPK, t), jnp.int32),
                   jax.ShapeDtypeStruct((MOE_TOPK, t), _F32)],
        compiler_params=_params(("arbitrary",)),
        name="mixout",
    )(attn, m_a, proj, x2, ada3, ada3, ada3, norm_g.reshape(1, d), w_moba_out, w_out, wr_hi, wr_lo, br)


def _plan_kernel(eid_ref, rank_ref, cnt_ref, carry_scr, tri_scr, *, n_exp):
    tc = eid_ref.shape[1]

    @pl.when(pl.program_id(0) == 0)
    def _():
        carry_scr[...] = jnp.zeros_like(carry_scr)
        earlier = lax.broadcasted_iota(jnp.int32, (tc, tc), 0) < lax.broadcasted_iota(jnp.int32, (tc, tc), 1)
        tri_scr[...] = earlier.astype(_BF16)

    eio = lax.broadcasted_iota(jnp.int32, (n_exp, tc), 0)
    oh0 = (eio == eid_ref[0:1, :]).astype(_F32)
    oh1 = (eio == eid_ref[1:2, :]).astype(_F32)
    oh = oh0 + oh1
    before = jnp.dot(oh.astype(_BF16), tri_scr[...], preferred_element_type=_F32)
    base = carry_scr[:, 0:1] + before
    rank_ref[0:1, :] = jnp.sum(oh0 * base, axis=0, keepdims=True).astype(jnp.int32)
    rank_ref[1:2, :] = jnp.sum(oh1 * base, axis=0, keepdims=True).astype(jnp.int32)
    carry_scr[...] = carry_scr[...] + jnp.sum(oh, axis=1, keepdims=True)
    cnt_ref[...] = carry_scr[...]


def _plan(eid, *, n_exp, tc):
    t = eid.shape[1]
    kern = functools.partial(_plan_kernel, n_exp=n_exp)
    return pl.pallas_call(
        kern,
        grid=(t // tc,),
        in_specs=[pl.BlockSpec((MOE_TOPK, tc), lambda i: (0, i))],
        out_specs=[pl.BlockSpec((MOE_TOPK, tc), lambda i: (0, i)),
                   pl.BlockSpec((n_exp, LANES), lambda i: (0, 0))],
        out_shape=[jax.ShapeDtypeStruct((MOE_TOPK, t), jnp.int32),
                   jax.ShapeDtypeStruct((n_exp, LANES), _F32)],
        scratch_shapes=[pltpu.VMEM((n_exp, LANES), _F32), pltpu.VMEM((tc, tc), _BF16)],
        compiler_params=_params(("arbitrary",)),
        name="plan",
    )(eid)


def _row_copy(src, src_row, dst, dst_row, sem):
    return pltpu.make_async_copy(src.at[pl.ds(src_row, 1)], dst.at[pl.ds(dst_row, 1)], sem)


def _scatter_kernel(dest_ref, n2_ref, xs_in_ref, xs_ref, sem):
    del xs_in_ref
    ts = dest_ref.shape[1]
    base = pl.program_id(0) * ts

    def issue(r, c):
        for k in range(MOE_TOPK):
            _row_copy(n2_ref, base + r, xs_ref, dest_ref[k, r], sem).start()
        return c

    lax.fori_loop(0, ts, issue, 0)

    def drain(r, c):
        for k in range(MOE_TOPK):
            _row_copy(n2_ref, 0, xs_ref, 0, sem).wait()
        return c

    lax.fori_loop(0, ts, drain, 0)


def _scatter(dest3, n2, xs_zero):
    steps, _, ts = dest3.shape
    return pl.pallas_call(
        _scatter_kernel,
        grid=(steps,),
        in_specs=[pl.BlockSpec((None, MOE_TOPK, ts), lambda i: (i, 0, 0), memory_space=pltpu.SMEM),
                  pl.BlockSpec(memory_space=pl.ANY),
                  pl.BlockSpec(memory_space=pl.ANY)],
        out_specs=pl.BlockSpec(memory_space=pl.ANY),
        out_shape=jax.ShapeDtypeStruct(xs_zero.shape, xs_zero.dtype),
        scratch_shapes=[pltpu.SemaphoreType.DMA(())],
        input_output_aliases={2: 0},
        compiler_params=_params(("arbitrary",)),
        name="scatter",
    )(dest3, n2, xs_zero)


def _expert_kernel(be_ref, nu_ref, xs_ref, wg_ref, wu_ref, wd_ref, y_ref, wg_s, wu_s, wd_s):
    i = pl.program_id(0)
    used = i < nu_ref[0]
    fresh = (i == 0) | (be_ref[i] != be_ref[jnp.maximum(i - 1, 0)])

    @pl.when(used & fresh)
    def _():
        wg_s[...] = wg_ref[0].astype(_BF16)
        wu_s[...] = wu_ref[0].astype(_BF16)
        wd_s[...] = wd_ref[0].astype(_BF16)

    @pl.when(used)
    def _():
        xb = xs_ref[...].astype(_BF16)
        gate = jnp.dot(xb, wg_s[...], preferred_element_type=_F32)
        up = jnp.dot(xb, wu_s[...], preferred_element_type=_F32)
        hid = (jax.nn.silu(gate) * up).astype(_BF16)
        y_ref[...] = jnp.dot(hid, wd_s[...], preferred_element_type=_F32)

    @pl.when(jnp.logical_not(used))
    def _():
        y_ref[...] = jnp.zeros_like(y_ref)


def _experts(block_expert, n_used, xs, w_gate, w_up, w_down):
    cap, d = xs.shape
    f = w_gate.shape[2]
    rb = EXPERT_ROW_BLOCK
    grid_spec = pltpu.PrefetchScalarGridSpec(
        num_scalar_prefetch=2,
        grid=(cap // rb,),
        in_specs=[pl.BlockSpec((rb, d), lambda i, be, nu: (jnp.minimum(i, jnp.maximum(nu[0] - 1, 0)), 0)),
                  pl.BlockSpec((1, d, f), lambda i, be, nu: (be[i], 0, 0)),
                  pl.BlockSpec((1, d, f), lambda i, be, nu: (be[i], 0, 0)),
                  pl.BlockSpec((1, f, d), lambda i, be, nu: (be[i], 0, 0))],
        out_specs=pl.BlockSpec((rb, d), lambda i, be, nu: (i, 0)),
        scratch_shapes=[pltpu.VMEM((d, f), _BF16), pltpu.VMEM((d, f), _BF16), pltpu.VMEM((f, d), _BF16)],
    )
    return pl.pallas_call(
        _expert_kernel,
        grid_spec=grid_spec,
        out_shape=jax.ShapeDtypeStruct((cap, d), _F32),
        compiler_params=_params(("arbitrary",)),
        name="experts",
    )(block_expert, n_used, xs, w_gate, w_up, w_down)


def _combine_kernel(dest_ref, h_ref, wc_ref, gf_ref, ng_ref, y_ref, o_ref, ybuf, sem):
    tk = h_ref.shape[0]

    def issue(r, c):
        for k in range(MOE_TOPK):
            _row_copy(y_ref, dest_ref[k, r], ybuf.at[k], r, sem).start()
        return c

    lax.fori_loop(0, tk, issue, 0)

    def drain(r, c):
        for k in range(MOE_TOPK):
            _row_copy(y_ref, 0, ybuf.at[k], 0, sem).wait()
        return c

    lax.fori_loop(0, tk, drain, 0)

    moe = wc_ref[:, 0:1] * ybuf[0] + wc_ref[:, 1:2] * ybuf[1]
    h = h_ref[...] + gf_ref[0] * moe
    o_ref[...] = h * lax.rsqrt(jnp.mean(h * h, axis=-1, keepdims=True) + EPS) * ng_ref[...]


def _combine(dest3, h1, w_cols, ada3, norm_g, y_rows, *, seq):
    t, d = h1.shape
    steps, _, tk = dest3.shape
    per_b = seq // tk
    return pl.pallas_call(
        _combine_kernel,
        grid=(steps,),
        in_specs=[pl.BlockSpec((None, MOE_TOPK, tk), lambda i: (i, 0, 0), memory_space=pltpu.SMEM),
                  pl.BlockSpec((tk, d), lambda i: (i, 0)),
                  pl.BlockSpec((tk, LANES), lambda i: (i, 0)),
                  pl.BlockSpec((1, 1, d), lambda i: ((i // per_b) * 6 + 5, 0, 0)),
                  pl.BlockSpec((1, d), lambda i: (0, 0)),
                  pl.BlockSpec(memory_space=pl.ANY)],
        out_specs=pl.BlockSpec((tk, d), lambda i: (i, 0)),
        out_shape=jax.ShapeDtypeStruct((t, d), _F32),
        scratch_shapes=[pltpu.VMEM((MOE_TOPK, tk, d), _F32), pltpu.SemaphoreType.DMA(())],
        compiler_params=_params(("arbitrary",)),
        name="combine",
    )(dest3, h1, w_cols, ada3, norm_g.reshape(1, d), y_rows)


def _rotary_tables(positions, dh):
    inv_freq = ROPE_THETA ** (-jnp.arange(0, dh, 2, dtype=_F32) / dh)
    ang = positions.astype(_F32)[..., None] * inv_freq
    cos, sin = jnp.cos(ang), jnp.sin(ang)
    t = cos.shape[0] * cos.shape[1]
    cos = jnp.concatenate([cos, cos], axis=-1).reshape(t, dh)
    sin = jnp.concatenate([-sin, sin], axis=-1).reshape(t, dh)
    return cos, sin


def _tile(n, want):
    while n % want:
        want //= 2
    return want


def kernel(x, c, positions, w_ada, b_ada, norm_mix_g, w_in, sgu_ln_g, sgu_ln_b, sgu_w_s, sgu_b_s,
           w_sgu_out, w_moba_out, w_out, norm_ffn_g, w_route_group, b_route_group, w_route_expert,
           b_route_expert, w_exp_gate, w_exp_up, w_exp_down, norm_final_g):
    batch, seq, d = x.shape
    depth = w_ada.shape[0]
    t = batch * seq
    sw = sgu_ln_g.shape[1]
    mw = w_moba_out.shape[1]
    dh = mw // MOBA_HEADS
    groups = w_route_group.shape[2]
    n_exp = w_route_expert.shape[2]
    epg = n_exp // groups
    rb = EXPERT_ROW_BLOCK
    cap = (t * MOE_TOPK // rb + n_exp) * rb
    assert depth == 1, "the final RMSNorm is fused into the single layer's combine"
    assert dh == LANES and seq % MOBA_BLOCK == 0 and sw * 2 == d and mw % sw == 0
    assert groups + n_exp <= ROUTER_ROWS and (t * MOE_TOPK) % rb == 0

    cos, sin = _rotary_tables(positions, dh)
    c_pad = jnp.zeros((8, d), _F32).at[:batch].set(c)
    h = x.reshape(t, d)
    ts = _tile(seq, 256)

    for l in range(depth):
        ada = _ada(c_pad, w_ada[l], b_ada[l], _tile(6 * d, 1024))
        ada3 = ada[:batch].reshape(batch * 6, 1, d)

        proj = _inproj(h, ada3, norm_mix_g[l], w_in[l].astype(_BF16), cos, sin, sgu_ln_g[l], sgu_ln_b[l],
                       seq=seq, sw=sw, mw=mw, dh=dh, tm=_tile(seq, 1024))
        bs_wide = jnp.repeat(sgu_b_s[l].T, sw // sgu_w_s.shape[1], axis=1)
        m_a = _sgu(proj, sgu_w_s[l], bs_wide, w_sgu_out[l].astype(_BF16), sw=sw, mw=mw, tm=_tile(seq, 512))
        attn = _moba(proj, batch=batch, seq=seq, sw=sw, mw=mw, dh=dh)

        wr = jnp.concatenate([w_route_group[l], w_route_expert[l]], axis=1).T
        wr = jnp.zeros((ROUTER_ROWS, d), _F32).at[:groups + n_exp].set(wr)
        wr_hi, wr_lo = _split_bf16(wr)
        br = jnp.concatenate([b_route_group[l], b_route_expert[l].reshape(-1)])
        br = jnp.broadcast_to(jnp.zeros((ROUTER_ROWS,), _F32).at[:groups + n_exp].set(br)[:, None],
                              (ROUTER_ROWS, LANES))
        h1, n2, eid, wt = _mixout(attn, m_a, proj, h, ada3, norm_ffn_g[l], w_moba_out[l].astype(_BF16),
                                  w_out[l].astype(_BF16), wr_hi, wr_lo, br,
                                  seq=seq, sw=sw, mw=mw, groups=groups, epg=epg, tm=_tile(seq, 256))

        rank, cnt = _plan(eid, n_exp=n_exp, tc=_tile(t, 512))
        counts = cnt[:, 0].astype(jnp.int32)
        padded = (counts + rb - 1) // rb * rb
        end_padded = jnp.cumsum(padded)
        start_padded = end_padded - padded
        dest = rank + start_padded[eid]
        dest3 = dest.reshape(MOE_TOPK, t // ts, ts).transpose(1, 0, 2)
        block_expert = jnp.minimum(
            jnp.searchsorted(end_padded, jnp.arange(cap // rb, dtype=jnp.int32) * rb, side="right"),
            n_exp - 1).astype(jnp.int32)
        n_used = (end_padded[-1:] // rb).astype(jnp.int32)

        xs = _scatter(dest3, n2, jnp.zeros((cap, d), _F32))
        y_rows = _experts(block_expert, n_used, xs, w_exp_gate[l], w_exp_up[l], w_exp_down[l])
        w_cols = jnp.zeros((t, LANES), _F32).at[:, :MOE_TOPK].set(wt.T)
        h = _combine(dest3, h1, w_cols, ada3, norm_final_g, y_rows, seq=seq)

    return h.reshape(batch, seq, d)
```

```python
import functools

import jax
import jax.numpy as jnp
from jax import lax
from jax.experimental import pallas as pl
from jax.experimental.pallas import tpu as pltpu

MOBA_HEADS = 16
MOBA_BLOCK = 256
MOBA_TOPK = 3
SGU_CHUNK = 128
ROPE_THETA = 10000.0
MOE_GROUPS = 4
MOE_TOPK = 2
EPS = 1e-6
NEG_INF = -1e30

LANES = 128
EXPERT_ROW_BLOCK = 256
MOBA_HEAD_GROUP = 4
ROUTER_ROWS = 40
VMEM_LIMIT = 56 * 1024 * 1024

_F32 = jnp.float32
_BF16 = jnp.bfloat16
_NT = (((1,), (1,)), ((), ()))


def _params(semantics, vmem=VMEM_LIMIT):
    return pltpu.CompilerParams(dimension_semantics=semantics, vmem_limit_bytes=vmem)


def _split_bf16(a):
    hi = a.astype(_BF16)
    lo = (a - hi.astype(_F32)).astype(_BF16)
    return hi, lo


def _ada_kernel(c_ref, w_ref, b_ref, o_ref):
    ca_hi, ca_lo = _split_bf16(jax.nn.silu(c_ref[...]))
    w_hi, w_lo = _split_bf16(w_ref[...])
    acc = jnp.dot(ca_hi, w_hi, preferred_element_type=_F32)
    acc += jnp.dot(ca_hi, w_lo, preferred_element_type=_F32)
    acc += jnp.dot(ca_lo, w_hi, preferred_element_type=_F32)
    o_ref[...] = acc + b_ref[...]


def _ada(c_pad, w_ada, b_ada, tn):
    rows, d = c_pad.shape
    n = w_ada.shape[1]
    return pl.pallas_call(
        _ada_kernel,
        grid=(n // tn,),
        in_specs=[pl.BlockSpec((rows, d), lambda j: (0, 0)),
                  pl.BlockSpec((d, tn), lambda j: (0, j)),
                  pl.BlockSpec((1, tn), lambda j: (0, j))],
        out_specs=pl.BlockSpec((rows, tn), lambda j: (0, j)),
        out_shape=jax.ShapeDtypeStruct((rows, n), _F32),
        compiler_params=_params(("arbitrary",)),
        name="ada",
    )(c_pad, w_ada, b_ada.reshape(1, n))


def _rms_modulate(x, g, sc, sh):
    y = x * lax.rsqrt(jnp.mean(x * x, axis=-1, keepdims=True) + EPS) * g
    return y * (1.0 + sc) + sh


def _inproj_kernel(x_ref, g_ref, sc_ref, sh_ref, w_ref, cos_ref, sin_ref, lng_ref, lnb_ref,
                   o_ref, n_scr, *, q0, k0, v0, dh, scale):
    j = pl.program_id(1)

    @pl.when(j == 0)
    def _():
        n_scr[...] = _rms_modulate(x_ref[...], g_ref[...], sc_ref[0], sh_ref[0]).astype(_BF16)

    acc = jnp.dot(n_scr[...], w_ref[...], preferred_element_type=_F32)

    @pl.when(j == 0)
    def _():
        o_ref[...] = jax.nn.gelu(acc).astype(_BF16)

    @pl.when(j == 1)
    def _():
        z = jax.nn.gelu(acc)
        mu = jnp.mean(z, axis=-1, keepdims=True)
        zc = z - mu
        var = jnp.mean(zc * zc, axis=-1, keepdims=True)
        o_ref[...] = (zc * lax.rsqrt(var + EPS) * lng_ref[...] + lnb_ref[...]).astype(_BF16)

    def rope(mult):
        cos = cos_ref[...]
        sin = sin_ref[...]
        for h in range(acc.shape[1] // dh):
            xh = acc[:, h * dh:(h + 1) * dh]
            rot = xh * cos + pltpu.roll(xh, dh // 2, 1) * sin
            o_ref[:, h * dh:(h + 1) * dh] = (rot * mult).astype(_BF16)

    @pl.when((j >= q0) & (j < k0))
    def _():
        rope(scale)

    @pl.when((j >= k0) & (j < v0))
    def _():
        rope(1.0)

    @pl.when(j >= v0)
    def _():
        o_ref[...] = acc.astype(_BF16)


def _inproj(x2, ada3, norm_g, w_in, cos, sin, ln_g, ln_b, *, seq, sw, mw, dh, tm):
    t, d = x2.shape
    n = w_in.shape[1]
    tn = sw
    per_b = seq // tm
    q0 = 2
    k0 = q0 + mw // tn
    v0 = k0 + mw // tn
    kern = functools.partial(_inproj_kernel, q0=q0, k0=k0, v0=v0, dh=dh, scale=dh ** -0.5)
    return pl.pallas_call(
        kern,
        grid=(t // tm, n // tn),
        in_specs=[pl.BlockSpec((tm, d), lambda i, j: (i, 0)),
                  pl.BlockSpec((1, d), lambda i, j: (0, 0)),
                  pl.BlockSpec((1, 1, d), lambda i, j: ((i // per_b) * 6 + 1, 0, 0)),
                  pl.BlockSpec((1, 1, d), lambda i, j: ((i // per_b) * 6 + 0, 0, 0)),
                  pl.BlockSpec((d, tn), lambda i, j: (0, j)),
                  pl.BlockSpec((tm, dh), lambda i, j: (i, 0)),
                  pl.BlockSpec((tm, dh), lambda i, j: (i, 0)),
                  pl.BlockSpec((1, sw), lambda i, j: (0, 0)),
                  pl.BlockSpec((1, sw), lambda i, j: (0, 0))],
        out_specs=pl.BlockSpec((tm, tn), lambda i, j: (i, j)),
        out_shape=jax.ShapeDtypeStruct((t, n), _BF16),
        scratch_shapes=[pltpu.VMEM((tm, d), _BF16)],
        compiler_params=_params(("arbitrary", "arbitrary")),
        name="inproj",
    )(x2, norm_g.reshape(1, d), ada3, ada3, w_in, cos, sin, ln_g.reshape(1, sw), ln_b.reshape(1, sw))


def _sgu_kernel(u_ref, v_ref, ga_ref, ws_ref, bs_ref, wo_ref, o_ref, gated_scr, *, groups, cg):
    c = SGU_CHUNK
    tri = lax.broadcasted_iota(jnp.int32, (c, c), 0) >= lax.broadcasted_iota(jnp.int32, (c, c), 1)
    for g in range(groups):
        wg = jnp.where(tri, ws_ref[g], 0.0).astype(_BF16)
        cols = slice(g * cg, (g + 1) * cg)
        for ci in range(u_ref.shape[0] // c):
            rows = slice(ci * c, (ci + 1) * c)
            sv = jnp.dot(wg, v_ref[rows, cols], preferred_element_type=_F32) + bs_ref[:, cols]
            gated_scr[rows, cols] = (u_ref[rows, cols].astype(_F32) * sv).astype(_BF16)
    ya = jnp.dot(gated_scr[...], wo_ref[...], preferred_element_type=_F32)
    o_ref[...] = (jax.nn.sigmoid(ga_ref[...].astype(_F32)) * ya).astype(_BF16)


def _sgu(proj, w_s, bs_wide, w_sgu_out, *, sw, mw, tm):
    t = proj.shape[0]
    d = w_sgu_out.shape[1]
    groups = w_s.shape[0]
    ga_blk = (2 * sw + 3 * mw) // d
    kern = functools.partial(_sgu_kernel, groups=groups, cg=sw // groups)
    return pl.pallas_call(
        kern,
        grid=(t // tm,),
        in_specs=[pl.BlockSpec((tm, sw), lambda i: (i, 0)),
                  pl.BlockSpec((tm, sw), lambda i: (i, 1)),
                  pl.BlockSpec((tm, d), lambda i: (i, ga_blk)),
                  pl.BlockSpec(w_s.shape, lambda i: (0, 0, 0)),
                  pl.BlockSpec(bs_wide.shape, lambda i: (0, 0)),
                  pl.BlockSpec(w_sgu_out.shape, lambda i: (0, 0))],
        out_specs=pl.BlockSpec((tm, d), lambda i: (i, 0)),
        out_shape=jax.ShapeDtypeStruct((t, d), _BF16),
        scratch_shapes=[pltpu.VMEM((tm, sw), _BF16)],
        compiler_params=_params(("arbitrary",)),
        name="sgu",
    )(proj, proj, proj, w_s, bs_wide, w_sgu_out)


def _moba_kernel(q_ref, k_ref, v_ref, o_ref, kmh_scr, kml_scr, vt_scr, sel_scr,
                 s_scr, p_scr, acc_scr, m_scr, l_scr, a_scr, *, nb, topk, hg, dh):
    qi = pl.program_id(2)
    blk = MOBA_BLOCK
    tq = q_ref.shape[0]

    @pl.when(qi == 0)
    def _():
        for h in range(hg):
            cols = slice(h * dh, (h + 1) * dh)
            kf = k_ref[:, cols].astype(_F32).reshape(nb, blk, dh)
            km_hi, km_lo = _split_bf16(jnp.mean(kf, axis=1))
            kmh_scr[h * nb:(h + 1) * nb, :] = km_hi
            kml_scr[h * nb:(h + 1) * nb, :] = km_lo
            for c in range(nb):
                rows = slice(c * blk, (c + 1) * blk)
                vt_scr[cols, rows] = v_ref[rows, cols].astype(_F32).T.astype(_BF16)

    row = lax.broadcasted_iota(jnp.int32, (nb, tq), 0)
    past = row < qi
    kpos = lax.broadcasted_iota(jnp.int32, (blk, tq), 0)
    qpos = lax.broadcasted_iota(jnp.int32, (blk, tq), 1)
    own = pl.multiple_of(qi * blk, blk)

    def scores(start, h):
        cols = slice(h * dh, (h + 1) * dh)
        return lax.dot_general(k_ref[pl.ds(start, blk), cols], q_ref[:, cols], _NT,
                               preferred_element_type=_F32)

    def weighted_values(start, h, p):
        return jnp.dot(vt_scr[h * dh:(h + 1) * dh, pl.ds(start, blk)], p, preferred_element_type=_F32)

    for h in range(hg):
        hrows = slice(h * nb, (h + 1) * nb)
        q = q_ref[:, h * dh:(h + 1) * dh]
        gate = (lax.dot_general(kmh_scr[hrows, :], q, _NT, preferred_element_type=_F32)
                + lax.dot_general(kml_scr[hrows, :], q, _NT, preferred_element_type=_F32))
        gm = jnp.where(past, gate, NEG_INF)
        beaten = jnp.zeros((nb, tq), jnp.int32)
        for jp in range(nb):
            other = gm[jp:jp + 1, :]
            wins = (other > gm) | ((other == gm) & (jp < row))
            beaten += wins.astype(jnp.int32)
        sel_scr[hrows, :] = (past & (beaten < topk)).astype(_F32)

        s = jnp.where(kpos <= qpos, scores(own, h), NEG_INF)
        m0 = jnp.max(s, axis=0, keepdims=True)
        p = jnp.exp(s - m0)
        m_scr[h] = m0
        l_scr[h] = jnp.sum(p, axis=0, keepdims=True)
        a_scr[h] = jnp.ones_like(m0)
        acc_scr[h] = jnp.zeros(acc_scr.shape[1:], _F32)
        p_scr[0, h] = p.astype(_BF16)
        s_scr[0, h] = scores(0, h)

    def body(j, c):
        par = lax.rem(j, 2)
        prev = pl.multiple_of(jnp.where(j == 0, qi, j - 1) * blk, blk)
        nxt = pl.multiple_of(jnp.minimum(j + 1, qi - 1) * blk, blk)
        for h in range(hg):
            acc_scr[h] = a_scr[h] * acc_scr[h] + weighted_values(prev, h, p_scr[par, h])
        for h in range(hg):
            s = jnp.where(sel_scr[pl.ds(h * nb + j, 1), :] > 0.0, s_scr[par, h], NEG_INF)
            m = m_scr[h]
            m_new = jnp.maximum(m, jnp.max(s, axis=0, keepdims=True))
            alpha = jnp.exp(m - m_new)
            p = jnp.exp(s - m_new)
            l_scr[h] = alpha * l_scr[h] + jnp.sum(p, axis=0, keepdims=True)
            m_scr[h] = m_new
            a_scr[h] = alpha
            p_scr[1 - par, h] = p.astype(_BF16)
        for h in range(hg):
            s_scr[1 - par, h] = scores(nxt, h)
        return c

    lax.fori_loop(0, qi, body, 0)
    last = pl.multiple_of(jnp.where(qi == 0, qi, qi - 1) * blk, blk)
    for h in range(hg):
        acc = a_scr[h] * acc_scr[h] + weighted_values(last, h, p_scr[lax.rem(qi, 2), h])
        o_ref[:, h * dh:(h + 1) * dh] = (acc / l_scr[h]).T.astype(_BF16)


def _moba(proj, *, batch, seq, sw, mw, dh, hg):
    t = proj.shape[0]
    heads = mw // dh
    nb = seq // MOBA_BLOCK
    tq = MOBA_BLOCK
    nq = seq // tq
    gw = hg * dh
    qc = 2 * sw // gw
    kc = qc + heads // hg
    vc = kc + heads // hg
    kern = functools.partial(_moba_kernel, nb=nb, topk=MOBA_TOPK, hg=hg, dh=dh)
    return pl.pallas_call(
        kern,
        grid=(batch, heads // hg, nq),
        in_specs=[pl.BlockSpec((tq, gw), lambda b, h, i: (b * nq + i, qc + h)),
                  pl.BlockSpec((seq, gw), lambda b, h, i: (b, kc + h)),
                  pl.BlockSpec((seq, gw), lambda b, h, i: (b, vc + h))],
        out_specs=pl.BlockSpec((tq, gw), lambda b, h, i: (b * nq + i, h)),
        out_shape=jax.ShapeDtypeStruct((t, mw), _BF16),
        scratch_shapes=[pltpu.VMEM((hg * nb, dh), _BF16), pltpu.VMEM((hg * nb, dh), _BF16),
                        pltpu.VMEM((gw, seq), _BF16), pltpu.VMEM((hg * nb, tq), _F32),
                        pltpu.VMEM((2, hg, MOBA_BLOCK, tq), _F32), pltpu.VMEM((2, hg, MOBA_BLOCK, tq), _BF16),
                        pltpu.VMEM((hg, dh, tq), _F32), pltpu.VMEM((hg, 1, tq), _F32),
                        pltpu.VMEM((hg, 1, tq), _F32), pltpu.VMEM((hg, 1, tq), _F32)],
        compiler_params=_params(("arbitrary", "arbitrary", "arbitrary")),
        name="moba",
    )(proj, proj, proj)


def _mixout_kernel(attn_ref, ma_ref, gb_ref, x_ref, gm_ref, shf_ref, scf_ref, ng_ref,
                   wmo_ref, wo_ref, wrh_ref, wrl_ref, br_ref,
                   h_ref, n2_ref, eid_ref, wt_ref, *, groups, epg):
    yb = jnp.dot(attn_ref[...], wmo_ref[...], preferred_element_type=_F32)
    merged = ma_ref[...].astype(_F32) + jax.nn.sigmoid(gb_ref[...].astype(_F32)) * yb
    mix = jnp.dot(merged.astype(_BF16), wo_ref[...], preferred_element_type=_F32)
    h = x_ref[...] + gm_ref[0] * mix
    h_ref[...] = h
    n2 = _rms_modulate(h, ng_ref[...], scf_ref[0], shf_ref[0])
    n2_ref[...] = n2

    n_hi, n_lo = _split_bf16(n2)
    lg = (lax.dot_general(wrh_ref[...], n_hi, _NT, preferred_element_type=_F32)
          + lax.dot_general(wrh_ref[...], n_lo, _NT, preferred_element_type=_F32)
          + lax.dot_general(wrl_ref[...], n_hi, _NT, preferred_element_type=_F32)) + br_ref[:, 0:1]

    gl = [lg[g:g + 1, :] for g in range(groups)]
    gmax = functools.reduce(jnp.maximum, gl)
    denom = functools.reduce(jnp.add, [jnp.exp(v - gmax) for v in gl])
    p_group = 1.0 / denom
    g_sel = jnp.full(gmax.shape, groups - 1, jnp.int32)
    for g in range(groups - 2, -1, -1):
        g_sel = jnp.where(gl[g] == gmax, g, g_sel)

    el = []
    for e in range(epg):
        v = lg[groups + e:groups + e + 1, :]
        for g in range(1, groups):
            r = groups + g * epg + e
            v = jnp.where(g_sel == g, lg[r:r + 1, :], v)
        el.append(v)

    def top1(vals):
        vmax = functools.reduce(jnp.maximum, vals)
        idx = jnp.full(vmax.shape, epg - 1, jnp.int32)
        for e in range(epg - 2, -1, -1):
            idx = jnp.where(vals[e] == vmax, e, idx)
        return vmax, idx

    v1, i1 = top1(el)
    v2, i2 = top1([jnp.where(i1 == e, -jnp.inf, el[e]) for e in range(epg)])
    b = jnp.exp(v2 - v1)
    eid_ref[0:1, :] = g_sel * epg + i1
    eid_ref[1:2, :] = g_sel * epg + i2
    wt_ref[0:1, :] = (1.0 / (1.0 + b)) * p_group
    wt_ref[1:2, :] = (b / (1.0 + b)) * p_group


def _mixout(attn, m_a, proj, x2, ada3, norm_g, w_moba_out, w_out, wr_hi, wr_lo, br,
            *, seq, sw, mw, groups, epg, tm):
    t, d = x2.shape
    per_b = seq // tm
    gb_blk = (2 * sw + 3 * mw) // d + 1
    once = pl.Buffered(1)
    kern = functools.partial(_mixout_kernel, groups=groups, epg=epg)

    def ada_spec(k):
        return pl.BlockSpec((1, 1, d), lambda i: ((i // per_b) * 6 + k, 0, 0))

    return pl.pallas_call(
        kern,
        grid=(t // tm,),
        in_specs=[pl.BlockSpec((tm, mw), lambda i: (i, 0)),
                  pl.BlockSpec((tm, d), lambda i: (i, 0)),
                  pl.BlockSpec((tm, d), lambda i: (i, gb_blk)),
                  pl.BlockSpec((tm, d), lambda i: (i, 0)),
                  ada_spec(2), ada_spec(3), ada_spec(4),
                  pl.BlockSpec((1, d), lambda i: (0, 0)),
                  pl.BlockSpec((mw, d), lambda i: (0, 0), pipeline_mode=once),
                  pl.BlockSpec((d, d), lambda i: (0, 0), pipeline_mode=once),
                  pl.BlockSpec((ROUTER_ROWS, d), lambda i: (0, 0)),
                  pl.BlockSpec((ROUTER_ROWS, d), lambda i: (0, 0)),
                  pl.BlockSpec((ROUTER_ROWS, LANES), lambda i: (0, 0))],
        out_specs=[pl.BlockSpec((tm, d), lambda i: (i, 0)),
                   pl.BlockSpec((tm, d), lambda i: (i, 0)),
                   pl.BlockSpec((MOE_TOPK, tm), lambda i: (0, i)),
                   pl.BlockSpec((MOE_TOPK, tm), lambda i: (0, i))],
        out_shape=[jax.ShapeDtypeStruct((t, d), _F32),
                   jax.ShapeDtypeStruct((t, d), _F32),
                   jax.ShapeDtypeStruct((MOE_TOPK, t), jnp.int32),
                   jax.ShapeDtypeStruct((MOE_TOPK, t), _F32)],
        compiler_params=_params(("arbitrary",)),
        name="mixout",
    )(attn, m_a, proj, x2, ada3, ada3, ada3, norm_g.reshape(1, d), w_moba_out, w_out, wr_hi, wr_lo, br)


def _plan_kernel(eid_ref, dest_ref, tab_ref, carry_scr, start_scr, tri_scr, *, n_exp, rb):
    phase = pl.program_id(0)
    first = pl.program_id(1) == 0
    tc = eid_ref.shape[1]
    nbp = tab_ref.shape[1]

    @pl.when((phase == 0) & first)
    def _():
        carry_scr[...] = jnp.zeros_like(carry_scr)
        earlier = lax.broadcasted_iota(jnp.int32, (tc, tc), 0) < lax.broadcasted_iota(jnp.int32, (tc, tc), 1)
        tri_scr[...] = earlier.astype(_BF16)

    eio = lax.broadcasted_iota(jnp.int32, (n_exp, tc), 0)
    oh0 = (eio == eid_ref[0:1, :]).astype(_F32)
    oh1 = (eio == eid_ref[1:2, :]).astype(_F32)
    oh = oh0 + oh1

    @pl.when((phase == 1) & first)
    def _():
        counts = carry_scr[...].astype(jnp.int32)
        blocks = lax.shift_right_logical(counts + (rb - 1), rb.bit_length() - 1).astype(_F32)
        upto = lax.broadcasted_iota(jnp.int32, (n_exp, n_exp), 0) >= lax.broadcasted_iota(jnp.int32, (n_exp, n_exp), 1)
        end_blk = jnp.dot(upto.astype(_BF16), blocks.astype(_BF16), preferred_element_type=_F32)
        start_scr[...] = (end_blk - blocks) * rb
        carry_scr[...] = jnp.zeros_like(carry_scr)
        blk_id = lax.broadcasted_iota(jnp.int32, (n_exp, nbp), 1).astype(_F32)
        owner = jnp.sum((end_blk[:, 0:1] <= blk_id).astype(_F32), axis=0, keepdims=True)
        owner = jnp.minimum(owner, n_exp - 1.0)
        used = jnp.broadcast_to(end_blk[n_exp - 1:n_exp, 0:1], (1, nbp))
        trow = lax.broadcasted_iota(jnp.int32, tab_ref.shape, 0)
        tab_ref[...] = jnp.where(trow == 0, owner, jnp.where(trow == 1, used, 0.0)).astype(jnp.int32)

    @pl.when(phase == 1)
    def _():
        before = jnp.dot(oh.astype(_BF16), tri_scr[...], preferred_element_type=_F32)
        base = start_scr[:, 0:1] + carry_scr[:, 0:1] + before
        dest_ref[0:1, :] = jnp.sum(oh0 * base, axis=0, keepdims=True).astype(jnp.int32)
        dest_ref[1:2, :] = jnp.sum(oh1 * base, axis=0, keepdims=True).astype(jnp.int32)

    carry_scr[...] = carry_scr[...] + jnp.sum(oh, axis=1, keepdims=True)


def _plan(eid, *, n_exp, tc, rb, n_blocks):
    t = eid.shape[1]
    assert rb & (rb - 1) == 0 and t * MOE_TOPK // rb + n_exp < 256
    nbp = -(-n_blocks // LANES) * LANES
    kern = functools.partial(_plan_kernel, n_exp=n_exp, rb=rb)
    return pl.pallas_call(
        kern,
        grid=(2, t // tc),
        in_specs=[pl.BlockSpec((MOE_TOPK, tc), lambda p, i: (0, i))],
        out_specs=[pl.BlockSpec((MOE_TOPK, tc), lambda p, i: (0, i * p)),
                   pl.BlockSpec((8, nbp), lambda p, i: (0, 0))],
        out_shape=[jax.ShapeDtypeStruct((MOE_TOPK, t), jnp.int32),
                   jax.ShapeDtypeStruct((8, nbp), jnp.int32)],
        scratch_shapes=[pltpu.VMEM((n_exp, LANES), _F32), pltpu.VMEM((n_exp, LANES), _F32),
                        pltpu.VMEM((tc, tc), _BF16)],
        compiler_params=_params(("arbitrary", "arbitrary")),
        name="plan",
    )(eid)


def _row_copy(src, src_row, dst, dst_row, sem):
    return pltpu.make_async_copy(src.at[pl.ds(src_row, 1)], dst.at[pl.ds(dst_row, 1)], sem)


def _scatter_kernel(dest_ref, n2_ref, xs_in_ref, xs_ref, sem):
    del xs_in_ref
    ts = dest_ref.shape[1]

    def issue(r, c):
        for k in range(MOE_TOPK):
            _row_copy(n2_ref, r, xs_ref, dest_ref[k, r], sem).start()
        return c

    lax.fori_loop(0, ts, issue, 0)

    def drain(r, c):
        for k in range(MOE_TOPK):
            _row_copy(n2_ref, 0, xs_ref, 0, sem).wait()
        return c

    lax.fori_loop(0, ts, drain, 0)


def _scatter(dest3, n2, xs_zero):
    steps, _, ts = dest3.shape
    return pl.pallas_call(
        _scatter_kernel,
        grid=(steps,),
        in_specs=[pl.BlockSpec((None, MOE_TOPK, ts), lambda i: (i, 0, 0), memory_space=pltpu.SMEM),
                  pl.BlockSpec((ts, n2.shape[1]), lambda i: (i, 0)),
                  pl.BlockSpec(memory_space=pl.ANY)],
        out_specs=pl.BlockSpec(memory_space=pl.ANY),
        out_shape=jax.ShapeDtypeStruct(xs_zero.shape, xs_zero.dtype),
        scratch_shapes=[pltpu.SemaphoreType.DMA(())],
        input_output_aliases={2: 0},
        compiler_params=_params(("arbitrary",)),
        name="scatter",
    )(dest3, n2, xs_zero)


def _expert_kernel(be_ref, nu_ref, xs_ref, wg_ref, wu_ref, wd_ref, y_ref, wg_s, wu_s, wd_s):
    i = pl.program_id(0)
    used = i < nu_ref[0]
    fresh = (i == 0) | (be_ref[i] != be_ref[jnp.maximum(i - 1, 0)])

    @pl.when(used & fresh)
    def _():
        wg_s[...] = wg_ref[0].astype(_BF16)
        wu_s[...] = wu_ref[0].astype(_BF16)
        wd_s[...] = wd_ref[0].astype(_BF16)

    @pl.when(used)
    def _():
        xb = xs_ref[...].astype(_BF16)
        gate = jnp.dot(xb, wg_s[...], preferred_element_type=_F32)
        up = jnp.dot(xb, wu_s[...], preferred_element_type=_F32)
        hid = (jax.nn.silu(gate) * up).astype(_BF16)
        y_ref[...] = jnp.dot(hid, wd_s[...], preferred_element_type=_F32)

    @pl.when(jnp.logical_not(used))
    def _():
        y_ref[...] = jnp.zeros_like(y_ref)


def _experts(block_expert, n_used, xs, w_gate, w_up, w_down):
    cap, d = xs.shape
    f = w_gate.shape[2]
    rb = EXPERT_ROW_BLOCK
    grid_spec = pltpu.PrefetchScalarGridSpec(
        num_scalar_prefetch=2,
        grid=(cap // rb,),
        in_specs=[pl.BlockSpec((rb, d), lambda i, be, nu: (jnp.minimum(i, jnp.maximum(nu[0] - 1, 0)), 0)),
                  pl.BlockSpec((1, d, f), lambda i, be, nu: (be[i], 0, 0)),
                  pl.BlockSpec((1, d, f), lambda i, be, nu: (be[i], 0, 0)),
                  pl.BlockSpec((1, f, d), lambda i, be, nu: (be[i], 0, 0))],
        out_specs=pl.BlockSpec((rb, d), lambda i, be, nu: (i, 0)),
        scratch_shapes=[pltpu.VMEM((d, f), _BF16), pltpu.VMEM((d, f), _BF16), pltpu.VMEM((f, d), _BF16)],
    )
    return pl.pallas_call(
        _expert_kernel,
        grid_spec=grid_spec,
        out_shape=jax.ShapeDtypeStruct((cap, d), _F32),
        compiler_params=_params(("arbitrary",)),
        name="experts",
    )(block_expert, n_used, xs, w_gate, w_up, w_down)


def _combine_kernel(dest_ref, h_ref, wc_ref, gf_ref, ng_ref, y_ref, o_ref, ybuf, sem):
    tk = h_ref.shape[0]

    def issue(r, c):
        for k in range(MOE_TOPK):
            _row_copy(y_ref, dest_ref[k, r], ybuf.at[k], r, sem).start()
        return c

    lax.fori_loop(0, tk, issue, 0)

    def drain(r, c):
        for k in range(MOE_TOPK):
            _row_copy(y_ref, 0, ybuf.at[k], 0, sem).wait()
        return c

    lax.fori_loop(0, tk, drain, 0)

    moe = wc_ref[:, 0:1] * ybuf[0] + wc_ref[:, 1:2] * ybuf[1]
    h = h_ref[...] + gf_ref[0] * moe
    o_ref[...] = h * lax.rsqrt(jnp.mean(h * h, axis=-1, keepdims=True) + EPS) * ng_ref[...]


def _combine(dest3, h1, w_cols, ada3, norm_g, y_rows, *, seq):
    t, d = h1.shape
    steps, _, tk = dest3.shape
    per_b = seq // tk
    return pl.pallas_call(
        _combine_kernel,
        grid=(steps,),
        in_specs=[pl.BlockSpec((None, MOE_TOPK, tk), lambda i: (i, 0, 0), memory_space=pltpu.SMEM),
                  pl.BlockSpec((tk, d), lambda i: (i, 0)),
                  pl.BlockSpec((tk, LANES), lambda i: (i, 0)),
                  pl.BlockSpec((1, 1, d), lambda i: ((i // per_b) * 6 + 5, 0, 0)),
                  pl.BlockSpec((1, d), lambda i: (0, 0)),
                  pl.BlockSpec(memory_space=pl.ANY)],
        out_specs=pl.BlockSpec((tk, d), lambda i: (i, 0)),
        out_shape=jax.ShapeDtypeStruct((t, d), _F32),
        scratch_shapes=[pltpu.VMEM((MOE_TOPK, tk, d), _F32), pltpu.SemaphoreType.DMA(())],
        compiler_params=_params(("arbitrary",)),
        name="combine",
    )(dest3, h1, w_cols, ada3, norm_g.reshape(1, d), y_rows)


def _rotary_tables(positions, dh):
    inv_freq = ROPE_THETA ** (-jnp.arange(0, dh, 2, dtype=_F32) / dh)
    ang = positions.astype(_F32)[..., None] * inv_freq
    cos, sin = jnp.cos(ang), jnp.sin(ang)
    t = cos.shape[0] * cos.shape[1]
    cos = jnp.concatenate([cos, cos], axis=-1).reshape(t, dh)
    sin = jnp.concatenate([-sin, sin], axis=-1).reshape(t, dh)
    return cos, sin


def _tile(n, want):
    while n % want:
        want //= 2
    return want


def kernel(x, c, positions, w_ada, b_ada, norm_mix_g, w_in, sgu_ln_g, sgu_ln_b, sgu_w_s, sgu_b_s,
           w_sgu_out, w_moba_out, w_out, norm_ffn_g, w_route_group, b_route_group, w_route_expert,
           b_route_expert, w_exp_gate, w_exp_up, w_exp_down, norm_final_g):
    batch, seq, d = x.shape
    depth = w_ada.shape[0]
    t = batch * seq
    sw = sgu_ln_g.shape[1]
    mw = w_moba_out.shape[1]
    dh = mw // MOBA_HEADS
    groups = w_route_group.shape[2]
    n_exp = w_route_expert.shape[2]
    epg = n_exp // groups
    rb = EXPERT_ROW_BLOCK
    cap = (t * MOE_TOPK // rb + n_exp) * rb
    assert depth == 1, "the final RMSNorm is fused into the single layer's combine"
    assert dh == LANES and seq % MOBA_BLOCK == 0 and sw * 2 == d and mw % sw == 0
    assert groups + n_exp <= ROUTER_ROWS and (t * MOE_TOPK) % rb == 0

    cos, sin = _rotary_tables(positions, dh)
    c_pad = jnp.zeros((8, d), _F32).at[:batch].set(c)
    h = x.reshape(t, d)
    ts = _tile(seq, 256)

    for l in range(depth):
        ada = _ada(c_pad, w_ada[l], b_ada[l], _tile(6 * d, 1024))
        ada3 = ada[:batch].reshape(batch * 6, 1, d)

        proj = _inproj(h, ada3, norm_mix_g[l], w_in[l].astype(_BF16), cos, sin, sgu_ln_g[l], sgu_ln_b[l],
                       seq=seq, sw=sw, mw=mw, dh=dh, tm=_tile(seq, 1024))
        bs_wide = jnp.repeat(sgu_b_s[l].T, sw // sgu_w_s.shape[1], axis=1)
        m_a = _sgu(proj, sgu_w_s[l], bs_wide, w_sgu_out[l].astype(_BF16), sw=sw, mw=mw, tm=_tile(seq, 512))
        attn = _moba(proj, batch=batch, seq=seq, sw=sw, mw=mw, dh=dh, hg=min(MOBA_HEAD_GROUP, MOBA_HEADS))

        wr = jnp.concatenate([w_route_group[l], w_route_expert[l]], axis=1).T
        wr = jnp.zeros((ROUTER_ROWS, d), _F32).at[:groups + n_exp].set(wr)
        wr_hi, wr_lo = _split_bf16(wr)
        br = jnp.concatenate([b_route_group[l], b_route_expert[l].reshape(-1)])
        br = jnp.broadcast_to(jnp.zeros((ROUTER_ROWS,), _F32).at[:groups + n_exp].set(br)[:, None],
                              (ROUTER_ROWS, LANES))
        h1, n2, eid, wt = _mixout(attn, m_a, proj, h, ada3, norm_ffn_g[l], w_moba_out[l].astype(_BF16),
                                  w_out[l].astype(_BF16), wr_hi, wr_lo, br,
                                  seq=seq, sw=sw, mw=mw, groups=groups, epg=epg, tm=_tile(seq, 256))

        dest, tab = _plan(eid, n_exp=n_exp, tc=_tile(t, 512), rb=rb, n_blocks=cap // rb)
        dest3 = dest.reshape(MOE_TOPK, t // ts, ts).transpose(1, 0, 2)
        block_expert = tab[0, :cap // rb]
        n_used = tab[1, :1]

        xs = _scatter(dest3, n2, jnp.zeros((cap, d), _F32))
        y_rows = _experts(block_expert, n_used, xs, w_exp_gate[l], w_exp_up[l], w_exp_down[l])
        w_cols = jnp.zeros((t, LANES), _F32).at[:, :MOE_TOPK].set(wt.T)
        h = _combine(dest3, h1, w_cols, ada3, norm_final_g, y_rows, seq=seq)

    return h.reshape(batch, seq, d)
```

```python
import functools

import jax
import jax.numpy as jnp
from jax import lax
from jax.experimental import pallas as pl
from jax.experimental.pallas import tpu as pltpu

MOBA_HEADS = 16
MOBA_BLOCK = 256
MOBA_TOPK = 3
SGU_CHUNK = 128
ROPE_THETA = 10000.0
MOE_GROUPS = 4
MOE_TOPK = 2
EPS = 1e-6
NEG_INF = -1e30
LOG2_E = 1.4426950408889634

LANES = 128
EXPERT_ROW_BLOCK = 256
MOBA_HEAD_GROUP = 4
ROUTER_ROWS = 40
VMEM_LIMIT = 56 * 1024 * 1024

_F32 = jnp.float32
_BF16 = jnp.bfloat16
_NT = (((1,), (1,)), ((), ()))


def _params(semantics, vmem=VMEM_LIMIT):
    return pltpu.CompilerParams(dimension_semantics=semantics, vmem_limit_bytes=vmem)


def _split_bf16(a):
    hi = a.astype(_BF16)
    lo = (a - hi.astype(_F32)).astype(_BF16)
    return hi, lo


def _ada_kernel(c_ref, w_ref, b_ref, o_ref):
    ca_hi, ca_lo = _split_bf16(jax.nn.silu(c_ref[...]))
    w_hi, w_lo = _split_bf16(w_ref[...])
    acc = jnp.dot(ca_hi, w_hi, preferred_element_type=_F32)
    acc += jnp.dot(ca_hi, w_lo, preferred_element_type=_F32)
    acc += jnp.dot(ca_lo, w_hi, preferred_element_type=_F32)
    o_ref[...] = acc + b_ref[...]


def _ada(c_pad, w_ada, b_ada, tn):
    rows, d = c_pad.shape
    n = w_ada.shape[1]
    return pl.pallas_call(
        _ada_kernel,
        grid=(n // tn,),
        in_specs=[pl.BlockSpec((rows, d), lambda j: (0, 0)),
                  pl.BlockSpec((d, tn), lambda j: (0, j)),
                  pl.BlockSpec((1, tn), lambda j: (0, j))],
        out_specs=pl.BlockSpec((rows, tn), lambda j: (0, j)),
        out_shape=jax.ShapeDtypeStruct((rows, n), _F32),
        compiler_params=_params(("arbitrary",)),
        name="ada",
    )(c_pad, w_ada, b_ada.reshape(1, n))


def _rms_modulate(x, g, sc, sh):
    y = x * lax.rsqrt(jnp.mean(x * x, axis=-1, keepdims=True) + EPS) * g
    return y * (1.0 + sc) + sh


def _inproj_kernel(x_ref, g_ref, sc_ref, sh_ref, w_ref, cos_ref, sin_ref, lng_ref, lnb_ref,
                   o_ref, n_scr, *, q0, k0, v0, dh, scale):
    j = pl.program_id(1)

    @pl.when(j == 0)
    def _():
        n_scr[...] = _rms_modulate(x_ref[...], g_ref[...], sc_ref[0], sh_ref[0]).astype(_BF16)

    acc = jnp.dot(n_scr[...], w_ref[...], preferred_element_type=_F32)

    @pl.when(j == 0)
    def _():
        o_ref[...] = jax.nn.gelu(acc).astype(_BF16)

    @pl.when(j == 1)
    def _():
        z = jax.nn.gelu(acc)
        mu = jnp.mean(z, axis=-1, keepdims=True)
        zc = z - mu
        var = jnp.mean(zc * zc, axis=-1, keepdims=True)
        o_ref[...] = (zc * lax.rsqrt(var + EPS) * lng_ref[...] + lnb_ref[...]).astype(_BF16)

    def rope(mult):
        cos = cos_ref[...]
        sin = sin_ref[...]
        for h in range(acc.shape[1] // dh):
            xh = acc[:, h * dh:(h + 1) * dh]
            rot = xh * cos + pltpu.roll(xh, dh // 2, 1) * sin
            o_ref[:, h * dh:(h + 1) * dh] = (rot * mult).astype(_BF16)

    @pl.when((j >= q0) & (j < k0))
    def _():
        rope(scale)

    @pl.when((j >= k0) & (j < v0))
    def _():
        rope(1.0)

    @pl.when(j >= v0)
    def _():
        o_ref[...] = acc.astype(_BF16)


def _inproj(x2, ada3, norm_g, w_in, cos, sin, ln_g, ln_b, *, seq, sw, mw, dh, tm):
    t, d = x2.shape
    n = w_in.shape[1]
    tn = sw
    per_b = seq // tm
    q0 = 2
    k0 = q0 + mw // tn
    v0 = k0 + mw // tn
    kern = functools.partial(_inproj_kernel, q0=q0, k0=k0, v0=v0, dh=dh, scale=dh ** -0.5 * LOG2_E)
    return pl.pallas_call(
        kern,
        grid=(t // tm, n // tn),
        in_specs=[pl.BlockSpec((tm, d), lambda i, j: (i, 0)),
                  pl.BlockSpec((1, d), lambda i, j: (0, 0)),
                  pl.BlockSpec((1, 1, d), lambda i, j: ((i // per_b) * 6 + 1, 0, 0)),
                  pl.BlockSpec((1, 1, d), lambda i, j: ((i // per_b) * 6 + 0, 0, 0)),
                  pl.BlockSpec((d, tn), lambda i, j: (0, j)),
                  pl.BlockSpec((tm, dh), lambda i, j: (i, 0)),
                  pl.BlockSpec((tm, dh), lambda i, j: (i, 0)),
                  pl.BlockSpec((1, sw), lambda i, j: (0, 0)),
                  pl.BlockSpec((1, sw), lambda i, j: (0, 0))],
        out_specs=pl.BlockSpec((tm, tn), lambda i, j: (i, j)),
        out_shape=jax.ShapeDtypeStruct((t, n), _BF16),
        scratch_shapes=[pltpu.VMEM((tm, d), _BF16)],
        compiler_params=_params(("arbitrary", "arbitrary")),
        name="inproj",
    )(x2, norm_g.reshape(1, d), ada3, ada3, w_in, cos, sin, ln_g.reshape(1, sw), ln_b.reshape(1, sw))


def _sgu_kernel(u_ref, v_ref, ga_ref, ws_ref, bs_ref, wo_ref, o_ref, gated_scr, *, groups, cg):
    c = SGU_CHUNK
    tri = lax.broadcasted_iota(jnp.int32, (c, c), 0) >= lax.broadcasted_iota(jnp.int32, (c, c), 1)
    for g in range(groups):
        wg = jnp.where(tri, ws_ref[g], 0.0).astype(_BF16)
        cols = slice(g * cg, (g + 1) * cg)
        for ci in range(u_ref.shape[0] // c):
            rows = slice(ci * c, (ci + 1) * c)
            sv = jnp.dot(wg, v_ref[rows, cols], preferred_element_type=_F32) + bs_ref[:, cols]
            gated_scr[rows, cols] = (u_ref[rows, cols].astype(_F32) * sv).astype(_BF16)
    ya = jnp.dot(gated_scr[...], wo_ref[...], preferred_element_type=_F32)
    o_ref[...] = (jax.nn.sigmoid(ga_ref[...].astype(_F32)) * ya).astype(_BF16)


def _sgu(proj, w_s, bs_wide, w_sgu_out, *, sw, mw, tm):
    t = proj.shape[0]
    d = w_sgu_out.shape[1]
    groups = w_s.shape[0]
    ga_blk = (2 * sw + 3 * mw) // d
    kern = functools.partial(_sgu_kernel, groups=groups, cg=sw // groups)
    return pl.pallas_call(
        kern,
        grid=(t // tm,),
        in_specs=[pl.BlockSpec((tm, sw), lambda i: (i, 0)),
                  pl.BlockSpec((tm, sw), lambda i: (i, 1)),
                  pl.BlockSpec((tm, d), lambda i: (i, ga_blk)),
                  pl.BlockSpec(w_s.shape, lambda i: (0, 0, 0)),
                  pl.BlockSpec(bs_wide.shape, lambda i: (0, 0)),
                  pl.BlockSpec(w_sgu_out.shape, lambda i: (0, 0))],
        out_specs=pl.BlockSpec((tm, d), lambda i: (i, 0)),
        out_shape=jax.ShapeDtypeStruct((t, d), _BF16),
        scratch_shapes=[pltpu.VMEM((tm, sw), _BF16)],
        compiler_params=_params(("arbitrary",)),
        name="sgu",
    )(proj, proj, proj, w_s, bs_wide, w_sgu_out)


def _moba_kernel(q_ref, k_ref, v_ref, o_ref, kmh_scr, kml_scr, vt_scr, sel_scr,
                 s_scr, p_scr, acc_scr, m_scr, l_scr, a_scr, *, nb, topk, hg, dh):
    qi = pl.program_id(2)
    blk = MOBA_BLOCK
    tq = q_ref.shape[0]

    @pl.when(qi == 0)
    def _():
        for h in range(hg):
            cols = slice(h * dh, (h + 1) * dh)
            kf = k_ref[:, cols].astype(_F32).reshape(nb, blk, dh)
            km_hi, km_lo = _split_bf16(jnp.mean(kf, axis=1))
            kmh_scr[h * nb:(h + 1) * nb, :] = km_hi
            kml_scr[h * nb:(h + 1) * nb, :] = km_lo
            for c in range(nb):
                rows = slice(c * blk, (c + 1) * blk)
                vt_scr[cols, rows] = v_ref[rows, cols].astype(_F32).T.astype(_BF16)

    row = lax.broadcasted_iota(jnp.int32, (nb, tq), 0)
    past = row < qi
    kpos = lax.broadcasted_iota(jnp.int32, (blk, tq), 0)
    qpos = lax.broadcasted_iota(jnp.int32, (blk, tq), 1)
    own = pl.multiple_of(qi * blk, blk)

    def scores(start, h):
        cols = slice(h * dh, (h + 1) * dh)
        return lax.dot_general(k_ref[pl.ds(start, blk), cols], q_ref[:, cols], _NT,
                               preferred_element_type=_F32)

    def weighted_values(start, h, p):
        return jnp.dot(vt_scr[h * dh:(h + 1) * dh, pl.ds(start, blk)], p, preferred_element_type=_F32)

    for h in range(hg):
        hrows = slice(h * nb, (h + 1) * nb)
        q = q_ref[:, h * dh:(h + 1) * dh]
        gate = (lax.dot_general(kmh_scr[hrows, :], q, _NT, preferred_element_type=_F32)
                + lax.dot_general(kml_scr[hrows, :], q, _NT, preferred_element_type=_F32))
        gm = jnp.where(past, gate, NEG_INF)
        beaten = jnp.zeros((nb, tq), jnp.int32)
        for jp in range(nb):
            other = gm[jp:jp + 1, :]
            wins = (other > gm) | ((other == gm) & (jp < row))
            beaten += wins.astype(jnp.int32)
        sel_scr[hrows, :] = (past & (beaten < topk)).astype(_F32)

        s = jnp.where(kpos <= qpos, scores(own, h), NEG_INF)
        m0 = jnp.max(s, axis=0, keepdims=True)
        p = jnp.exp2(s - m0)
        m_scr[h] = m0
        l_scr[h] = jnp.sum(p, axis=0, keepdims=True)
        a_scr[h] = jnp.ones_like(m0)
        acc_scr[h] = jnp.zeros(acc_scr.shape[1:], _F32)
        p_scr[0, h] = p.astype(_BF16)
        s_scr[0, h] = scores(0, h)

    def body(j, c):
        par = lax.rem(j, 2)
        prev = pl.multiple_of(jnp.where(j == 0, qi, j - 1) * blk, blk)
        nxt = pl.multiple_of(jnp.minimum(j + 1, qi - 1) * blk, blk)
        for h in range(hg):
            acc_scr[h] = a_scr[h] * acc_scr[h] + weighted_values(prev, h, p_scr[par, h])
        for h in range(hg):
            s = jnp.where(sel_scr[pl.ds(h * nb + j, 1), :] > 0.0, s_scr[par, h], NEG_INF)
            m = m_scr[h]
            m_new = jnp.maximum(m, jnp.max(s, axis=0, keepdims=True))
            alpha = jnp.exp2(m - m_new)
            p = jnp.exp2(s - m_new)
            l_scr[h] = alpha * l_scr[h] + jnp.sum(p, axis=0, keepdims=True)
            m_scr[h] = m_new
            a_scr[h] = alpha
            p_scr[1 - par, h] = p.astype(_BF16)
        for h in range(hg):
            s_scr[1 - par, h] = scores(nxt, h)
        return c

    lax.fori_loop(0, qi, body, 0)
    last = pl.multiple_of(jnp.where(qi == 0, qi, qi - 1) * blk, blk)
    for h in range(hg):
        acc = a_scr[h] * acc_scr[h] + weighted_values(last, h, p_scr[lax.rem(qi, 2), h])
        o_ref[:, h * dh:(h + 1) * dh] = (acc / l_scr[h]).T.astype(_BF16)


def _moba(proj, *, batch, seq, sw, mw, dh, hg):
    t = proj.shape[0]
    heads = mw // dh
    nb = seq // MOBA_BLOCK
    tq = MOBA_BLOCK
    nq = seq // tq
    gw = hg * dh
    qc = 2 * sw // gw
    kc = qc + heads // hg
    vc = kc + heads // hg
    kern = functools.partial(_moba_kernel, nb=nb, topk=MOBA_TOPK, hg=hg, dh=dh)
    return pl.pallas_call(
        kern,
        grid=(batch, heads // hg, nq),
        in_specs=[pl.BlockSpec((tq, gw), lambda b, h, i: (b * nq + i, qc + h)),
                  pl.BlockSpec((seq, gw), lambda b, h, i: (b, kc + h)),
                  pl.BlockSpec((seq, gw), lambda b, h, i: (b, vc + h))],
        out_specs=pl.BlockSpec((tq, gw), lambda b, h, i: (b * nq + i, h)),
        out_shape=jax.ShapeDtypeStruct((t, mw), _BF16),
        scratch_shapes=[pltpu.VMEM((hg * nb, dh), _BF16), pltpu.VMEM((hg * nb, dh), _BF16),
                        pltpu.VMEM((gw, seq), _BF16), pltpu.VMEM((hg * nb, tq), _F32),
                        pltpu.VMEM((2, hg, MOBA_BLOCK, tq), _F32), pltpu.VMEM((2, hg, MOBA_BLOCK, tq), _BF16),
                        pltpu.VMEM((hg, dh, tq), _F32), pltpu.VMEM((hg, 1, tq), _F32),
                        pltpu.VMEM((hg, 1, tq), _F32), pltpu.VMEM((hg, 1, tq), _F32)],
        compiler_params=_params(("arbitrary", "arbitrary", "arbitrary")),
        name="moba",
    )(proj, proj, proj)


def _mixout_kernel(attn_ref, ma_ref, gb_ref, x_ref, gm_ref, shf_ref, scf_ref, ng_ref,
                   wmo_ref, wo_ref, wrh_ref, wrl_ref, br_ref,
                   h_ref, n2_ref, eid_ref, wt_ref, *, groups, epg):
    yb = jnp.dot(attn_ref[...], wmo_ref[...], preferred_element_type=_F32)
    merged = ma_ref[...].astype(_F32) + jax.nn.sigmoid(gb_ref[...].astype(_F32)) * yb
    mix = jnp.dot(merged.astype(_BF16), wo_ref[...], preferred_element_type=_F32)
    h = x_ref[...] + gm_ref[0] * mix
    h_ref[...] = h
    n2 = _rms_modulate(h, ng_ref[...], scf_ref[0], shf_ref[0])
    n2_ref[...] = n2

    n_hi, n_lo = _split_bf16(n2)
    lg = (lax.dot_general(wrh_ref[...], n_hi, _NT, preferred_element_type=_F32)
          + lax.dot_general(wrh_ref[...], n_lo, _NT, preferred_element_type=_F32)
          + lax.dot_general(wrl_ref[...], n_hi, _NT, preferred_element_type=_F32)) + br_ref[:, 0:1]

    gl = [lg[g:g + 1, :] for g in range(groups)]
    gmax = functools.reduce(jnp.maximum, gl)
    denom = functools.reduce(jnp.add, [jnp.exp(v - gmax) for v in gl])
    p_group = 1.0 / denom
    g_sel = jnp.full(gmax.shape, groups - 1, jnp.int32)
    for g in range(groups - 2, -1, -1):
        g_sel = jnp.where(gl[g] == gmax, g, g_sel)

    el = []
    for e in range(epg):
        v = lg[groups + e:groups + e + 1, :]
        for g in range(1, groups):
            r = groups + g * epg + e
            v = jnp.where(g_sel == g, lg[r:r + 1, :], v)
        el.append(v)

    def top1(vals):
        vmax = functools.reduce(jnp.maximum, vals)
        idx = jnp.full(vmax.shape, epg - 1, jnp.int32)
        for e in range(epg - 2, -1, -1):
            idx = jnp.where(vals[e] == vmax, e, idx)
        return vmax, idx

    v1, i1 = top1(el)
    v2, i2 = top1([jnp.where(i1 == e, -jnp.inf, el[e]) for e in range(epg)])
    b = jnp.exp(v2 - v1)
    eid_ref[0:1, :] = g_sel * epg + i1
    eid_ref[1:2, :] = g_sel * epg + i2
    wt_ref[0:1, :] = (1.0 / (1.0 + b)) * p_group
    wt_ref[1:2, :] = (b / (1.0 + b)) * p_group


def _mixout(attn, m_a, proj, x2, ada3, norm_g, w_moba_out, w_out, wr_hi, wr_lo, br,
            *, seq, sw, mw, groups, epg, tm):
    t, d = x2.shape
    per_b = seq // tm
    gb_blk = (2 * sw + 3 * mw) // d + 1
    once = pl.Buffered(1)
    kern = functools.partial(_mixout_kernel, groups=groups, epg=epg)

    def ada_spec(k):
        return pl.BlockSpec((1, 1, d), lambda i: ((i // per_b) * 6 + k, 0, 0))

    return pl.pallas_call(
        kern,
        grid=(t // tm,),
        in_specs=[pl.BlockSpec((tm, mw), lambda i: (i, 0)),
                  pl.BlockSpec((tm, d), lambda i: (i, 0)),
                  pl.BlockSpec((tm, d), lambda i: (i, gb_blk)),
                  pl.BlockSpec((tm, d), lambda i: (i, 0)),
                  ada_spec(2), ada_spec(3), ada_spec(4),
                  pl.BlockSpec((1, d), lambda i: (0, 0)),
                  pl.BlockSpec((mw, d), lambda i: (0, 0), pipeline_mode=once),
                  pl.BlockSpec((d, d), lambda i: (0, 0), pipeline_mode=once),
                  pl.BlockSpec((ROUTER_ROWS, d), lambda i: (0, 0)),
                  pl.BlockSpec((ROUTER_ROWS, d), lambda i: (0, 0)),
                  pl.BlockSpec((ROUTER_ROWS, LANES), lambda i: (0, 0))],
        out_specs=[pl.BlockSpec((tm, d), lambda i: (i, 0)),
                   pl.BlockSpec((tm, d), lambda i: (i, 0)),
                   pl.BlockSpec((MOE_TOPK, tm), lambda i: (0, i)),
                   pl.BlockSpec((MOE_TOPK, tm), lambda i: (0, i))],
        out_shape=[jax.ShapeDtypeStruct((t, d), _F32),
                   jax.ShapeDtypeStruct((t, d), _F32),
                   jax.ShapeDtypeStruct((MOE_TOPK, t), jnp.int32),
                   jax.ShapeDtypeStruct((MOE_TOPK, t), _F32)],
        compiler_params=_params(("arbitrary",)),
        name="mixout",
    )(attn, m_a, proj, x2, ada3, ada3, ada3, norm_g.reshape(1, d), w_moba_out, w_out, wr_hi, wr_lo, br)


def _plan_kernel(eid_ref, dest_ref, tab_ref, carry_scr, start_scr, tri_scr, *, n_exp, rb):
    phase = pl.program_id(0)
    first = pl.program_id(1) == 0
    tc = eid_ref.shape[1]
    nbp = tab_ref.shape[1]

    @pl.when((phase == 0) & first)
    def _():
        carry_scr[...] = jnp.zeros_like(carry_scr)
        earlier = lax.broadcasted_iota(jnp.int32, (tc, tc), 0) < lax.broadcasted_iota(jnp.int32, (tc, tc), 1)
        tri_scr[...] = earlier.astype(_BF16)

    eio = lax.broadcasted_iota(jnp.int32, (n_exp, tc), 0)
    oh0 = (eio == eid_ref[0:1, :]).astype(_F32)
    oh1 = (eio == eid_ref[1:2, :]).astype(_F32)
    oh = oh0 + oh1

    @pl.when((phase == 1) & first)
    def _():
        counts = carry_scr[...].astype(jnp.int32)
        blocks = lax.shift_right_logical(counts + (rb - 1), rb.bit_length() - 1).astype(_F32)
        upto = lax.broadcasted_iota(jnp.int32, (n_exp, n_exp), 0) >= lax.broadcasted_iota(jnp.int32, (n_exp, n_exp), 1)
        end_blk = jnp.dot(upto.astype(_BF16), blocks.astype(_BF16), preferred_element_type=_F32)
        start_scr[...] = (end_blk - blocks) * rb
        carry_scr[...] = jnp.zeros_like(carry_scr)
        blk_id = lax.broadcasted_iota(jnp.int32, (n_exp, nbp), 1).astype(_F32)
        owner = jnp.sum((end_blk[:, 0:1] <= blk_id).astype(_F32), axis=0, keepdims=True)
        owner = jnp.minimum(owner, n_exp - 1.0)
        used = jnp.broadcast_to(end_blk[n_exp - 1:n_exp, 0:1], (1, nbp))
        trow = lax.broadcasted_iota(jnp.int32, tab_ref.shape, 0)
        tab_ref[...] = jnp.where(trow == 0, owner, jnp.where(trow == 1, used, 0.0)).astype(jnp.int32)

    @pl.when(phase == 1)
    def _():
        before = jnp.dot(oh.astype(_BF16), tri_scr[...], preferred_element_type=_F32)
        base = start_scr[:, 0:1] + carry_scr[:, 0:1] + before
        dest_ref[0:1, :] = jnp.sum(oh0 * base, axis=0, keepdims=True).astype(jnp.int32)
        dest_ref[1:2, :] = jnp.sum(oh1 * base, axis=0, keepdims=True).astype(jnp.int32)

    carry_scr[...] = carry_scr[...] + jnp.sum(oh, axis=1, keepdims=True)


def _plan(eid, *, n_exp, tc, rb, n_blocks):
    t = eid.shape[1]
    assert rb & (rb - 1) == 0 and t * MOE_TOPK // rb + n_exp < 256
    nbp = -(-n_blocks // LANES) * LANES
    kern = functools.partial(_plan_kernel, n_exp=n_exp, rb=rb)
    return pl.pallas_call(
        kern,
        grid=(2, t // tc),
        in_specs=[pl.BlockSpec((MOE_TOPK, tc), lambda p, i: (0, i))],
        out_specs=[pl.BlockSpec((MOE_TOPK, tc), lambda p, i: (0, i * p)),
                   pl.BlockSpec((8, nbp), lambda p, i: (0, 0))],
        out_shape=[jax.ShapeDtypeStruct((MOE_TOPK, t), jnp.int32),
                   jax.ShapeDtypeStruct((8, nbp), jnp.int32)],
        scratch_shapes=[pltpu.VMEM((n_exp, LANES), _F32), pltpu.VMEM((n_exp, LANES), _F32),
                        pltpu.VMEM((tc, tc), _BF16)],
        compiler_params=_params(("arbitrary", "arbitrary")),
        name="plan",
    )(eid)


def _invert_kernel(dest_ref, inv_ref):
    i = pl.program_id(0)
    tch = dest_ref.shape[1]

    @pl.when(i == 0)
    def _():
        def init(r, c):
            inv_ref[r] = -1
            return c

        lax.fori_loop(0, inv_ref.shape[0], init, 0, unroll=8)

    def body(t, c):
        a = (i * tch + t) * MOE_TOPK
        for k in range(MOE_TOPK):
            inv_ref[dest_ref[k, t]] = a + k
        return c

    lax.fori_loop(0, tch, body, 0, unroll=8)


def _invert(dest3, cap):
    steps, _, tch = dest3.shape
    return pl.pallas_call(
        _invert_kernel,
        grid=(steps,),
        in_specs=[pl.BlockSpec((None, MOE_TOPK, tch), lambda i: (i, 0, 0), memory_space=pltpu.SMEM)],
        out_specs=pl.BlockSpec((cap,), lambda i: (0,), memory_space=pltpu.SMEM),
        out_shape=jax.ShapeDtypeStruct((cap,), jnp.int32),
        compiler_params=_params(("arbitrary",)),
        name="invert",
    )(dest3)


def _expert_kernel(be_ref, nu_ref, invp_ref, invc_ref, invn_ref, n2_ref, wg_ref, wu_ref, wd_ref, y_ref,
                   wg_s, wu_s, wd_s, xbuf, ybuf, gsem, ssem, *, rb, n_assign):
    i = pl.program_id(0)
    last = nu_ref[0] - 1
    used = i <= last
    fresh = (i == 0) | (be_ref[i] != be_ref[jnp.maximum(i - 1, 0)])
    xs = lax.rem(i, 2)
    ys = lax.rem(i, 3)
    yprev = lax.rem(i + 2, 3)

    def gather(inv_ref, slot):
        for r in range(rb):
            tok = lax.shift_right_logical(jnp.maximum(inv_ref[0, r], 0), 1)
            pltpu.make_async_copy(n2_ref.at[pl.ds(tok, 1)], xbuf.at[slot, pl.ds(r, 1)], gsem.at[slot]).start()

    def gather_wait(slot):
        pltpu.make_async_copy(n2_ref.at[pl.ds(0, rb)], xbuf.at[slot], gsem.at[slot]).wait()

    d = xbuf.shape[2]

    def scatter(inv_ref, slot, live):
        for r in range(rb):
            a = inv_ref[0, r]
            row = jnp.where((a >= 0) & live, a, n_assign + slot * rb + r)
            col = pl.multiple_of((row & 1) * d, d)
            dst = y_ref.at[pl.ds(lax.shift_right_logical(row, 1), 1), pl.ds(col, d)]
            pltpu.make_async_copy(ybuf.at[slot, pl.ds(r, 1)], dst, ssem.at[slot]).start()

    def scatter_wait(slot):
        pltpu.make_async_copy(ybuf.at[slot], y_ref.at[pl.ds(0, rb), pl.ds(0, d)], ssem.at[slot]).wait()

    @pl.when(i == 0)
    def _():
        gather(invc_ref, 0)
        ybuf[2] = jnp.zeros(ybuf.shape[1:], _F32)
        fills = [pltpu.make_async_copy(ybuf.at[2], y_ref.at[pl.ds(row0, rb), pl.ds(half * d, d)], ssem.at[2])
                 for row0 in range(n_assign // MOE_TOPK, y_ref.shape[0], rb) for half in range(MOE_TOPK)]
        for fill in fills:
            fill.start()
        for fill in fills:
            fill.wait()

    @pl.when(used & fresh)
    def _():
        wg_s[...] = wg_ref[0].astype(_BF16)
        wu_s[...] = wu_ref[0].astype(_BF16)
        wd_s[...] = wd_ref[0].astype(_BF16)

    @pl.when(used & (i >= 2))
    def _():
        scatter_wait(ys)

    @pl.when(used)
    def _():
        gather_wait(xs)
        gather(invn_ref, 1 - xs)
        scatter(invp_ref, yprev, i > 0)
        xb = xbuf[xs].astype(_BF16)
        gate = jnp.dot(xb, wg_s[...], preferred_element_type=_F32)
        up = jnp.dot(xb, wu_s[...], preferred_element_type=_F32)
        hid = (jax.nn.silu(gate) * up).astype(_BF16)
        ybuf[ys] = jnp.dot(hid, wd_s[...], preferred_element_type=_F32)

    @pl.when(i == last)
    def _():
        scatter(invc_ref, ys, True)
        scatter_wait(ys)
        scatter_wait(yprev)
        gather_wait(1 - xs)

        @pl.when(last >= 1)
        def _():
            scatter_wait(lax.rem(i + 1, 3))


def _experts(block_expert, n_used, inv3, n2, w_gate, w_up, w_down, *, n_trash):
    nblk, _, rb = inv3.shape
    t, d = n2.shape
    f = w_gate.shape[2]
    n_assign = t * MOE_TOPK
    assert n_trash >= 3 * rb and n_trash % (MOE_TOPK * rb) == 0

    def inv_spec(shift):
        def index(i, be, nu):
            return (jnp.clip(i + shift, 0, nu[0] - 1), 0, 0)
        return pl.BlockSpec((None, 1, rb), index, memory_space=pltpu.SMEM)

    grid_spec = pltpu.PrefetchScalarGridSpec(
        num_scalar_prefetch=2,
        grid=(nblk,),
        in_specs=[inv_spec(-1), inv_spec(0), inv_spec(1),
                  pl.BlockSpec(memory_space=pl.ANY),
                  pl.BlockSpec((1, d, f), lambda i, be, nu: (be[i], 0, 0)),
                  pl.BlockSpec((1, d, f), lambda i, be, nu: (be[i], 0, 0)),
                  pl.BlockSpec((1, f, d), lambda i, be, nu: (be[i], 0, 0))],
        out_specs=pl.BlockSpec(memory_space=pl.ANY),
        scratch_shapes=[pltpu.VMEM((d, f), _BF16), pltpu.VMEM((d, f), _BF16), pltpu.VMEM((f, d), _BF16),
                        pltpu.VMEM((2, rb, d), _F32), pltpu.VMEM((3, rb, d), _F32),
                        pltpu.SemaphoreType.DMA((2,)), pltpu.SemaphoreType.DMA((3,))],
    )
    kern = functools.partial(_expert_kernel, rb=rb, n_assign=n_assign)
    return pl.pallas_call(
        kern,
        grid_spec=grid_spec,
        out_shape=jax.ShapeDtypeStruct(((n_assign + n_trash) // MOE_TOPK, MOE_TOPK * d), _F32),
        compiler_params=_params(("arbitrary",)),
        name="experts",
    )(block_expert, n_used, inv3, inv3, inv3, n2, w_gate, w_up, w_down)


def _combine_kernel(h_ref, wc_ref, gf_ref, ng_ref, y_ref, o_ref):
    d = h_ref.shape[1]
    moe = wc_ref[:, 0:1] * y_ref[:, 0:d] + wc_ref[:, 1:2] * y_ref[:, d:2 * d]
    h = h_ref[...] + gf_ref[0] * moe
    o_ref[...] = h * lax.rsqrt(jnp.mean(h * h, axis=-1, keepdims=True) + EPS) * ng_ref[...]


def _combine(h1, w_cols, ada3, norm_g, y_assign, *, seq, tk):
    t, d = h1.shape
    per_b = seq // tk
    return pl.pallas_call(
        _combine_kernel,
        grid=(t // tk,),
        in_specs=[pl.BlockSpec((tk, d), lambda i: (i, 0)),
                  pl.BlockSpec((tk, LANES), lambda i: (i, 0)),
                  pl.BlockSpec((1, 1, d), lambda i: ((i // per_b) * 6 + 5, 0, 0)),
                  pl.BlockSpec((1, d), lambda i: (0, 0)),
                  pl.BlockSpec((tk, MOE_TOPK * d), lambda i: (i, 0))],
        out_specs=pl.BlockSpec((tk, d), lambda i: (i, 0)),
        out_shape=jax.ShapeDtypeStruct((t, d), _F32),
        compiler_params=_params(("arbitrary",)),
        name="combine",
    )(h1, w_cols, ada3, norm_g.reshape(1, d), y_assign)


def _rotary_tables(positions, dh):
    inv_freq = ROPE_THETA ** (-jnp.arange(0, dh, 2, dtype=_F32) / dh)
    ang = positions.astype(_F32)[..., None] * inv_freq
    cos, sin = jnp.cos(ang), jnp.sin(ang)
    t = cos.shape[0] * cos.shape[1]
    cos = jnp.concatenate([cos, cos], axis=-1).reshape(t, dh)
    sin = jnp.concatenate([-sin, sin], axis=-1).reshape(t, dh)
    return cos, sin


def _tile(n, want):
    while n % want:
        want //= 2
    return want


def kernel(x, c, positions, w_ada, b_ada, norm_mix_g, w_in, sgu_ln_g, sgu_ln_b, sgu_w_s, sgu_b_s,
           w_sgu_out, w_moba_out, w_out, norm_ffn_g, w_route_group, b_route_group, w_route_expert,
           b_route_expert, w_exp_gate, w_exp_up, w_exp_down, norm_final_g):
    batch, seq, d = x.shape
    depth = w_ada.shape[0]
    t = batch * seq
    sw = sgu_ln_g.shape[1]
    mw = w_moba_out.shape[1]
    dh = mw // MOBA_HEADS
    groups = w_route_group.shape[2]
    n_exp = w_route_expert.shape[2]
    epg = n_exp // groups
    rb = EXPERT_ROW_BLOCK
    cap = (t * MOE_TOPK // rb + n_exp) * rb
    assert depth == 1, "the final RMSNorm is fused into the single layer's combine"
    assert dh == LANES and seq % MOBA_BLOCK == 0 and sw * 2 == d and mw % sw == 0
    assert groups + n_exp <= ROUTER_ROWS and (t * MOE_TOPK) % rb == 0

    cos, sin = _rotary_tables(positions, dh)
    c_pad = jnp.zeros((8, d), _F32).at[:batch].set(c)
    h = x.reshape(t, d)

    for l in range(depth):
        ada = _ada(c_pad, w_ada[l], b_ada[l], _tile(6 * d, 1024))
        ada3 = ada[:batch].reshape(batch * 6, 1, d)

        proj = _inproj(h, ada3, norm_mix_g[l], w_in[l].astype(_BF16), cos, sin, sgu_ln_g[l], sgu_ln_b[l],
                       seq=seq, sw=sw, mw=mw, dh=dh, tm=_tile(seq, 1024))
        bs_wide = jnp.repeat(sgu_b_s[l].T, sw // sgu_w_s.shape[1], axis=1)
        m_a = _sgu(proj, sgu_w_s[l], bs_wide, w_sgu_out[l].astype(_BF16), sw=sw, mw=mw, tm=_tile(seq, 512))
        attn = _moba(proj, batch=batch, seq=seq, sw=sw, mw=mw, dh=dh, hg=min(MOBA_HEAD_GROUP, MOBA_HEADS))

        wr = jnp.concatenate([w_route_group[l], w_route_expert[l]], axis=1).T
        wr = jnp.zeros((ROUTER_ROWS, d), _F32).at[:groups + n_exp].set(wr)
        wr_hi, wr_lo = _split_bf16(wr)
        br = jnp.concatenate([b_route_group[l], b_route_expert[l].reshape(-1)])
        br = jnp.broadcast_to(jnp.zeros((ROUTER_ROWS,), _F32).at[:groups + n_exp].set(br)[:, None],
                              (ROUTER_ROWS, LANES))
        h1, n2, eid, wt = _mixout(attn, m_a, proj, h, ada3, norm_ffn_g[l], w_moba_out[l].astype(_BF16),
                                  w_out[l].astype(_BF16), wr_hi, wr_lo, br,
                                  seq=seq, sw=sw, mw=mw, groups=groups, epg=epg, tm=_tile(seq, 256))

        dest, tab = _plan(eid, n_exp=n_exp, tc=_tile(t, 512), rb=rb, n_blocks=cap // rb)
        tch = _tile(t, 2048)
        inv = _invert(dest.reshape(MOE_TOPK, t // tch, tch).transpose(1, 0, 2), cap)
        tk = _tile(seq, 256)
        unit = MOE_TOPK * max(tk, rb)
        n_trash = -(-3 * rb // unit) * unit
        y_assign = _experts(tab[0, :cap // rb], tab[1, :1], inv.reshape(cap // rb, 1, rb), n2,
                            w_exp_gate[l], w_exp_up[l], w_exp_down[l], n_trash=n_trash)
        w_cols = jnp.zeros((t, LANES), _F32).at[:, :MOE_TOPK].set(wt.T)
        h = _combine(h1, w_cols, ada3, norm_final_g, y_assign, seq=seq, tk=tk)

    return h.reshape(batch, seq, d)
```

```python
import functools

import jax
import jax.numpy as jnp
from jax import lax
from jax.experimental import pallas as pl
from jax.experimental.pallas import tpu as pltpu

MOBA_HEADS = 16
MOBA_BLOCK = 256
MOBA_TOPK = 3
SGU_CHUNK = 128
ROPE_THETA = 10000.0
MOE_GROUPS = 4
MOE_TOPK = 2
EPS = 1e-6
NEG_INF = -1e30
LOG2_E = 1.4426950408889634

LANES = 128
EXPERT_ROW_BLOCK = 256
INPROJ_CHUNK = 256
MOBA_HEAD_GROUP = 4
ROUTER_ROWS = 40
VMEM_LIMIT = 56 * 1024 * 1024

_F32 = jnp.float32
_BF16 = jnp.bfloat16
_NT = (((1,), (1,)), ((), ()))


def _params(semantics, vmem=VMEM_LIMIT):
    return pltpu.CompilerParams(dimension_semantics=semantics, vmem_limit_bytes=vmem)


def _split_bf16(a):
    hi = a.astype(_BF16)
    lo = (a - hi.astype(_F32)).astype(_BF16)
    return hi, lo


def _ada_kernel(c_ref, w_ref, b_ref, o_ref):
    ca_hi, ca_lo = _split_bf16(jax.nn.silu(c_ref[...]))
    w_hi, w_lo = _split_bf16(w_ref[...])
    acc = jnp.dot(ca_hi, w_hi, preferred_element_type=_F32)
    acc += jnp.dot(ca_hi, w_lo, preferred_element_type=_F32)
    acc += jnp.dot(ca_lo, w_hi, preferred_element_type=_F32)
    o_ref[...] = acc + b_ref[...]


def _ada(c_pad, w_ada, b_ada, tn):
    rows, d = c_pad.shape
    n = w_ada.shape[1]
    return pl.pallas_call(
        _ada_kernel,
        grid=(n // tn,),
        in_specs=[pl.BlockSpec((rows, d), lambda j: (0, 0)),
                  pl.BlockSpec((d, tn), lambda j: (0, j)),
                  pl.BlockSpec((1, tn), lambda j: (0, j))],
        out_specs=pl.BlockSpec((rows, tn), lambda j: (0, j)),
        out_shape=jax.ShapeDtypeStruct((rows, n), _F32),
        compiler_params=_params(("arbitrary",)),
        name="ada",
    )(c_pad, w_ada, b_ada.reshape(1, n))


def _rms_modulate(x, g, sc, sh):
    y = x * lax.rsqrt(jnp.mean(x * x, axis=-1, keepdims=True) + EPS) * g
    return y * (1.0 + sc) + sh


def _inproj_kernel(x_ref, g_ref, sc_ref, sh_ref, w_ref, cos_ref, sin_ref, lng_ref, lnb_ref,
                   o_ref, n_scr, z_scr, *, q0, k0, v0, dh, scale, cw):
    j = pl.program_id(1)
    tn = o_ref.shape[1]

    @pl.when(j == 0)
    def _():
        n_scr[...] = _rms_modulate(x_ref[...], g_ref[...], sc_ref[0], sh_ref[0]).astype(_BF16)

    def chunk(c):
        return jnp.dot(n_scr[...], w_ref[:, c * cw:(c + 1) * cw], preferred_element_type=_F32)

    @pl.when(j < q0)
    def _():
        for c in range(tn // cw):
            z_scr[:, c * cw:(c + 1) * cw] = jax.nn.gelu(chunk(c))
        z = z_scr[...]
        is_v = j == 1
        mu = jnp.where(is_v, jnp.mean(z, axis=-1, keepdims=True), 0.0)
        zc = z - mu
        var = jnp.mean(zc * zc, axis=-1, keepdims=True)
        gain = jnp.where(is_v, lax.rsqrt(var + EPS) * lng_ref[...], 1.0)
        o_ref[...] = (zc * gain + jnp.where(is_v, lnb_ref[...], 0.0)).astype(_BF16)

    @pl.when(j >= q0)
    def _():
        mult = jnp.where(j < k0, scale, 1.0)
        cos = jnp.where(j < v0, cos_ref[...], 1.0) * mult
        sin = jnp.where(j < v0, sin_ref[...], 0.0) * mult
        for c in range(tn // cw):
            acc = chunk(c)
            for h in range(cw // dh):
                xh = acc[:, h * dh:(h + 1) * dh]
                rot = xh * cos + pltpu.roll(xh, dh // 2, 1) * sin
                o_ref[:, c * cw + h * dh:c * cw + (h + 1) * dh] = rot.astype(_BF16)


def _inproj(x2, ada3, norm_g, w_in, cos, sin, ln_g, ln_b, *, seq, sw, mw, dh, tm):
    t, d = x2.shape
    n = w_in.shape[1]
    tn = sw
    per_b = seq // tm
    q0 = 2
    k0 = q0 + mw // tn
    v0 = k0 + mw // tn
    cw = _tile(tn, INPROJ_CHUNK)
    assert cw % dh == 0
    kern = functools.partial(_inproj_kernel, q0=q0, k0=k0, v0=v0, dh=dh, scale=dh ** -0.5 * LOG2_E, cw=cw)
    return pl.pallas_call(
        kern,
        grid=(t // tm, n // tn),
        in_specs=[pl.BlockSpec((tm, d), lambda i, j: (i, 0)),
                  pl.BlockSpec((1, d), lambda i, j: (0, 0)),
                  pl.BlockSpec((1, 1, d), lambda i, j: ((i // per_b) * 6 + 1, 0, 0)),
                  pl.BlockSpec((1, 1, d), lambda i, j: ((i // per_b) * 6 + 0, 0, 0)),
                  pl.BlockSpec((d, tn), lambda i, j: (0, j)),
                  pl.BlockSpec((tm, dh), lambda i, j: (i, 0)),
                  pl.BlockSpec((tm, dh), lambda i, j: (i, 0)),
                  pl.BlockSpec((1, sw), lambda i, j: (0, 0)),
                  pl.BlockSpec((1, sw), lambda i, j: (0, 0))],
        out_specs=pl.BlockSpec((tm, tn), lambda i, j: (i, j)),
        out_shape=jax.ShapeDtypeStruct((t, n), _BF16),
        scratch_shapes=[pltpu.VMEM((tm, d), _BF16), pltpu.VMEM((tm, tn), _F32)],
        compiler_params=_params(("arbitrary", "arbitrary")),
        name="inproj",
    )(x2, norm_g.reshape(1, d), ada3, ada3, w_in, cos, sin, ln_g.reshape(1, sw), ln_b.reshape(1, sw))


def _sgu_kernel(u_ref, v_ref, ga_ref, ws_ref, bs_ref, wo_ref, o_ref, gated_scr, *, groups, cg):
    c = SGU_CHUNK
    tri = lax.broadcasted_iota(jnp.int32, (c, c), 0) >= lax.broadcasted_iota(jnp.int32, (c, c), 1)
    for g in range(groups):
        wg = jnp.where(tri, ws_ref[g], 0.0).astype(_BF16)
        cols = slice(g * cg, (g + 1) * cg)
        for ci in range(u_ref.shape[0] // c):
            rows = slice(ci * c, (ci + 1) * c)
            sv = jnp.dot(wg, v_ref[rows, cols], preferred_element_type=_F32) + bs_ref[:, cols]
            gated_scr[rows, cols] = (u_ref[rows, cols].astype(_F32) * sv).astype(_BF16)
    ya = jnp.dot(gated_scr[...], wo_ref[...], preferred_element_type=_F32)
    o_ref[...] = (jax.nn.sigmoid(ga_ref[...].astype(_F32)) * ya).astype(_BF16)


def _sgu(proj, w_s, bs_wide, w_sgu_out, *, sw, mw, tm):
    t = proj.shape[0]
    d = w_sgu_out.shape[1]
    groups = w_s.shape[0]
    ga_blk = (2 * sw + 3 * mw) // d
    kern = functools.partial(_sgu_kernel, groups=groups, cg=sw // groups)
    return pl.pallas_call(
        kern,
        grid=(t // tm,),
        in_specs=[pl.BlockSpec((tm, sw), lambda i: (i, 0)),
                  pl.BlockSpec((tm, sw), lambda i: (i, 1)),
                  pl.BlockSpec((tm, d), lambda i: (i, ga_blk)),
                  pl.BlockSpec(w_s.shape, lambda i: (0, 0, 0)),
                  pl.BlockSpec(bs_wide.shape, lambda i: (0, 0)),
                  pl.BlockSpec(w_sgu_out.shape, lambda i: (0, 0))],
        out_specs=pl.BlockSpec((tm, d), lambda i: (i, 0)),
        out_shape=jax.ShapeDtypeStruct((t, d), _BF16),
        scratch_shapes=[pltpu.VMEM((tm, sw), _BF16)],
        compiler_params=_params(("arbitrary",)),
        name="sgu",
    )(proj, proj, proj, w_s, bs_wide, w_sgu_out)


def _moba_kernel(q_ref, k_ref, v_ref, o_ref, kmh_scr, kml_scr, vt_scr, sel_scr,
                 s_scr, p_scr, acc_scr, m_scr, l_scr, a_scr, *, nb, topk, hg, dh):
    qi = pl.program_id(2)
    blk = MOBA_BLOCK
    tq = q_ref.shape[0]

    @pl.when(qi == 0)
    def _():
        for h in range(hg):
            cols = slice(h * dh, (h + 1) * dh)
            kf = k_ref[:, cols].astype(_F32).reshape(nb, blk, dh)
            km_hi, km_lo = _split_bf16(jnp.mean(kf, axis=1))
            kmh_scr[h * nb:(h + 1) * nb, :] = km_hi
            kml_scr[h * nb:(h + 1) * nb, :] = km_lo
            for c in range(nb):
                rows = slice(c * blk, (c + 1) * blk)
                vt_scr[cols, rows] = v_ref[rows, cols].astype(_F32).T.astype(_BF16)

    row = lax.broadcasted_iota(jnp.int32, (nb, tq), 0)
    past = row < qi
    kpos = lax.broadcasted_iota(jnp.int32, (blk, tq), 0)
    qpos = lax.broadcasted_iota(jnp.int32, (blk, tq), 1)
    own = pl.multiple_of(qi * blk, blk)

    def scores(start, h):
        cols = slice(h * dh, (h + 1) * dh)
        return lax.dot_general(k_ref[pl.ds(start, blk), cols], q_ref[:, cols], _NT,
                               preferred_element_type=_F32)

    def weighted_values(start, h, p):
        return jnp.dot(vt_scr[h * dh:(h + 1) * dh, pl.ds(start, blk)], p, preferred_element_type=_F32)

    for h in range(hg):
        hrows = slice(h * nb, (h + 1) * nb)
        q = q_ref[:, h * dh:(h + 1) * dh]
        gate = (lax.dot_general(kmh_scr[hrows, :], q, _NT, preferred_element_type=_F32)
                + lax.dot_general(kml_scr[hrows, :], q, _NT, preferred_element_type=_F32))
        gm = jnp.where(past, gate, NEG_INF)
        beaten = jnp.zeros((nb, tq), jnp.int32)
        for jp in range(nb):
            other = gm[jp:jp + 1, :]
            wins = (other > gm) | ((other == gm) & (jp < row))
            beaten += wins.astype(jnp.int32)
        sel_scr[hrows, :] = (past & (beaten < topk)).astype(_F32)

        s = jnp.where(kpos <= qpos, scores(own, h), NEG_INF)
        m0 = jnp.max(s, axis=0, keepdims=True)
        p = jnp.exp2(s - m0)
        m_scr[h] = m0
        l_scr[h] = jnp.sum(p, axis=0, keepdims=True)
        a_scr[h] = jnp.ones_like(m0)
        acc_scr[h] = jnp.zeros(acc_scr.shape[1:], _F32)
        p_scr[0, h] = p.astype(_BF16)
        s_scr[0, h] = scores(0, h)

    def body(j, c):
        par = lax.rem(j, 2)
        prev = pl.multiple_of(jnp.where(j == 0, qi, j - 1) * blk, blk)
        nxt = pl.multiple_of(jnp.minimum(j + 1, qi - 1) * blk, blk)
        for h in range(hg):
            acc_scr[h] = a_scr[h] * acc_scr[h] + weighted_values(prev, h, p_scr[par, h])
        for h in range(hg):
            s = jnp.where(sel_scr[pl.ds(h * nb + j, 1), :] > 0.0, s_scr[par, h], NEG_INF)
            m = m_scr[h]
            m_new = jnp.maximum(m, jnp.max(s, axis=0, keepdims=True))
            alpha = jnp.exp2(m - m_new)
            p = jnp.exp2(s - m_new)
            l_scr[h] = alpha * l_scr[h] + jnp.sum(p, axis=0, keepdims=True)
            m_scr[h] = m_new
            a_scr[h] = alpha
            p_scr[1 - par, h] = p.astype(_BF16)
        for h in range(hg):
            s_scr[1 - par, h] = scores(nxt, h)
        return c

    lax.fori_loop(0, qi, body, 0)
    last = pl.multiple_of(jnp.where(qi == 0, qi, qi - 1) * blk, blk)
    for h in range(hg):
        acc = a_scr[h] * acc_scr[h] + weighted_values(last, h, p_scr[lax.rem(qi, 2), h])
        o_ref[:, h * dh:(h + 1) * dh] = (acc / l_scr[h]).T.astype(_BF16)


def _moba(proj, *, batch, seq, sw, mw, dh, hg):
    t = proj.shape[0]
    heads = mw // dh
    nb = seq // MOBA_BLOCK
    tq = MOBA_BLOCK
    nq = seq // tq
    gw = hg * dh
    qc = 2 * sw // gw
    kc = qc + heads // hg
    vc = kc + heads // hg
    kern = functools.partial(_moba_kernel, nb=nb, topk=MOBA_TOPK, hg=hg, dh=dh)
    return pl.pallas_call(
        kern,
        grid=(batch, heads // hg, nq),
        in_specs=[pl.BlockSpec((tq, gw), lambda b, h, i: (b * nq + i, qc + h)),
                  pl.BlockSpec((seq, gw), lambda b, h, i: (b, kc + h)),
                  pl.BlockSpec((seq, gw), lambda b, h, i: (b, vc + h))],
        out_specs=pl.BlockSpec((tq, gw), lambda b, h, i: (b * nq + i, h)),
        out_shape=jax.ShapeDtypeStruct((t, mw), _BF16),
        scratch_shapes=[pltpu.VMEM((hg * nb, dh), _BF16), pltpu.VMEM((hg * nb, dh), _BF16),
                        pltpu.VMEM((gw, seq), _BF16), pltpu.VMEM((hg * nb, tq), _F32),
                        pltpu.VMEM((2, hg, MOBA_BLOCK, tq), _F32), pltpu.VMEM((2, hg, MOBA_BLOCK, tq), _BF16),
                        pltpu.VMEM((hg, dh, tq), _F32), pltpu.VMEM((hg, 1, tq), _F32),
                        pltpu.VMEM((hg, 1, tq), _F32), pltpu.VMEM((hg, 1, tq), _F32)],
        compiler_params=_params(("arbitrary", "arbitrary", "arbitrary")),
        name="moba",
    )(proj, proj, proj)


def _mixout_kernel(attn_ref, ma_ref, gb_ref, x_ref, gm_ref, shf_ref, scf_ref, ng_ref,
                   wmo_ref, wo_ref, wrh_ref, wrl_ref, br_ref,
                   h_ref, n2_ref, eid_ref, wt_ref, *, groups, epg):
    yb = jnp.dot(attn_ref[...], wmo_ref[...], preferred_element_type=_F32)
    merged = ma_ref[...].astype(_F32) + jax.nn.sigmoid(gb_ref[...].astype(_F32)) * yb
    mix = jnp.dot(merged.astype(_BF16), wo_ref[...], preferred_element_type=_F32)
    h = x_ref[...] + gm_ref[0] * mix
    h_ref[...] = h
    n2 = _rms_modulate(h, ng_ref[...], scf_ref[0], shf_ref[0])
    n2_ref[...] = n2

    n_hi, n_lo = _split_bf16(n2)
    lg = (lax.dot_general(wrh_ref[...], n_hi, _NT, preferred_element_type=_F32)
          + lax.dot_general(wrh_ref[...], n_lo, _NT, preferred_element_type=_F32)
          + lax.dot_general(wrl_ref[...], n_hi, _NT, preferred_element_type=_F32)) + br_ref[:, 0:1]

    gl = [lg[g:g + 1, :] for g in range(groups)]
    gmax = functools.reduce(jnp.maximum, gl)
    denom = functools.reduce(jnp.add, [jnp.exp(v - gmax) for v in gl])
    p_group = 1.0 / denom
    g_sel = jnp.full(gmax.shape, groups - 1, jnp.int32)
    for g in range(groups - 2, -1, -1):
        g_sel = jnp.where(gl[g] == gmax, g, g_sel)

    el = []
    for e in range(epg):
        v = lg[groups + e:groups + e + 1, :]
        for g in range(1, groups):
            r = groups + g * epg + e
            v = jnp.where(g_sel == g, lg[r:r + 1, :], v)
        el.append(v)

    def top1(vals):
        vmax = functools.reduce(jnp.maximum, vals)
        idx = jnp.full(vmax.shape, epg - 1, jnp.int32)
        for e in range(epg - 2, -1, -1):
            idx = jnp.where(vals[e] == vmax, e, idx)
        return vmax, idx

    v1, i1 = top1(el)
    v2, i2 = top1([jnp.where(i1 == e, -jnp.inf, el[e]) for e in range(epg)])
    b = jnp.exp(v2 - v1)
    eid_ref[0:1, :] = g_sel * epg + i1
    eid_ref[1:2, :] = g_sel * epg + i2
    wt_ref[0:1, :] = (1.0 / (1.0 + b)) * p_group
    wt_ref[1:2, :] = (b / (1.0 + b)) * p_group


def _mixout(attn, m_a, proj, x2, ada3, norm_g, w_moba_out, w_out, wr_hi, wr_lo, br,
            *, seq, sw, mw, groups, epg, tm):
    t, d = x2.shape
    per_b = seq // tm
    gb_blk = (2 * sw + 3 * mw) // d + 1
    once = pl.Buffered(1)
    kern = functools.partial(_mixout_kernel, groups=groups, epg=epg)

    def ada_spec(k):
        return pl.BlockSpec((1, 1, d), lambda i: ((i // per_b) * 6 + k, 0, 0))

    return pl.pallas_call(
        kern,
        grid=(t // tm,),
        in_specs=[pl.BlockSpec((tm, mw), lambda i: (i, 0)),
                  pl.BlockSpec((tm, d), lambda i: (i, 0)),
                  pl.BlockSpec((tm, d), lambda i: (i, gb_blk)),
                  pl.BlockSpec((tm, d), lambda i: (i, 0)),
                  ada_spec(2), ada_spec(3), ada_spec(4),
                  pl.BlockSpec((1, d), lambda i: (0, 0)),
                  pl.BlockSpec((mw, d), lambda i: (0, 0), pipeline_mode=once),
                  pl.BlockSpec((d, d), lambda i: (0, 0), pipeline_mode=once),
                  pl.BlockSpec((ROUTER_ROWS, d), lambda i: (0, 0)),
                  pl.BlockSpec((ROUTER_ROWS, d), lambda i: (0, 0)),
                  pl.BlockSpec((ROUTER_ROWS, LANES), lambda i: (0, 0))],
        out_specs=[pl.BlockSpec((tm, d), lambda i: (i, 0)),
                   pl.BlockSpec((tm, d), lambda i: (i, 0)),
                   pl.BlockSpec((MOE_TOPK, tm), lambda i: (0, i)),
                   pl.BlockSpec((MOE_TOPK, tm), lambda i: (0, i))],
        out_shape=[jax.ShapeDtypeStruct((t, d), _F32),
                   jax.ShapeDtypeStruct((t, d), _F32),
                   jax.ShapeDtypeStruct((MOE_TOPK, t), jnp.int32),
                   jax.ShapeDtypeStruct((MOE_TOPK, t), _F32)],
        compiler_params=_params(("arbitrary",)),
        name="mixout",
    )(attn, m_a, proj, x2, ada3, ada3, ada3, norm_g.reshape(1, d), w_moba_out, w_out, wr_hi, wr_lo, br)


def _plan_kernel(eid_ref, dest_ref, tab_ref, carry_scr, start_scr, tri_scr, *, n_exp, rb):
    phase = pl.program_id(0)
    first = pl.program_id(1) == 0
    tc = eid_ref.shape[1]
    nbp = tab_ref.shape[1]

    @pl.when((phase == 0) & first)
    def _():
        carry_scr[...] = jnp.zeros_like(carry_scr)
        earlier = lax.broadcasted_iota(jnp.int32, (tc, tc), 0) < lax.broadcasted_iota(jnp.int32, (tc, tc), 1)
        tri_scr[...] = earlier.astype(_BF16)

    eio = lax.broadcasted_iota(jnp.int32, (n_exp, tc), 0)
    oh0 = (eio == eid_ref[0:1, :]).astype(_F32)
    oh1 = (eio == eid_ref[1:2, :]).astype(_F32)
    oh = oh0 + oh1

    @pl.when((phase == 1) & first)
    def _():
        counts = carry_scr[...].astype(jnp.int32)
        blocks = lax.shift_right_logical(counts + (rb - 1), rb.bit_length() - 1).astype(_F32)
        upto = lax.broadcasted_iota(jnp.int32, (n_exp, n_exp), 0) >= lax.broadcasted_iota(jnp.int32, (n_exp, n_exp), 1)
        end_blk = jnp.dot(upto.astype(_BF16), blocks.astype(_BF16), preferred_element_type=_F32)
        start_scr[...] = (end_blk - blocks) * rb
        carry_scr[...] = jnp.zeros_like(carry_scr)
        blk_id = lax.broadcasted_iota(jnp.int32, (n_exp, nbp), 1).astype(_F32)
        owner = jnp.sum((end_blk[:, 0:1] <= blk_id).astype(_F32), axis=0, keepdims=True)
        owner = jnp.minimum(owner, n_exp - 1.0)
        used = jnp.broadcast_to(end_blk[n_exp - 1:n_exp, 0:1], (1, nbp))
        trow = lax.broadcasted_iota(jnp.int32, tab_ref.shape, 0)
        tab_ref[...] = jnp.where(trow == 0, owner, jnp.where(trow == 1, used, 0.0)).astype(jnp.int32)

    @pl.when(phase == 1)
    def _():
        before = jnp.dot(oh.astype(_BF16), tri_scr[...], preferred_element_type=_F32)
        base = start_scr[:, 0:1] + carry_scr[:, 0:1] + before
        dest_ref[0:1, :] = jnp.sum(oh0 * base, axis=0, keepdims=True).astype(jnp.int32)
        dest_ref[1:2, :] = jnp.sum(oh1 * base, axis=0, keepdims=True).astype(jnp.int32)

    carry_scr[...] = carry_scr[...] + jnp.sum(oh, axis=1, keepdims=True)


def _plan(eid, *, n_exp, tc, rb, n_blocks):
    t = eid.shape[1]
    assert rb & (rb - 1) == 0 and t * MOE_TOPK // rb + n_exp < 256
    nbp = -(-n_blocks // LANES) * LANES
    kern = functools.partial(_plan_kernel, n_exp=n_exp, rb=rb)
    return pl.pallas_call(
        kern,
        grid=(2, t // tc),
        in_specs=[pl.BlockSpec((MOE_TOPK, tc), lambda p, i: (0, i))],
        out_specs=[pl.BlockSpec((MOE_TOPK, tc), lambda p, i: (0, i * p)),
                   pl.BlockSpec((8, nbp), lambda p, i: (0, 0))],
        out_shape=[jax.ShapeDtypeStruct((MOE_TOPK, t), jnp.int32),
                   jax.ShapeDtypeStruct((8, nbp), jnp.int32)],
        scratch_shapes=[pltpu.VMEM((n_exp, LANES), _F32), pltpu.VMEM((n_exp, LANES), _F32),
                        pltpu.VMEM((tc, tc), _BF16)],
        compiler_params=_params(("arbitrary", "arbitrary")),
        name="plan",
    )(eid)


def _invert_kernel(dest_ref, inv_ref):
    i = pl.program_id(0)
    tch = dest_ref.shape[1]

    @pl.when(i == 0)
    def _():
        def init(r, c):
            inv_ref[r] = -1
            return c

        lax.fori_loop(0, inv_ref.shape[0], init, 0, unroll=8)

    def body(t, c):
        a = (i * tch + t) * MOE_TOPK
        for k in range(MOE_TOPK):
            inv_ref[dest_ref[k, t]] = a + k
        return c

    lax.fori_loop(0, tch, body, 0, unroll=8)


def _invert(dest3, cap):
    steps, _, tch = dest3.shape
    return pl.pallas_call(
        _invert_kernel,
        grid=(steps,),
        in_specs=[pl.BlockSpec((None, MOE_TOPK, tch), lambda i: (i, 0, 0), memory_space=pltpu.SMEM)],
        out_specs=pl.BlockSpec((cap,), lambda i: (0,), memory_space=pltpu.SMEM),
        out_shape=jax.ShapeDtypeStruct((cap,), jnp.int32),
        compiler_params=_params(("arbitrary",)),
        name="invert",
    )(dest3)


def _expert_kernel(be_ref, nu_ref, invp_ref, invc_ref, invn_ref, n2_ref, wg_ref, wu_ref, wd_ref, y_ref,
                   wg_s, wu_s, wd_s, xbuf, ybuf, gsem, ssem, *, rb, n_assign):
    i = pl.program_id(0)
    last = nu_ref[0] - 1
    used = i <= last
    fresh = (i == 0) | (be_ref[i] != be_ref[jnp.maximum(i - 1, 0)])
    xs = lax.rem(i, 2)
    ys = lax.rem(i, 3)
    yprev = lax.rem(i + 2, 3)

    def gather(inv_ref, slot):
        for r in range(rb):
            tok = lax.shift_right_logical(jnp.maximum(inv_ref[0, r], 0), 1)
            pltpu.make_async_copy(n2_ref.at[pl.ds(tok, 1)], xbuf.at[slot, pl.ds(r, 1)],
                                  gsem.at[slot]).start(priority=r % 2)

    def gather_wait(slot):
        pltpu.make_async_copy(n2_ref.at[pl.ds(0, rb)], xbuf.at[slot], gsem.at[slot]).wait()

    d = xbuf.shape[2]

    def scatter(inv_ref, slot, live):
        for r in range(rb):
            a = inv_ref[0, r]
            row = jnp.where((a >= 0) & live, a, n_assign + slot * rb + r)
            col = pl.multiple_of((row & 1) * d, d)
            dst = y_ref.at[pl.ds(lax.shift_right_logical(row, 1), 1), pl.ds(col, d)]
            pltpu.make_async_copy(ybuf.at[slot, pl.ds(r, 1)], dst, ssem.at[slot]).start(priority=r % 2)

    def scatter_wait(slot):
        pltpu.make_async_copy(ybuf.at[slot], y_ref.at[pl.ds(0, rb), pl.ds(0, d)], ssem.at[slot]).wait()

    @pl.when(i == 0)
    def _():
        gather(invc_ref, 0)
        ybuf[2] = jnp.zeros(ybuf.shape[1:], _F32)
        fills = [pltpu.make_async_copy(ybuf.at[2], y_ref.at[pl.ds(row0, rb), pl.ds(half * d, d)], ssem.at[2])
                 for row0 in range(n_assign // MOE_TOPK, y_ref.shape[0], rb) for half in range(MOE_TOPK)]
        for fill in fills:
            fill.start()
        for fill in fills:
            fill.wait()

    @pl.when(used & fresh)
    def _():
        wg_s[...] = wg_ref[0].astype(_BF16)
        wu_s[...] = wu_ref[0].astype(_BF16)
        wd_s[...] = wd_ref[0].astype(_BF16)

    @pl.when(used & (i >= 2))
    def _():
        scatter_wait(ys)

    @pl.when(used)
    def _():
        gather_wait(xs)
        gather(invn_ref, 1 - xs)
        scatter(invp_ref, yprev, i > 0)
        xb = xbuf[xs].astype(_BF16)
        gate = jnp.dot(xb, wg_s[...], preferred_element_type=_F32)
        up = jnp.dot(xb, wu_s[...], preferred_element_type=_F32)
        hid = (jax.nn.silu(gate) * up).astype(_BF16)
        ybuf[ys] = jnp.dot(hid, wd_s[...], preferred_element_type=_F32)

    @pl.when(i == last)
    def _():
        scatter(invc_ref, ys, True)
        scatter_wait(ys)
        scatter_wait(yprev)
        gather_wait(1 - xs)

        @pl.when(last >= 1)
        def _():
            scatter_wait(lax.rem(i + 1, 3))


def _experts(block_expert, n_used, inv3, n2, w_gate, w_up, w_down, *, n_trash):
    nblk, _, rb = inv3.shape
    t, d = n2.shape
    f = w_gate.shape[2]
    n_assign = t * MOE_TOPK
    assert n_trash >= 3 * rb and n_trash % (MOE_TOPK * rb) == 0

    def inv_spec(shift):
        def index(i, be, nu):
            return (jnp.clip(i + shift, 0, nu[0] - 1), 0, 0)
        return pl.BlockSpec((None, 1, rb), index, memory_space=pltpu.SMEM)

    grid_spec = pltpu.PrefetchScalarGridSpec(
        num_scalar_prefetch=2,
        grid=(nblk,),
        in_specs=[inv_spec(-1), inv_spec(0), inv_spec(1),
                  pl.BlockSpec(memory_space=pl.ANY),
                  pl.BlockSpec((1, d, f), lambda i, be, nu: (be[i], 0, 0)),
                  pl.BlockSpec((1, d, f), lambda i, be, nu: (be[i], 0, 0)),
                  pl.BlockSpec((1, f, d), lambda i, be, nu: (be[i], 0, 0))],
        out_specs=pl.BlockSpec(memory_space=pl.ANY),
        scratch_shapes=[pltpu.VMEM((d, f), _BF16), pltpu.VMEM((d, f), _BF16), pltpu.VMEM((f, d), _BF16),
                        pltpu.VMEM((2, rb, d), _F32), pltpu.VMEM((3, rb, d), _F32),
                        pltpu.SemaphoreType.DMA((2,)), pltpu.SemaphoreType.DMA((3,))],
    )
    kern = functools.partial(_expert_kernel, rb=rb, n_assign=n_assign)
    return pl.pallas_call(
        kern,
        grid_spec=grid_spec,
        out_shape=jax.ShapeDtypeStruct(((n_assign + n_trash) // MOE_TOPK, MOE_TOPK * d), _F32),
        compiler_params=_params(("arbitrary",)),
        name="experts",
    )(block_expert, n_used, inv3, inv3, inv3, n2, w_gate, w_up, w_down)


def _combine_kernel(h_ref, wc_ref, gf_ref, ng_ref, y_ref, o_ref):
    d = h_ref.shape[1]
    moe = wc_ref[:, 0:1] * y_ref[:, 0:d] + wc_ref[:, 1:2] * y_ref[:, d:2 * d]
    h = h_ref[...] + gf_ref[0] * moe
    o_ref[...] = h * lax.rsqrt(jnp.mean(h * h, axis=-1, keepdims=True) + EPS) * ng_ref[...]


def _combine(h1, w_cols, ada3, norm_g, y_assign, *, seq, tk):
    t, d = h1.shape
    per_b = seq // tk
    return pl.pallas_call(
        _combine_kernel,
        grid=(t // tk,),
        in_specs=[pl.BlockSpec((tk, d), lambda i: (i, 0)),
                  pl.BlockSpec((tk, LANES), lambda i: (i, 0)),
                  pl.BlockSpec((1, 1, d), lambda i: ((i // per_b) * 6 + 5, 0, 0)),
                  pl.BlockSpec((1, d), lambda i: (0, 0)),
                  pl.BlockSpec((tk, MOE_TOPK * d), lambda i: (i, 0))],
        out_specs=pl.BlockSpec((tk, d), lambda i: (i, 0)),
        out_shape=jax.ShapeDtypeStruct((t, d), _F32),
        compiler_params=_params(("arbitrary",)),
        name="combine",
    )(h1, w_cols, ada3, norm_g.reshape(1, d), y_assign)


def _rotary_tables(positions, dh):
    inv_freq = ROPE_THETA ** (-jnp.arange(0, dh, 2, dtype=_F32) / dh)
    ang = positions.astype(_F32)[..., None] * inv_freq
    cos, sin = jnp.cos(ang), jnp.sin(ang)
    t = cos.shape[0] * cos.shape[1]
    cos = jnp.concatenate([cos, cos], axis=-1).reshape(t, dh)
    sin = jnp.concatenate([-sin, sin], axis=-1).reshape(t, dh)
    return cos, sin


def _tile(n, want):
    while n % want:
        want //= 2
    return want


def kernel(x, c, positions, w_ada, b_ada, norm_mix_g, w_in, sgu_ln_g, sgu_ln_b, sgu_w_s, sgu_b_s,
           w_sgu_out, w_moba_out, w_out, norm_ffn_g, w_route_group, b_route_group, w_route_expert,
           b_route_expert, w_exp_gate, w_exp_up, w_exp_down, norm_final_g):
    batch, seq, d = x.shape
    depth = w_ada.shape[0]
    t = batch * seq
    sw = sgu_ln_g.shape[1]
    mw = w_moba_out.shape[1]
    dh = mw // MOBA_HEADS
    groups = w_route_group.shape[2]
    n_exp = w_route_expert.shape[2]
    epg = n_exp // groups
    rb = EXPERT_ROW_BLOCK
    cap = (t * MOE_TOPK // rb + n_exp) * rb
    assert depth == 1, "the final RMSNorm is fused into the single layer's combine"
    assert dh == LANES and seq % MOBA_BLOCK == 0 and sw * 2 == d and mw % sw == 0
    assert groups + n_exp <= ROUTER_ROWS and (t * MOE_TOPK) % rb == 0

    cos, sin = _rotary_tables(positions, dh)
    c_pad = jnp.zeros((8, d), _F32).at[:batch].set(c)
    h = x.reshape(t, d)

    for l in range(depth):
        ada = _ada(c_pad, w_ada[l], b_ada[l], _tile(6 * d, 1024))
        ada3 = ada[:batch].reshape(batch * 6, 1, d)

        proj = _inproj(h, ada3, norm_mix_g[l], w_in[l].astype(_BF16), cos, sin, sgu_ln_g[l], sgu_ln_b[l],
                       seq=seq, sw=sw, mw=mw, dh=dh, tm=_tile(seq, 1024))
        bs_wide = jnp.repeat(sgu_b_s[l].T, sw // sgu_w_s.shape[1], axis=1)
        m_a = _sgu(proj, sgu_w_s[l], bs_wide, w_sgu_out[l].astype(_BF16), sw=sw, mw=mw, tm=_tile(seq, 512))
        attn = _moba(proj, batch=batch, seq=seq, sw=sw, mw=mw, dh=dh, hg=min(MOBA_HEAD_GROUP, MOBA_HEADS))

        wr = jnp.concatenate([w_route_group[l], w_route_expert[l]], axis=1).T
        wr = jnp.zeros((ROUTER_ROWS, d), _F32).at[:groups + n_exp].set(wr)
        wr_hi, wr_lo = _split_bf16(wr)
        br = jnp.concatenate([b_route_group[l], b_route_expert[l].reshape(-1)])
        br = jnp.broadcast_to(jnp.zeros((ROUTER_ROWS,), _F32).at[:groups + n_exp].set(br)[:, None],
                              (ROUTER_ROWS, LANES))
        h1, n2, eid, wt = _mixout(attn, m_a, proj, h, ada3, norm_ffn_g[l], w_moba_out[l].astype(_BF16),
                                  w_out[l].astype(_BF16), wr_hi, wr_lo, br,
                                  seq=seq, sw=sw, mw=mw, groups=groups, epg=epg, tm=_tile(seq, 256))

        dest, tab = _plan(eid, n_exp=n_exp, tc=_tile(t, 512), rb=rb, n_blocks=cap // rb)
        tch = _tile(t, 2048)
        inv = _invert(dest.reshape(MOE_TOPK, t // tch, tch).transpose(1, 0, 2), cap)
        tk = _tile(seq, 256)
        unit = MOE_TOPK * max(tk, rb)
        n_trash = -(-3 * rb // unit) * unit
        y_assign = _experts(tab[0, :cap // rb], tab[1, :1], inv.reshape(cap // rb, 1, rb), n2,
                            w_exp_gate[l], w_exp_up[l], w_exp_down[l], n_trash=n_trash)
        w_cols = jnp.zeros((t, LANES), _F32).at[:, :MOE_TOPK].set(wt.T)
        h = _combine(h1, w_cols, ada3, norm_final_g, y_assign, seq=seq, tk=tk)

    return h.reshape(batch, seq, d)
```

```python
import functools

import jax
import jax.numpy as jnp
from jax import lax
from jax.experimental import pallas as pl
from jax.experimental.pallas import tpu as pltpu

MOBA_HEADS = 16
MOBA_BLOCK = 256
MOBA_TOPK = 3
SGU_CHUNK = 128
ROPE_THETA = 10000.0
MOE_GROUPS = 4
MOE_TOPK = 2
EPS = 1e-6
NEG_INF = -1e30
LOG2_E = 1.4426950408889634

LANES = 128
TOKEN_PAD_ROWS = 8
EXPERT_ROW_BLOCK = 256
INPROJ_CHUNK = 256
MOBA_HEAD_GROUP = 4
ROUTER_ROWS = 40
VMEM_LIMIT = 56 * 1024 * 1024

_F32 = jnp.float32
_BF16 = jnp.bfloat16
_NT = (((1,), (1,)), ((), ()))


def _params(semantics, vmem=VMEM_LIMIT):
    return pltpu.CompilerParams(dimension_semantics=semantics, vmem_limit_bytes=vmem)


def _pitch(d):
    return d // LANES + TOKEN_PAD_ROWS


def _store_token_major(ref, val, zero_pad=True):
    n, d = val.shape
    pitch = ref.shape[0] // n
    for c in range(pitch if zero_pad else d // LANES):
        piece = val[:, c * LANES:(c + 1) * LANES] if c < d // LANES else jnp.zeros((n, LANES), val.dtype)
        ref[pl.ds(c, n, stride=pitch), :] = piece


def _load_token_major(ref, n, d, first=0, pitch=None):
    pitch = pitch or ref.shape[0] // n
    return jnp.concatenate([ref[pl.ds(first + c, n, stride=pitch), :] for c in range(d // LANES)], axis=1)


def _split_bf16(a):
    hi = a.astype(_BF16)
    lo = (a - hi.astype(_F32)).astype(_BF16)
    return hi, lo


def _ada_kernel(c_ref, w_ref, b_ref, o_ref):
    ca_hi, ca_lo = _split_bf16(jax.nn.silu(c_ref[...]))
    w_hi, w_lo = _split_bf16(w_ref[...])
    acc = jnp.dot(ca_hi, w_hi, preferred_element_type=_F32)
    acc += jnp.dot(ca_hi, w_lo, preferred_element_type=_F32)
    acc += jnp.dot(ca_lo, w_hi, preferred_element_type=_F32)
    o_ref[...] = acc + b_ref[...]


def _ada(c_pad, w_ada, b_ada, tn):
    rows, d = c_pad.shape
    n = w_ada.shape[1]
    return pl.pallas_call(
        _ada_kernel,
        grid=(n // tn,),
        in_specs=[pl.BlockSpec((rows, d), lambda j: (0, 0)),
                  pl.BlockSpec((d, tn), lambda j: (0, j)),
                  pl.BlockSpec((1, tn), lambda j: (0, j))],
        out_specs=pl.BlockSpec((rows, tn), lambda j: (0, j)),
        out_shape=jax.ShapeDtypeStruct((rows, n), _F32),
        compiler_params=_params(("arbitrary",)),
        name="ada",
    )(c_pad, w_ada, b_ada.reshape(1, n))


def _rms_modulate(x, g, sc, sh):
    y = x * lax.rsqrt(jnp.mean(x * x, axis=-1, keepdims=True) + EPS) * g
    return y * (1.0 + sc) + sh


def _inproj_kernel(x_ref, g_ref, sc_ref, sh_ref, w_ref, cos_ref, sin_ref, lng_ref, lnb_ref,
                   o_ref, n_scr, z_scr, *, q0, k0, v0, dh, scale, cw):
    j = pl.program_id(1)
    tn = o_ref.shape[1]

    @pl.when(j == 0)
    def _():
        n_scr[...] = _rms_modulate(x_ref[...], g_ref[...], sc_ref[0], sh_ref[0]).astype(_BF16)

    def chunk(c):
        return jnp.dot(n_scr[...], w_ref[:, c * cw:(c + 1) * cw], preferred_element_type=_F32)

    @pl.when(j < q0)
    def _():
        for c in range(tn // cw):
            z_scr[:, c * cw:(c + 1) * cw] = jax.nn.gelu(chunk(c))
        z = z_scr[...]
        is_v = j == 1
        mu = jnp.where(is_v, jnp.mean(z, axis=-1, keepdims=True), 0.0)
        zc = z - mu
        var = jnp.mean(zc * zc, axis=-1, keepdims=True)
        gain = jnp.where(is_v, lax.rsqrt(var + EPS) * lng_ref[...], 1.0)
        o_ref[...] = (zc * gain + jnp.where(is_v, lnb_ref[...], 0.0)).astype(_BF16)

    @pl.when(j >= q0)
    def _():
        mult = jnp.where(j < k0, scale, 1.0)
        cos = jnp.where(j < v0, cos_ref[...], 1.0) * mult
        sin = jnp.where(j < v0, sin_ref[...], 0.0) * mult
        for c in range(tn // cw):
            acc = chunk(c)
            for h in range(cw // dh):
                xh = acc[:, h * dh:(h + 1) * dh]
                rot = xh * cos + pltpu.roll(xh, dh // 2, 1) * sin
                o_ref[:, c * cw + h * dh:c * cw + (h + 1) * dh] = rot.astype(_BF16)


def _inproj(x2, ada3, norm_g, w_in, cos, sin, ln_g, ln_b, *, seq, sw, mw, dh, tm):
    t, d = x2.shape
    n = w_in.shape[1]
    tn = sw
    per_b = seq // tm
    q0 = 2
    k0 = q0 + mw // tn
    v0 = k0 + mw // tn
    cw = _tile(tn, INPROJ_CHUNK)
    assert cw % dh == 0
    kern = functools.partial(_inproj_kernel, q0=q0, k0=k0, v0=v0, dh=dh, scale=dh ** -0.5 * LOG2_E, cw=cw)
    return pl.pallas_call(
        kern,
        grid=(t // tm, n // tn),
        in_specs=[pl.BlockSpec((tm, d), lambda i, j: (i, 0)),
                  pl.BlockSpec((1, d), lambda i, j: (0, 0)),
                  pl.BlockSpec((1, 1, d), lambda i, j: ((i // per_b) * 6 + 1, 0, 0)),
                  pl.BlockSpec((1, 1, d), lambda i, j: ((i // per_b) * 6 + 0, 0, 0)),
                  pl.BlockSpec((d, tn), lambda i, j: (0, j)),
                  pl.BlockSpec((tm, dh), lambda i, j: (i, 0)),
                  pl.BlockSpec((tm, dh), lambda i, j: (i, 0)),
                  pl.BlockSpec((1, sw), lambda i, j: (0, 0)),
                  pl.BlockSpec((1, sw), lambda i, j: (0, 0))],
        out_specs=pl.BlockSpec((tm, tn), lambda i, j: (i, j)),
        out_shape=jax.ShapeDtypeStruct((t, n), _BF16),
        scratch_shapes=[pltpu.VMEM((tm, d), _BF16), pltpu.VMEM((tm, tn), _F32)],
        compiler_params=_params(("arbitrary", "arbitrary")),
        name="inproj",
    )(x2, norm_g.reshape(1, d), ada3, ada3, w_in, cos, sin, ln_g.reshape(1, sw), ln_b.reshape(1, sw))


def _sgu_kernel(u_ref, v_ref, ga_ref, ws_ref, bs_ref, wo_ref, o_ref, gated_scr, *, groups, cg):
    c = SGU_CHUNK
    tri = lax.broadcasted_iota(jnp.int32, (c, c), 0) >= lax.broadcasted_iota(jnp.int32, (c, c), 1)
    for g in range(groups):
        wg = jnp.where(tri, ws_ref[g], 0.0).astype(_BF16)
        cols = slice(g * cg, (g + 1) * cg)
        for ci in range(u_ref.shape[0] // c):
            rows = slice(ci * c, (ci + 1) * c)
            sv = jnp.dot(wg, v_ref[rows, cols], preferred_element_type=_F32) + bs_ref[:, cols]
            gated_scr[rows, cols] = (u_ref[rows, cols].astype(_F32) * sv).astype(_BF16)
    ya = jnp.dot(gated_scr[...], wo_ref[...], preferred_element_type=_F32)
    o_ref[...] = (jax.nn.sigmoid(ga_ref[...].astype(_F32)) * ya).astype(_BF16)


def _sgu(proj, w_s, bs_wide, w_sgu_out, *, sw, mw, tm):
    t = proj.shape[0]
    d = w_sgu_out.shape[1]
    groups = w_s.shape[0]
    ga_blk = (2 * sw + 3 * mw) // d
    kern = functools.partial(_sgu_kernel, groups=groups, cg=sw // groups)
    return pl.pallas_call(
        kern,
        grid=(t // tm,),
        in_specs=[pl.BlockSpec((tm, sw), lambda i: (i, 0)),
                  pl.BlockSpec((tm, sw), lambda i: (i, 1)),
                  pl.BlockSpec((tm, d), lambda i: (i, ga_blk)),
                  pl.BlockSpec(w_s.shape, lambda i: (0, 0, 0)),
                  pl.BlockSpec(bs_wide.shape, lambda i: (0, 0)),
                  pl.BlockSpec(w_sgu_out.shape, lambda i: (0, 0))],
        out_specs=pl.BlockSpec((tm, d), lambda i: (i, 0)),
        out_shape=jax.ShapeDtypeStruct((t, d), _BF16),
        scratch_shapes=[pltpu.VMEM((tm, sw), _BF16)],
        compiler_params=_params(("arbitrary",)),
        name="sgu",
    )(proj, proj, proj, w_s, bs_wide, w_sgu_out)


def _moba_kernel(q_ref, k_ref, v_ref, o_ref, kmh_scr, kml_scr, vt_scr, sel_scr,
                 s_scr, p_scr, acc_scr, m_scr, l_scr, a_scr, *, nb, topk, hg, dh):
    qi = pl.program_id(2)
    blk = MOBA_BLOCK
    tq = q_ref.shape[0]

    @pl.when(qi == 0)
    def _():
        for h in range(hg):
            cols = slice(h * dh, (h + 1) * dh)
            kf = k_ref[:, cols].astype(_F32).reshape(nb, blk, dh)
            km_hi, km_lo = _split_bf16(jnp.mean(kf, axis=1))
            kmh_scr[h * nb:(h + 1) * nb, :] = km_hi
            kml_scr[h * nb:(h + 1) * nb, :] = km_lo
            for c in range(nb):
                rows = slice(c * blk, (c + 1) * blk)
                vt_scr[cols, rows] = v_ref[rows, cols].astype(_F32).T.astype(_BF16)

    row = lax.broadcasted_iota(jnp.int32, (nb, tq), 0)
    past = row < qi
    kpos = lax.broadcasted_iota(jnp.int32, (blk, tq), 0)
    qpos = lax.broadcasted_iota(jnp.int32, (blk, tq), 1)
    own = pl.multiple_of(qi * blk, blk)

    def scores(start, h):
        cols = slice(h * dh, (h + 1) * dh)
        return lax.dot_general(k_ref[pl.ds(start, blk), cols], q_ref[:, cols], _NT,
                               preferred_element_type=_F32)

    def weighted_values(start, h, p):
        return jnp.dot(vt_scr[h * dh:(h + 1) * dh, pl.ds(start, blk)], p, preferred_element_type=_F32)

    for h in range(hg):
        hrows = slice(h * nb, (h + 1) * nb)
        q = q_ref[:, h * dh:(h + 1) * dh]
        gate = (lax.dot_general(kmh_scr[hrows, :], q, _NT, preferred_element_type=_F32)
                + lax.dot_general(kml_scr[hrows, :], q, _NT, preferred_element_type=_F32))
        gm = jnp.where(past, gate, NEG_INF)
        beaten = jnp.zeros((nb, tq), jnp.int32)
        for jp in range(nb):
            other = gm[jp:jp + 1, :]
            wins = (other > gm) | ((other == gm) & (jp < row))
            beaten += wins.astype(jnp.int32)
        sel_scr[hrows, :] = (past & (beaten < topk)).astype(_F32)

        s = jnp.where(kpos <= qpos, scores(own, h), NEG_INF)
        m0 = jnp.max(s, axis=0, keepdims=True)
        p = jnp.exp2(s - m0)
        m_scr[h] = m0
        l_scr[h] = jnp.sum(p, axis=0, keepdims=True)
        a_scr[h] = jnp.ones_like(m0)
        acc_scr[h] = jnp.zeros(acc_scr.shape[1:], _F32)
        p_scr[0, h] = p.astype(_BF16)
        s_scr[0, h] = scores(0, h)

    def body(j, c):
        par = lax.rem(j, 2)
        prev = pl.multiple_of(jnp.where(j == 0, qi, j - 1) * blk, blk)
        nxt = pl.multiple_of(jnp.minimum(j + 1, qi - 1) * blk, blk)
        for h in range(hg):
            acc_scr[h] = a_scr[h] * acc_scr[h] + weighted_values(prev, h, p_scr[par, h])
        for h in range(hg):
            s = jnp.where(sel_scr[pl.ds(h * nb + j, 1), :] > 0.0, s_scr[par, h], NEG_INF)
            m = m_scr[h]
            m_new = jnp.maximum(m, jnp.max(s, axis=0, keepdims=True))
            alpha = jnp.exp2(m - m_new)
            p = jnp.exp2(s - m_new)
            l_scr[h] = alpha * l_scr[h] + jnp.sum(p, axis=0, keepdims=True)
            m_scr[h] = m_new
            a_scr[h] = alpha
            p_scr[1 - par, h] = p.astype(_BF16)
        for h in range(hg):
            s_scr[1 - par, h] = scores(nxt, h)
        return c

    lax.fori_loop(0, qi, body, 0)
    last = pl.multiple_of(jnp.where(qi == 0, qi, qi - 1) * blk, blk)
    for h in range(hg):
        acc = a_scr[h] * acc_scr[h] + weighted_values(last, h, p_scr[lax.rem(qi, 2), h])
        o_ref[:, h * dh:(h + 1) * dh] = (acc / l_scr[h]).T.astype(_BF16)


def _moba(proj, *, batch, seq, sw, mw, dh, hg):
    t = proj.shape[0]
    heads = mw // dh
    nb = seq // MOBA_BLOCK
    tq = MOBA_BLOCK
    nq = seq // tq
    gw = hg * dh
    qc = 2 * sw // gw
    kc = qc + heads // hg
    vc = kc + heads // hg
    kern = functools.partial(_moba_kernel, nb=nb, topk=MOBA_TOPK, hg=hg, dh=dh)
    return pl.pallas_call(
        kern,
        grid=(batch, heads // hg, nq),
        in_specs=[pl.BlockSpec((tq, gw), lambda b, h, i: (b * nq + i, qc + h)),
                  pl.BlockSpec((seq, gw), lambda b, h, i: (b, kc + h)),
                  pl.BlockSpec((seq, gw), lambda b, h, i: (b, vc + h))],
        out_specs=pl.BlockSpec((tq, gw), lambda b, h, i: (b * nq + i, h)),
        out_shape=jax.ShapeDtypeStruct((t, mw), _BF16),
        scratch_shapes=[pltpu.VMEM((hg * nb, dh), _BF16), pltpu.VMEM((hg * nb, dh), _BF16),
                        pltpu.VMEM((gw, seq), _BF16), pltpu.VMEM((hg * nb, tq), _F32),
                        pltpu.VMEM((2, hg, MOBA_BLOCK, tq), _F32), pltpu.VMEM((2, hg, MOBA_BLOCK, tq), _BF16),
                        pltpu.VMEM((hg, dh, tq), _F32), pltpu.VMEM((hg, 1, tq), _F32),
                        pltpu.VMEM((hg, 1, tq), _F32), pltpu.VMEM((hg, 1, tq), _F32)],
        compiler_params=_params(("arbitrary", "arbitrary", "arbitrary")),
        name="moba",
    )(proj, proj, proj)


def _mixout_kernel(attn_ref, ma_ref, gb_ref, x_ref, gm_ref, shf_ref, scf_ref, ng_ref,
                   wmo_ref, wo_ref, wrh_ref, wrl_ref, br_ref,
                   h_ref, n2_ref, eid_ref, wt_ref, *, groups, epg):
    yb = jnp.dot(attn_ref[...], wmo_ref[...], preferred_element_type=_F32)
    merged = ma_ref[...].astype(_F32) + jax.nn.sigmoid(gb_ref[...].astype(_F32)) * yb
    mix = jnp.dot(merged.astype(_BF16), wo_ref[...], preferred_element_type=_F32)
    h = x_ref[...] + gm_ref[0] * mix
    h_ref[...] = h
    n2 = _rms_modulate(h, ng_ref[...], scf_ref[0], shf_ref[0])
    _store_token_major(n2_ref, n2)

    n_hi, n_lo = _split_bf16(n2)
    lg = (lax.dot_general(wrh_ref[...], n_hi, _NT, preferred_element_type=_F32)
          + lax.dot_general(wrh_ref[...], n_lo, _NT, preferred_element_type=_F32)
          + lax.dot_general(wrl_ref[...], n_hi, _NT, preferred_element_type=_F32)) + br_ref[:, 0:1]

    gl = [lg[g:g + 1, :] for g in range(groups)]
    gmax = functools.reduce(jnp.maximum, gl)
    denom = functools.reduce(jnp.add, [jnp.exp(v - gmax) for v in gl])
    p_group = 1.0 / denom
    g_sel = jnp.full(gmax.shape, groups - 1, jnp.int32)
    for g in range(groups - 2, -1, -1):
        g_sel = jnp.where(gl[g] == gmax, g, g_sel)

    el = []
    for e in range(epg):
        v = lg[groups + e:groups + e + 1, :]
        for g in range(1, groups):
            r = groups + g * epg + e
            v = jnp.where(g_sel == g, lg[r:r + 1, :], v)
        el.append(v)

    def top1(vals):
        vmax = functools.reduce(jnp.maximum, vals)
        idx = jnp.full(vmax.shape, epg - 1, jnp.int32)
        for e in range(epg - 2, -1, -1):
            idx = jnp.where(vals[e] == vmax, e, idx)
        return vmax, idx

    v1, i1 = top1(el)
    v2, i2 = top1([jnp.where(i1 == e, -jnp.inf, el[e]) for e in range(epg)])
    b = jnp.exp(v2 - v1)
    eid_ref[0:1, :] = g_sel * epg + i1
    eid_ref[1:2, :] = g_sel * epg + i2
    wt_ref[0:1, :] = (1.0 / (1.0 + b)) * p_group
    wt_ref[1:2, :] = (b / (1.0 + b)) * p_group


def _mixout(attn, m_a, proj, x2, ada3, norm_g, w_moba_out, w_out, wr_hi, wr_lo, br,
            *, seq, sw, mw, groups, epg, tm):
    t, d = x2.shape
    per_b = seq // tm
    gb_blk = (2 * sw + 3 * mw) // d + 1
    once = pl.Buffered(1)
    kern = functools.partial(_mixout_kernel, groups=groups, epg=epg)

    def ada_spec(k):
        return pl.BlockSpec((1, 1, d), lambda i: ((i // per_b) * 6 + k, 0, 0))

    return pl.pallas_call(
        kern,
        grid=(t // tm,),
        in_specs=[pl.BlockSpec((tm, mw), lambda i: (i, 0)),
                  pl.BlockSpec((tm, d), lambda i: (i, 0)),
                  pl.BlockSpec((tm, d), lambda i: (i, gb_blk)),
                  pl.BlockSpec((tm, d), lambda i: (i, 0)),
                  ada_spec(2), ada_spec(3), ada_spec(4),
                  pl.BlockSpec((1, d), lambda i: (0, 0)),
                  pl.BlockSpec((mw, d), lambda i: (0, 0), pipeline_mode=once),
                  pl.BlockSpec((d, d), lambda i: (0, 0), pipeline_mode=once),
                  pl.BlockSpec((ROUTER_ROWS, d), lambda i: (0, 0)),
                  pl.BlockSpec((ROUTER_ROWS, d), lambda i: (0, 0)),
                  pl.BlockSpec((ROUTER_ROWS, LANES), lambda i: (0, 0))],
        out_specs=[pl.BlockSpec((tm, d), lambda i: (i, 0)),
                   pl.BlockSpec((tm * _pitch(d), LANES), lambda i: (i, 0)),
                   pl.BlockSpec((MOE_TOPK, tm), lambda i: (0, i)),
                   pl.BlockSpec((MOE_TOPK, tm), lambda i: (0, i))],
        out_shape=[jax.ShapeDtypeStruct((t, d), _F32),
                   jax.ShapeDtypeStruct((t * _pitch(d), LANES), _F32),
                   jax.ShapeDtypeStruct((MOE_TOPK, t), jnp.int32),
                   jax.ShapeDtypeStruct((MOE_TOPK, t), _F32)],
        compiler_params=_params(("arbitrary",)),
        name="mixout",
    )(attn, m_a, proj, x2, ada3, ada3, ada3, norm_g.reshape(1, d), w_moba_out, w_out, wr_hi, wr_lo, br)


def _plan_kernel(eid_ref, dest_ref, tab_ref, carry_scr, start_scr, tri_scr, *, n_exp, rb):
    phase = pl.program_id(0)
    first = pl.program_id(1) == 0
    tc = eid_ref.shape[1]
    nbp = tab_ref.shape[1]

    @pl.when((phase == 0) & first)
    def _():
        carry_scr[...] = jnp.zeros_like(carry_scr)
        earlier = lax.broadcasted_iota(jnp.int32, (tc, tc), 0) < lax.broadcasted_iota(jnp.int32, (tc, tc), 1)
        tri_scr[...] = earlier.astype(_BF16)

    eio = lax.broadcasted_iota(jnp.int32, (n_exp, tc), 0)
    oh0 = (eio == eid_ref[0:1, :]).astype(_F32)
    oh1 = (eio == eid_ref[1:2, :]).astype(_F32)
    oh = oh0 + oh1

    @pl.when((phase == 1) & first)
    def _():
        counts = carry_scr[...].astype(jnp.int32)
        blocks = lax.shift_right_logical(counts + (rb - 1), rb.bit_length() - 1).astype(_F32)
        upto = lax.broadcasted_iota(jnp.int32, (n_exp, n_exp), 0) >= lax.broadcasted_iota(jnp.int32, (n_exp, n_exp), 1)
        end_blk = jnp.dot(upto.astype(_BF16), blocks.astype(_BF16), preferred_element_type=_F32)
        start_scr[...] = (end_blk - blocks) * rb
        carry_scr[...] = jnp.zeros_like(carry_scr)
        blk_id = lax.broadcasted_iota(jnp.int32, (n_exp, nbp), 1).astype(_F32)
        owner = jnp.sum((end_blk[:, 0:1] <= blk_id).astype(_F32), axis=0, keepdims=True)
        owner = jnp.minimum(owner, n_exp - 1.0)
        used = jnp.broadcast_to(end_blk[n_exp - 1:n_exp, 0:1], (1, nbp))
        trow = lax.broadcasted_iota(jnp.int32, tab_ref.shape, 0)
        tab_ref[...] = jnp.where(trow == 0, owner, jnp.where(trow == 1, used, 0.0)).astype(jnp.int32)

    @pl.when(phase == 1)
    def _():
        before = jnp.dot(oh.astype(_BF16), tri_scr[...], preferred_element_type=_F32)
        base = start_scr[:, 0:1] + carry_scr[:, 0:1] + before
        dest_ref[0:1, :] = jnp.sum(oh0 * base, axis=0, keepdims=True).astype(jnp.int32)
        dest_ref[1:2, :] = jnp.sum(oh1 * base, axis=0, keepdims=True).astype(jnp.int32)

    carry_scr[...] = carry_scr[...] + jnp.sum(oh, axis=1, keepdims=True)


def _plan(eid, *, n_exp, tc, rb, n_blocks):
    t = eid.shape[1]
    assert rb & (rb - 1) == 0 and t * MOE_TOPK // rb + n_exp < 256
    nbp = -(-n_blocks // LANES) * LANES
    kern = functools.partial(_plan_kernel, n_exp=n_exp, rb=rb)
    return pl.pallas_call(
        kern,
        grid=(2, t // tc),
        in_specs=[pl.BlockSpec((MOE_TOPK, tc), lambda p, i: (0, i))],
        out_specs=[pl.BlockSpec((MOE_TOPK, tc), lambda p, i: (0, i * p)),
                   pl.BlockSpec((8, nbp), lambda p, i: (0, 0))],
        out_shape=[jax.ShapeDtypeStruct((MOE_TOPK, t), jnp.int32),
                   jax.ShapeDtypeStruct((8, nbp), jnp.int32)],
        scratch_shapes=[pltpu.VMEM((n_exp, LANES), _F32), pltpu.VMEM((n_exp, LANES), _F32),
                        pltpu.VMEM((tc, tc), _BF16)],
        compiler_params=_params(("arbitrary", "arbitrary")),
        name="plan",
    )(eid)


def _invert_kernel(dest_ref, inv_ref, *, rb, n_assign, n_blocks):
    i = pl.program_id(0)
    tch = dest_ref.shape[1]
    shift = rb.bit_length() - 1

    @pl.when(i == 0)
    def _():
        def init(r, c):
            blk = lax.shift_right_logical(r, shift)
            slot = jnp.where(blk == n_blocks, 3, lax.rem(blk, 3))
            inv_ref[r] = n_assign + slot * rb + (r & (rb - 1))
            return c

        lax.fori_loop(0, inv_ref.shape[0], init, 0, unroll=8)

    def body(t, c):
        a = (i * tch + t) * MOE_TOPK
        for k in range(MOE_TOPK):
            inv_ref[dest_ref[k, t]] = a + k
        return c

    lax.fori_loop(0, tch, body, 0, unroll=8)


def _invert(dest3, cap, rb, n_assign):
    steps, _, tch = dest3.shape
    kern = functools.partial(_invert_kernel, rb=rb, n_assign=n_assign, n_blocks=cap // rb)
    return pl.pallas_call(
        kern,
        grid=(steps,),
        in_specs=[pl.BlockSpec((None, MOE_TOPK, tch), lambda i: (i, 0, 0), memory_space=pltpu.SMEM)],
        out_specs=pl.BlockSpec((cap + rb,), lambda i: (0,), memory_space=pltpu.SMEM),
        out_shape=jax.ShapeDtypeStruct((cap + rb,), jnp.int32),
        compiler_params=_params(("arbitrary",)),
        name="invert",
    )(dest3)


N_TRASH_SLOTS = 4


def _expert_kernel(be_ref, nu_ref, invp_ref, invc_ref, invn_ref, n2_ref, wg_ref, wu_ref, wd_ref, y_ref,
                   wg_s, wu_s, wd_s, xbuf0, xbuf1, ybuf0, ybuf1, gsem, ssem, *, rb, n_tok, d):
    i = pl.program_id(0)
    last = nu_ref[0] - 1
    used = i <= last
    fresh = (i == 0) | (be_ref[i] != be_ref[jnp.maximum(i - 1, 0)])
    ch = d // LANES
    pitch = _pitch(d)
    n_assign = n_tok * MOE_TOPK
    xbuf = (xbuf0, xbuf1)
    ybuf = (ybuf0, ybuf1)

    def aligned(row):
        return pl.multiple_of(row, 8) if pitch % 8 == 0 else row

    def gather(inv_ref, p):
        for r in range(rb):
            tok = jnp.minimum(lax.shift_right_logical(inv_ref[0, r], 1), n_tok - 1)
            pltpu.make_async_copy(n2_ref.at[pl.ds(aligned(tok * pitch), ch)],
                                  xbuf[p].at[pl.ds(r * pitch, ch)], gsem.at[p]).start()

    def gather_wait(p):
        pltpu.make_async_copy(n2_ref.at[pl.ds(0, rb * ch)], xbuf[p].at[pl.ds(0, rb * ch)], gsem.at[p]).wait()

    def scatter(inv_ref, p):
        for r in range(rb):
            pltpu.make_async_copy(ybuf[p].at[pl.ds(r * pitch, pitch)],
                                  y_ref.at[pl.ds(aligned(inv_ref[0, r] * pitch), pitch)], ssem.at[p]).start()

    def scatter_wait(p):
        pltpu.make_async_copy(ybuf[p], y_ref.at[pl.ds(0, rb * pitch)], ssem.at[p]).wait()

    @pl.when(i == 0)
    def _():
        gather(invc_ref, 0)
        for buf in ybuf:
            buf[...] = jnp.zeros(buf.shape, _F32)
        fills = [pltpu.make_async_copy(ybuf[1], y_ref.at[pl.ds((n_assign + k * rb) * pitch, rb * pitch)],
                                       ssem.at[1]) for k in range(N_TRASH_SLOTS)]
        for fill in fills:
            fill.start()
        for fill in fills:
            fill.wait()

    @pl.when(used & fresh)
    def _():
        wg_s[...] = wg_ref[0].astype(_BF16)
        wu_s[...] = wu_ref[0].astype(_BF16)
        wd_s[...] = wd_ref[0].astype(_BF16)

    def step(p):
        @pl.when(i >= 1)
        def _():
            scatter_wait(p)

        gather_wait(p)
        gather(invn_ref, 1 - p)
        scatter(invp_ref, 1 - p)
        xb = _load_token_major(xbuf[p], rb, d).astype(_BF16)
        gate = jnp.dot(xb, wg_s[...], preferred_element_type=_F32)
        up = jnp.dot(xb, wu_s[...], preferred_element_type=_F32)
        hid = (jax.nn.silu(gate) * up).astype(_BF16)
        _store_token_major(ybuf[p], jnp.dot(hid, wd_s[...], preferred_element_type=_F32), zero_pad=False)

        @pl.when(i == last)
        def _():
            scatter(invc_ref, p)
            scatter_wait(p)
            scatter_wait(1 - p)
            gather_wait(1 - p)

    for p in range(2):
        pl.when(used & (lax.rem(i, 2) == p))(functools.partial(step, p))


def _experts(block_expert, n_used, inv3, n2, w_gate, w_up, w_down, *, n_tok):
    nblk, _, rb = inv3.shape
    nblk -= 1
    d, f = w_gate.shape[1:]
    pitch = _pitch(d)

    def inv_spec(shift):
        def index(i, be, nu):
            blk = jnp.clip(i + shift, 0, nu[0] - 1)
            return (jnp.where(i + shift < 0, nblk, blk), 0, 0)
        return pl.BlockSpec((None, 1, rb), index, memory_space=pltpu.SMEM)

    grid_spec = pltpu.PrefetchScalarGridSpec(
        num_scalar_prefetch=2,
        grid=(nblk,),
        in_specs=[inv_spec(-1), inv_spec(0), inv_spec(1),
                  pl.BlockSpec(memory_space=pl.ANY),
                  pl.BlockSpec((1, d, f), lambda i, be, nu: (be[i], 0, 0)),
                  pl.BlockSpec((1, d, f), lambda i, be, nu: (be[i], 0, 0)),
                  pl.BlockSpec((1, f, d), lambda i, be, nu: (be[i], 0, 0))],
        out_specs=pl.BlockSpec(memory_space=pl.ANY),
        scratch_shapes=[pltpu.VMEM((d, f), _BF16), pltpu.VMEM((d, f), _BF16), pltpu.VMEM((f, d), _BF16),
                        *[pltpu.VMEM((rb * pitch, LANES), _F32) for _ in range(4)],
                        pltpu.SemaphoreType.DMA((2,)), pltpu.SemaphoreType.DMA((2,))],
    )
    kern = functools.partial(_expert_kernel, rb=rb, n_tok=n_tok, d=d)
    n_chunks = n_tok * MOE_TOPK + N_TRASH_SLOTS * rb
    return pl.pallas_call(
        kern,
        grid_spec=grid_spec,
        out_shape=jax.ShapeDtypeStruct((n_chunks * pitch, LANES), _F32),
        compiler_params=_params(("arbitrary",)),
        name="experts",
    )(block_expert, n_used, inv3, inv3, inv3, n2, w_gate, w_up, w_down)


def _combine_kernel(h_ref, wc_ref, gf_ref, ng_ref, y_ref, o_ref):
    tk, d = h_ref.shape
    pitch = _pitch(d)
    y = [_load_token_major(y_ref, tk, d, first=k * pitch, pitch=MOE_TOPK * pitch) for k in range(MOE_TOPK)]
    moe = wc_ref[:, 0:1] * y[0] + wc_ref[:, 1:2] * y[1]
    h = h_ref[...] + gf_ref[0] * moe
    o_ref[...] = h * lax.rsqrt(jnp.mean(h * h, axis=-1, keepdims=True) + EPS) * ng_ref[...]


def _combine(h1, w_cols, ada3, norm_g, y_assign, *, seq, tk):
    t, d = h1.shape
    per_b = seq // tk
    assert y_assign.shape[0] % (tk * MOE_TOPK * _pitch(d)) == 0
    return pl.pallas_call(
        _combine_kernel,
        grid=(t // tk,),
        in_specs=[pl.BlockSpec((tk, d), lambda i: (i, 0)),
                  pl.BlockSpec((tk, LANES), lambda i: (i, 0)),
                  pl.BlockSpec((1, 1, d), lambda i: ((i // per_b) * 6 + 5, 0, 0)),
                  pl.BlockSpec((1, d), lambda i: (0, 0)),
                  pl.BlockSpec((tk * MOE_TOPK * _pitch(d), LANES), lambda i: (i, 0))],
        out_specs=pl.BlockSpec((tk, d), lambda i: (i, 0)),
        out_shape=jax.ShapeDtypeStruct((t, d), _F32),
        compiler_params=_params(("arbitrary",)),
        name="combine",
    )(h1, w_cols, ada3, norm_g.reshape(1, d), y_assign)


def _rotary_tables(positions, dh):
    inv_freq = ROPE_THETA ** (-jnp.arange(0, dh, 2, dtype=_F32) / dh)
    ang = positions.astype(_F32)[..., None] * inv_freq
    cos, sin = jnp.cos(ang), jnp.sin(ang)
    t = cos.shape[0] * cos.shape[1]
    cos = jnp.concatenate([cos, cos], axis=-1).reshape(t, dh)
    sin = jnp.concatenate([-sin, sin], axis=-1).reshape(t, dh)
    return cos, sin


def _tile(n, want):
    while n % want:
        want //= 2
    return want


def kernel(x, c, positions, w_ada, b_ada, norm_mix_g, w_in, sgu_ln_g, sgu_ln_b, sgu_w_s, sgu_b_s,
           w_sgu_out, w_moba_out, w_out, norm_ffn_g, w_route_group, b_route_group, w_route_expert,
           b_route_expert, w_exp_gate, w_exp_up, w_exp_down, norm_final_g):
    batch, seq, d = x.shape
    depth = w_ada.shape[0]
    t = batch * seq
    sw = sgu_ln_g.shape[1]
    mw = w_moba_out.shape[1]
    dh = mw // MOBA_HEADS
    groups = w_route_group.shape[2]
    n_exp = w_route_expert.shape[2]
    epg = n_exp // groups
    rb = EXPERT_ROW_BLOCK
    cap = (t * MOE_TOPK // rb + n_exp) * rb
    assert depth == 1, "the final RMSNorm is fused into the single layer's combine"
    assert dh == LANES and seq % MOBA_BLOCK == 0 and sw * 2 == d and mw % sw == 0
    assert groups + n_exp <= ROUTER_ROWS and (t * MOE_TOPK) % rb == 0

    cos, sin = _rotary_tables(positions, dh)
    c_pad = jnp.zeros((8, d), _F32).at[:batch].set(c)
    h = x.reshape(t, d)

    for l in range(depth):
        ada = _ada(c_pad, w_ada[l], b_ada[l], _tile(6 * d, 1024))
        ada3 = ada[:batch].reshape(batch * 6, 1, d)

        proj = _inproj(h, ada3, norm_mix_g[l], w_in[l].astype(_BF16), cos, sin, sgu_ln_g[l], sgu_ln_b[l],
                       seq=seq, sw=sw, mw=mw, dh=dh, tm=_tile(seq, 1024))
        bs_wide = jnp.repeat(sgu_b_s[l].T, sw // sgu_w_s.shape[1], axis=1)
        m_a = _sgu(proj, sgu_w_s[l], bs_wide, w_sgu_out[l].astype(_BF16), sw=sw, mw=mw, tm=_tile(seq, 512))
        attn = _moba(proj, batch=batch, seq=seq, sw=sw, mw=mw, dh=dh, hg=min(MOBA_HEAD_GROUP, MOBA_HEADS))

        wr = jnp.concatenate([w_route_group[l], w_route_expert[l]], axis=1).T
        wr = jnp.zeros((ROUTER_ROWS, d), _F32).at[:groups + n_exp].set(wr)
        wr_hi, wr_lo = _split_bf16(wr)
        br = jnp.concatenate([b_route_group[l], b_route_expert[l].reshape(-1)])
        br = jnp.broadcast_to(jnp.zeros((ROUTER_ROWS,), _F32).at[:groups + n_exp].set(br)[:, None],
                              (ROUTER_ROWS, LANES))
        h1, n2, eid, wt = _mixout(attn, m_a, proj, h, ada3, norm_ffn_g[l], w_moba_out[l].astype(_BF16),
                                  w_out[l].astype(_BF16), wr_hi, wr_lo, br,
                                  seq=seq, sw=sw, mw=mw, groups=groups, epg=epg, tm=_tile(seq, 256))

        dest, tab = _plan(eid, n_exp=n_exp, tc=_tile(t, 512), rb=rb, n_blocks=cap // rb)
        tch = _tile(t, 2048)
        inv = _invert(dest.reshape(MOE_TOPK, t // tch, tch).transpose(1, 0, 2), cap, rb, t * MOE_TOPK)
        y_assign = _experts(tab[0, :cap // rb], tab[1, :1], inv.reshape(cap // rb + 1, 1, rb), n2,
                            w_exp_gate[l], w_exp_up[l], w_exp_down[l], n_tok=t)
        w_cols = jnp.zeros((t, LANES), _F32).at[:, :MOE_TOPK].set(wt.T)
        h = _combine(h1, w_cols, ada3, norm_final_g, y_assign, seq=seq, tk=_tile(seq, 256))

    return h.reshape(batch, seq, d)
```

```python
import functools

import jax
import jax.numpy as jnp
import numpy as np
from jax import lax
from jax.experimental import pallas as pl
from jax.experimental.pallas import tpu as pltpu

MOBA_HEADS = 16
MOBA_BLOCK = 256
MOBA_TOPK = 3
SGU_CHUNK = 128
ROPE_THETA = 10000.0
MOE_GROUPS = 4
MOE_TOPK = 2
EPS = 1e-6
NEG_INF = -1e30
LOG2_E = 1.4426950408889634

LANES = 128
TOKEN_PAD_ROWS = 8
EXPERT_ROW_BLOCK = 256
INPROJ_CHUNK = 256
MOBA_HEAD_GROUP = 4
ROUTER_ROWS = 40
VMEM_LIMIT = 56 * 1024 * 1024

_F32 = jnp.float32
_BF16 = jnp.bfloat16
_NT = (((1,), (1,)), ((), ()))


def _params(semantics, vmem=VMEM_LIMIT):
    return pltpu.CompilerParams(dimension_semantics=semantics, vmem_limit_bytes=vmem)


def _pitch(d):
    return d // LANES + TOKEN_PAD_ROWS


def _store_token_major(ref, val, zero_pad=True):
    n, d = val.shape
    pitch = ref.shape[0] // n
    for c in range(pitch if zero_pad else d // LANES):
        piece = val[:, c * LANES:(c + 1) * LANES] if c < d // LANES else jnp.zeros((n, LANES), val.dtype)
        ref[pl.ds(c, n, stride=pitch), :] = piece


def _load_token_major(ref, n, d, first=0, pitch=None):
    pitch = pitch or ref.shape[0] // n
    return jnp.concatenate([ref[pl.ds(first + c, n, stride=pitch), :] for c in range(d // LANES)], axis=1)


def _split_bf16(a):
    hi = a.astype(_BF16)
    lo = (a - hi.astype(_F32)).astype(_BF16)
    return hi, lo


def _ada_kernel(c_ref, w_ref, b_ref, o_ref):
    ca_hi, ca_lo = _split_bf16(jax.nn.silu(c_ref[...]))
    w_hi, w_lo = _split_bf16(w_ref[...])
    acc = jnp.dot(ca_hi, w_hi, preferred_element_type=_F32)
    acc += jnp.dot(ca_hi, w_lo, preferred_element_type=_F32)
    acc += jnp.dot(ca_lo, w_hi, preferred_element_type=_F32)
    o_ref[...] = acc + b_ref[...]


def _ada(c_pad, w_ada, b_ada, tn):
    rows, d = c_pad.shape
    n = w_ada.shape[1]
    return pl.pallas_call(
        _ada_kernel,
        grid=(n // tn,),
        in_specs=[pl.BlockSpec((rows, d), lambda j: (0, 0)),
                  pl.BlockSpec((d, tn), lambda j: (0, j)),
                  pl.BlockSpec((1, tn), lambda j: (0, j))],
        out_specs=pl.BlockSpec((rows, tn), lambda j: (0, j)),
        out_shape=jax.ShapeDtypeStruct((rows, n), _F32),
        compiler_params=_params(("arbitrary",)),
        name="ada",
    )(c_pad, w_ada, b_ada.reshape(1, n))


def _rms_modulate(x, g, sc, sh):
    y = x * lax.rsqrt(jnp.mean(x * x, axis=-1, keepdims=True) + EPS) * g
    return y * (1.0 + sc) + sh


def _inproj_kernel(x_ref, g_ref, sc_ref, sh_ref, w_ref, cos_ref, sin_ref, lng_ref, lnb_ref,
                   o_ref, n_scr, z_scr, *, q0, k0, v0, dh, scale, cw):
    j = pl.program_id(1)
    tn = o_ref.shape[1]

    @pl.when(j == 0)
    def _():
        n_scr[...] = _rms_modulate(x_ref[...], g_ref[...], sc_ref[0], sh_ref[0]).astype(_BF16)

    def chunk(c):
        return jnp.dot(n_scr[...], w_ref[:, c * cw:(c + 1) * cw], preferred_element_type=_F32)

    @pl.when(j < q0)
    def _():
        for c in range(tn // cw):
            z_scr[:, c * cw:(c + 1) * cw] = jax.nn.gelu(chunk(c))
        z = z_scr[...]
        is_v = j == 1
        mu = jnp.where(is_v, jnp.mean(z, axis=-1, keepdims=True), 0.0)
        zc = z - mu
        var = jnp.mean(zc * zc, axis=-1, keepdims=True)
        gain = jnp.where(is_v, lax.rsqrt(var + EPS) * lng_ref[...], 1.0)
        o_ref[...] = (zc * gain + jnp.where(is_v, lnb_ref[...], 0.0)).astype(_BF16)

    @pl.when(j >= q0)
    def _():
        mult = jnp.where(j < k0, scale, 1.0)
        cos = jnp.where(j < v0, cos_ref[...], 1.0) * mult
        sin = jnp.where(j < v0, sin_ref[...], 0.0) * mult
        for c in range(tn // cw):
            acc = chunk(c)
            for h in range(cw // dh):
                xh = acc[:, h * dh:(h + 1) * dh]
                rot = xh * cos + pltpu.roll(xh, dh // 2, 1) * sin
                o_ref[:, c * cw + h * dh:c * cw + (h + 1) * dh] = rot.astype(_BF16)


def _inproj(x2, ada3, norm_g, w_in, cos, sin, ln_g, ln_b, *, seq, sw, mw, dh, tm):
    t, d = x2.shape
    n = w_in.shape[1]
    tn = sw
    per_b = seq // tm
    q0 = 2
    k0 = q0 + mw // tn
    v0 = k0 + mw // tn
    cw = _tile(tn, INPROJ_CHUNK)
    assert cw % dh == 0
    kern = functools.partial(_inproj_kernel, q0=q0, k0=k0, v0=v0, dh=dh, scale=dh ** -0.5 * LOG2_E, cw=cw)
    return pl.pallas_call(
        kern,
        grid=(t // tm, n // tn),
        in_specs=[pl.BlockSpec((tm, d), lambda i, j: (i, 0)),
                  pl.BlockSpec((1, d), lambda i, j: (0, 0)),
                  pl.BlockSpec((1, 1, d), lambda i, j: ((i // per_b) * 6 + 1, 0, 0)),
                  pl.BlockSpec((1, 1, d), lambda i, j: ((i // per_b) * 6 + 0, 0, 0)),
                  pl.BlockSpec((d, tn), lambda i, j: (0, j)),
                  pl.BlockSpec((tm, dh), lambda i, j: (i, 0)),
                  pl.BlockSpec((tm, dh), lambda i, j: (i, 0)),
                  pl.BlockSpec((1, sw), lambda i, j: (0, 0)),
                  pl.BlockSpec((1, sw), lambda i, j: (0, 0))],
        out_specs=pl.BlockSpec((tm, tn), lambda i, j: (i, j)),
        out_shape=jax.ShapeDtypeStruct((t, n), _BF16),
        scratch_shapes=[pltpu.VMEM((tm, d), _BF16), pltpu.VMEM((tm, tn), _F32)],
        compiler_params=_params(("arbitrary", "arbitrary")),
        name="inproj",
    )(x2, norm_g.reshape(1, d), ada3, ada3, w_in, cos, sin, ln_g.reshape(1, sw), ln_b.reshape(1, sw))


def _sgu_kernel(u_ref, v_ref, ga_ref, ws_ref, bs_ref, wo_ref, o_ref, gated_scr, *, groups, cg):
    c = SGU_CHUNK
    tri = lax.broadcasted_iota(jnp.int32, (c, c), 0) >= lax.broadcasted_iota(jnp.int32, (c, c), 1)
    for g in range(groups):
        wg = jnp.where(tri, ws_ref[g], 0.0).astype(_BF16)
        cols = slice(g * cg, (g + 1) * cg)
        for ci in range(u_ref.shape[0] // c):
            rows = slice(ci * c, (ci + 1) * c)
            sv = jnp.dot(wg, v_ref[rows, cols], preferred_element_type=_F32) + bs_ref[:, cols]
            gated_scr[rows, cols] = (u_ref[rows, cols].astype(_F32) * sv).astype(_BF16)
    ya = jnp.dot(gated_scr[...], wo_ref[...], preferred_element_type=_F32)
    o_ref[...] = (jax.nn.sigmoid(ga_ref[...].astype(_F32)) * ya).astype(_BF16)


def _sgu(proj, w_s, bs_wide, w_sgu_out, *, sw, mw, tm):
    t = proj.shape[0]
    d = w_sgu_out.shape[1]
    groups = w_s.shape[0]
    ga_blk = (2 * sw + 3 * mw) // d
    kern = functools.partial(_sgu_kernel, groups=groups, cg=sw // groups)
    return pl.pallas_call(
        kern,
        grid=(t // tm,),
        in_specs=[pl.BlockSpec((tm, sw), lambda i: (i, 0)),
                  pl.BlockSpec((tm, sw), lambda i: (i, 1)),
                  pl.BlockSpec((tm, d), lambda i: (i, ga_blk)),
                  pl.BlockSpec(w_s.shape, lambda i: (0, 0, 0)),
                  pl.BlockSpec(bs_wide.shape, lambda i: (0, 0)),
                  pl.BlockSpec(w_sgu_out.shape, lambda i: (0, 0))],
        out_specs=pl.BlockSpec((tm, d), lambda i: (i, 0)),
        out_shape=jax.ShapeDtypeStruct((t, d), _BF16),
        scratch_shapes=[pltpu.VMEM((tm, sw), _BF16)],
        compiler_params=_params(("arbitrary",)),
        name="sgu",
    )(proj, proj, proj, w_s, bs_wide, w_sgu_out)


def _moba_kernel(q_ref, k_ref, v_ref, o_ref, kmh_scr, kml_scr, vt_scr, sel_scr,
                 s_scr, p_scr, acc_scr, m_scr, l_scr, a_scr, *, nb, topk, hg, dh):
    qi = pl.program_id(2)
    blk = MOBA_BLOCK
    tq = q_ref.shape[0]

    @pl.when(qi == 0)
    def _():
        for h in range(hg):
            cols = slice(h * dh, (h + 1) * dh)
            kf = k_ref[:, cols].astype(_F32).reshape(nb, blk, dh)
            km_hi, km_lo = _split_bf16(jnp.mean(kf, axis=1))
            kmh_scr[h * nb:(h + 1) * nb, :] = km_hi
            kml_scr[h * nb:(h + 1) * nb, :] = km_lo
            for c in range(nb):
                rows = slice(c * blk, (c + 1) * blk)
                vt_scr[cols, rows] = v_ref[rows, cols].astype(_F32).T.astype(_BF16)

    row = lax.broadcasted_iota(jnp.int32, (nb, tq), 0)
    past = row < qi
    kpos = lax.broadcasted_iota(jnp.int32, (blk, tq), 0)
    qpos = lax.broadcasted_iota(jnp.int32, (blk, tq), 1)
    own = pl.multiple_of(qi * blk, blk)

    def scores(start, h):
        cols = slice(h * dh, (h + 1) * dh)
        return lax.dot_general(k_ref[pl.ds(start, blk), cols], q_ref[:, cols], _NT,
                               preferred_element_type=_F32)

    def weighted_values(start, h, p):
        return jnp.dot(vt_scr[h * dh:(h + 1) * dh, pl.ds(start, blk)], p, preferred_element_type=_F32)

    for h in range(hg):
        hrows = slice(h * nb, (h + 1) * nb)
        q = q_ref[:, h * dh:(h + 1) * dh]
        gate = (lax.dot_general(kmh_scr[hrows, :], q, _NT, preferred_element_type=_F32)
                + lax.dot_general(kml_scr[hrows, :], q, _NT, preferred_element_type=_F32))
        gm = jnp.where(past, gate, NEG_INF)
        beaten = jnp.zeros((nb, tq), jnp.int32)
        for jp in range(nb):
            other = gm[jp:jp + 1, :]
            wins = (other > gm) | ((other == gm) & (jp < row))
            beaten += wins.astype(jnp.int32)
        sel_scr[hrows, :] = (past & (beaten < topk)).astype(_F32)

        s = jnp.where(kpos <= qpos, scores(own, h), NEG_INF)
        m0 = jnp.max(s, axis=0, keepdims=True)
        p = jnp.exp2(s - m0)
        m_scr[h] = m0
        l_scr[h] = jnp.sum(p, axis=0, keepdims=True)
        a_scr[h] = jnp.ones_like(m0)
        acc_scr[h] = jnp.zeros(acc_scr.shape[1:], _F32)
        p_scr[0, h] = p.astype(_BF16)
        s_scr[0, h] = scores(0, h)

    def body(j, c):
        par = lax.rem(j, 2)
        prev = pl.multiple_of(jnp.where(j == 0, qi, j - 1) * blk, blk)
        nxt = pl.multiple_of(jnp.minimum(j + 1, qi - 1) * blk, blk)
        for h in range(hg):
            acc_scr[h] = a_scr[h] * acc_scr[h] + weighted_values(prev, h, p_scr[par, h])
        for h in range(hg):
            s = jnp.where(sel_scr[pl.ds(h * nb + j, 1), :] > 0.0, s_scr[par, h], NEG_INF)
            m = m_scr[h]
            m_new = jnp.maximum(m, jnp.max(s, axis=0, keepdims=True))
            alpha = jnp.exp2(m - m_new)
            p = jnp.exp2(s - m_new)
            l_scr[h] = alpha * l_scr[h] + jnp.sum(p, axis=0, keepdims=True)
            m_scr[h] = m_new
            a_scr[h] = alpha
            p_scr[1 - par, h] = p.astype(_BF16)
        for h in range(hg):
            s_scr[1 - par, h] = scores(nxt, h)
        return c

    lax.fori_loop(0, qi, body, 0)
    last = pl.multiple_of(jnp.where(qi == 0, qi, qi - 1) * blk, blk)
    for h in range(hg):
        acc = a_scr[h] * acc_scr[h] + weighted_values(last, h, p_scr[lax.rem(qi, 2), h])
        o_ref[:, h * dh:(h + 1) * dh] = (acc / l_scr[h]).T.astype(_BF16)


def _moba(proj, *, batch, seq, sw, mw, dh, hg):
    t = proj.shape[0]
    heads = mw // dh
    nb = seq // MOBA_BLOCK
    tq = MOBA_BLOCK
    nq = seq // tq
    gw = hg * dh
    qc = 2 * sw // gw
    kc = qc + heads // hg
    vc = kc + heads // hg
    kern = functools.partial(_moba_kernel, nb=nb, topk=MOBA_TOPK, hg=hg, dh=dh)
    return pl.pallas_call(
        kern,
        grid=(batch, heads // hg, nq),
        in_specs=[pl.BlockSpec((tq, gw), lambda b, h, i: (b * nq + i, qc + h)),
                  pl.BlockSpec((seq, gw), lambda b, h, i: (b, kc + h)),
                  pl.BlockSpec((seq, gw), lambda b, h, i: (b, vc + h))],
        out_specs=pl.BlockSpec((tq, gw), lambda b, h, i: (b * nq + i, h)),
        out_shape=jax.ShapeDtypeStruct((t, mw), _BF16),
        scratch_shapes=[pltpu.VMEM((hg * nb, dh), _BF16), pltpu.VMEM((hg * nb, dh), _BF16),
                        pltpu.VMEM((gw, seq), _BF16), pltpu.VMEM((hg * nb, tq), _F32),
                        pltpu.VMEM((2, hg, MOBA_BLOCK, tq), _F32), pltpu.VMEM((2, hg, MOBA_BLOCK, tq), _BF16),
                        pltpu.VMEM((hg, dh, tq), _F32), pltpu.VMEM((hg, 1, tq), _F32),
                        pltpu.VMEM((hg, 1, tq), _F32), pltpu.VMEM((hg, 1, tq), _F32)],
        compiler_params=_params(("arbitrary", "arbitrary", "arbitrary")),
        name="moba",
    )(proj, proj, proj)


def _mixout_kernel(attn_ref, ma_ref, gb_ref, x_ref, gm_ref, shf_ref, scf_ref, ng_ref,
                   wmo_ref, wo_ref, wrh_ref, wrl_ref, br_ref,
                   h_ref, n2_ref, eid_ref, wt_ref, *, groups, epg):
    yb = jnp.dot(attn_ref[...], wmo_ref[...], preferred_element_type=_F32)
    merged = ma_ref[...].astype(_F32) + jax.nn.sigmoid(gb_ref[...].astype(_F32)) * yb
    mix = jnp.dot(merged.astype(_BF16), wo_ref[...], preferred_element_type=_F32)
    h = x_ref[...] + gm_ref[0] * mix
    h_ref[...] = h
    n2 = _rms_modulate(h, ng_ref[...], scf_ref[0], shf_ref[0])
    _store_token_major(n2_ref, n2)

    n_hi, n_lo = _split_bf16(n2)
    lg = (lax.dot_general(wrh_ref[...], n_hi, _NT, preferred_element_type=_F32)
          + lax.dot_general(wrh_ref[...], n_lo, _NT, preferred_element_type=_F32)
          + lax.dot_general(wrl_ref[...], n_hi, _NT, preferred_element_type=_F32)) + br_ref[:, 0:1]

    gl = [lg[g:g + 1, :] for g in range(groups)]
    gmax = functools.reduce(jnp.maximum, gl)
    denom = functools.reduce(jnp.add, [jnp.exp(v - gmax) for v in gl])
    p_group = 1.0 / denom
    g_sel = jnp.full(gmax.shape, groups - 1, jnp.int32)
    for g in range(groups - 2, -1, -1):
        g_sel = jnp.where(gl[g] == gmax, g, g_sel)

    el = []
    for e in range(epg):
        v = lg[groups + e:groups + e + 1, :]
        for g in range(1, groups):
            r = groups + g * epg + e
            v = jnp.where(g_sel == g, lg[r:r + 1, :], v)
        el.append(v)

    def top1(vals):
        vmax = functools.reduce(jnp.maximum, vals)
        idx = jnp.full(vmax.shape, epg - 1, jnp.int32)
        for e in range(epg - 2, -1, -1):
            idx = jnp.where(vals[e] == vmax, e, idx)
        return vmax, idx

    v1, i1 = top1(el)
    v2, i2 = top1([jnp.where(i1 == e, -jnp.inf, el[e]) for e in range(epg)])
    b = jnp.exp(v2 - v1)
    eid_ref[0:1, :] = g_sel * epg + i1
    eid_ref[1:2, :] = g_sel * epg + i2
    wt_ref[0:1, :] = (1.0 / (1.0 + b)) * p_group
    wt_ref[1:2, :] = (b / (1.0 + b)) * p_group


def _mixout(attn, m_a, proj, x2, ada3, norm_g, w_moba_out, w_out, wr_hi, wr_lo, br,
            *, seq, sw, mw, groups, epg, tm):
    t, d = x2.shape
    per_b = seq // tm
    gb_blk = (2 * sw + 3 * mw) // d + 1
    once = pl.Buffered(1)
    kern = functools.partial(_mixout_kernel, groups=groups, epg=epg)

    def ada_spec(k):
        return pl.BlockSpec((1, 1, d), lambda i: ((i // per_b) * 6 + k, 0, 0))

    return pl.pallas_call(
        kern,
        grid=(t // tm,),
        in_specs=[pl.BlockSpec((tm, mw), lambda i: (i, 0)),
                  pl.BlockSpec((tm, d), lambda i: (i, 0)),
                  pl.BlockSpec((tm, d), lambda i: (i, gb_blk)),
                  pl.BlockSpec((tm, d), lambda i: (i, 0)),
                  ada_spec(2), ada_spec(3), ada_spec(4),
                  pl.BlockSpec((1, d), lambda i: (0, 0)),
                  pl.BlockSpec((mw, d), lambda i: (0, 0), pipeline_mode=once),
                  pl.BlockSpec((d, d), lambda i: (0, 0), pipeline_mode=once),
                  pl.BlockSpec((ROUTER_ROWS, d), lambda i: (0, 0)),
                  pl.BlockSpec((ROUTER_ROWS, d), lambda i: (0, 0)),
                  pl.BlockSpec((ROUTER_ROWS, LANES), lambda i: (0, 0))],
        out_specs=[pl.BlockSpec((tm, d), lambda i: (i, 0)),
                   pl.BlockSpec((tm * _pitch(d), LANES), lambda i: (i, 0)),
                   pl.BlockSpec((MOE_TOPK, tm), lambda i: (0, i)),
                   pl.BlockSpec((MOE_TOPK, tm), lambda i: (0, i))],
        out_shape=[jax.ShapeDtypeStruct((t, d), _F32),
                   jax.ShapeDtypeStruct((t * _pitch(d), LANES), _F32),
                   jax.ShapeDtypeStruct((MOE_TOPK, t), jnp.int32),
                   jax.ShapeDtypeStruct((MOE_TOPK, t), _F32)],
        compiler_params=_params(("arbitrary",)),
        name="mixout",
    )(attn, m_a, proj, x2, ada3, ada3, ada3, norm_g.reshape(1, d), w_moba_out, w_out, wr_hi, wr_lo, br)


def _plan_kernel(eid_ref, dest_ref, tab_ref, carry_scr, start_scr, tri_scr, *, n_exp, rb):
    phase = pl.program_id(0)
    first = pl.program_id(1) == 0
    tc = eid_ref.shape[1]
    nbp = tab_ref.shape[1]

    @pl.when((phase == 0) & first)
    def _():
        carry_scr[...] = jnp.zeros_like(carry_scr)
        earlier = lax.broadcasted_iota(jnp.int32, (tc, tc), 0) < lax.broadcasted_iota(jnp.int32, (tc, tc), 1)
        tri_scr[...] = earlier.astype(_BF16)

    eio = lax.broadcasted_iota(jnp.int32, (n_exp, tc), 0)
    oh0 = (eio == eid_ref[0:1, :]).astype(_F32)
    oh1 = (eio == eid_ref[1:2, :]).astype(_F32)
    oh = oh0 + oh1

    @pl.when((phase == 1) & first)
    def _():
        counts = carry_scr[...].astype(jnp.int32)
        blocks = lax.shift_right_logical(counts + (rb - 1), rb.bit_length() - 1).astype(_F32)
        upto = lax.broadcasted_iota(jnp.int32, (n_exp, n_exp), 0) >= lax.broadcasted_iota(jnp.int32, (n_exp, n_exp), 1)
        end_blk = jnp.dot(upto.astype(_BF16), blocks.astype(_BF16), preferred_element_type=_F32)
        start_scr[...] = (end_blk - blocks) * rb
        carry_scr[...] = jnp.zeros_like(carry_scr)
        blk_id = lax.broadcasted_iota(jnp.int32, (n_exp, nbp), 1).astype(_F32)
        owner = jnp.sum((end_blk[:, 0:1] <= blk_id).astype(_F32), axis=0, keepdims=True)
        owner = jnp.minimum(owner, n_exp - 1.0)
        used = jnp.broadcast_to(end_blk[n_exp - 1:n_exp, 0:1], (1, nbp))
        trow = lax.broadcasted_iota(jnp.int32, tab_ref.shape, 0)
        tab_ref[...] = jnp.where(trow == 0, owner, jnp.where(trow == 1, used, 0.0)).astype(jnp.int32)

    @pl.when(phase == 1)
    def _():
        before = jnp.dot(oh.astype(_BF16), tri_scr[...], preferred_element_type=_F32)
        base = start_scr[:, 0:1] + carry_scr[:, 0:1] + before
        dest_ref[0:1, :] = jnp.sum(oh0 * base, axis=0, keepdims=True).astype(jnp.int32)
        dest_ref[1:2, :] = jnp.sum(oh1 * base, axis=0, keepdims=True).astype(jnp.int32)

    carry_scr[...] = carry_scr[...] + jnp.sum(oh, axis=1, keepdims=True)


def _plan(eid, *, n_exp, tc, rb, n_blocks):
    t = eid.shape[1]
    assert rb & (rb - 1) == 0 and t * MOE_TOPK // rb + n_exp < 256
    nbp = -(-n_blocks // LANES) * LANES
    kern = functools.partial(_plan_kernel, n_exp=n_exp, rb=rb)
    return pl.pallas_call(
        kern,
        grid=(2, t // tc),
        in_specs=[pl.BlockSpec((MOE_TOPK, tc), lambda p, i: (0, i))],
        out_specs=[pl.BlockSpec((MOE_TOPK, tc), lambda p, i: (0, i * p)),
                   pl.BlockSpec((8, nbp), lambda p, i: (0, 0))],
        out_shape=[jax.ShapeDtypeStruct((MOE_TOPK, t), jnp.int32),
                   jax.ShapeDtypeStruct((8, nbp), jnp.int32)],
        scratch_shapes=[pltpu.VMEM((n_exp, LANES), _F32), pltpu.VMEM((n_exp, LANES), _F32),
                        pltpu.VMEM((tc, tc), _BF16)],
        compiler_params=_params(("arbitrary", "arbitrary")),
        name="plan",
    )(eid)


def _invert_kernel(dest_ref, pad_ref, inv_ref, sem):
    i = pl.program_id(0)
    tch = dest_ref.shape[1]

    @pl.when(i == 0)
    def _():
        fill = pltpu.make_async_copy(pad_ref, inv_ref, sem)
        fill.start()
        fill.wait()

    def body(t, c):
        a = (i * tch + t) * MOE_TOPK
        for k in range(MOE_TOPK):
            inv_ref[dest_ref[k, t]] = a + k
        return c

    lax.fori_loop(0, tch, body, 0, unroll=8)


def _pad_targets(cap, rb, n_assign):
    r = np.arange(cap + rb)
    blk = r // rb
    slot = np.where(blk == cap // rb, 3, blk % 3)
    return (n_assign + slot * rb + r % rb).astype(np.int32)


def _invert(dest3, cap, rb, n_assign):
    steps, _, tch = dest3.shape
    pad = _pad_targets(cap, rb, n_assign)
    return pl.pallas_call(
        _invert_kernel,
        grid=(steps,),
        in_specs=[pl.BlockSpec((None, MOE_TOPK, tch), lambda i: (i, 0, 0), memory_space=pltpu.SMEM),
                  pl.BlockSpec(memory_space=pl.ANY)],
        out_specs=pl.BlockSpec(pad.shape, lambda i: (0,), memory_space=pltpu.SMEM),
        out_shape=jax.ShapeDtypeStruct(pad.shape, jnp.int32),
        scratch_shapes=[pltpu.SemaphoreType.DMA(())],
        compiler_params=_params(("arbitrary",)),
        name="invert",
    )(dest3, jnp.asarray(pad))


N_TRASH_SLOTS = 4


def _expert_kernel(be_ref, nu_ref, invp_ref, invc_ref, invn_ref, n2_ref, wg_ref, wu_ref, wd_ref, y_ref,
                   wg_s, wu_s, wd_s, xbuf0, xbuf1, ybuf0, ybuf1, gsem, ssem, *, rb, n_tok, d):
    i = pl.program_id(0)
    last = nu_ref[0] - 1
    used = i <= last
    fresh = (i == 0) | (be_ref[i] != be_ref[jnp.maximum(i - 1, 0)])
    ch = d // LANES
    pitch = _pitch(d)
    n_assign = n_tok * MOE_TOPK
    xbuf = (xbuf0, xbuf1)
    ybuf = (ybuf0, ybuf1)

    def aligned(row):
        return pl.multiple_of(row, 8) if pitch % 8 == 0 else row

    def gather(inv_ref, p):
        for r in range(rb):
            tok = jnp.minimum(lax.shift_right_logical(inv_ref[0, r], 1), n_tok - 1)
            pltpu.make_async_copy(n2_ref.at[pl.ds(aligned(tok * pitch), ch)],
                                  xbuf[p].at[pl.ds(r * pitch, ch)], gsem.at[p]).start(priority=r % 2)

    def gather_wait(p):
        pltpu.make_async_copy(n2_ref.at[pl.ds(0, rb * ch)], xbuf[p].at[pl.ds(0, rb * ch)], gsem.at[p]).wait()

    def scatter(inv_ref, p):
        for r in range(rb):
            pltpu.make_async_copy(ybuf[p].at[pl.ds(r * pitch, pitch)],
                                  y_ref.at[pl.ds(aligned(inv_ref[0, r] * pitch), pitch)],
                                  ssem.at[p]).start(priority=r % 2)

    def scatter_wait(p):
        pltpu.make_async_copy(ybuf[p], y_ref.at[pl.ds(0, rb * pitch)], ssem.at[p]).wait()

    @pl.when(i == 0)
    def _():
        gather(invc_ref, 0)
        for buf in ybuf:
            buf[...] = jnp.zeros(buf.shape, _F32)
        fills = [pltpu.make_async_copy(ybuf[1], y_ref.at[pl.ds((n_assign + k * rb) * pitch, rb * pitch)],
                                       ssem.at[1]) for k in range(N_TRASH_SLOTS)]
        for fill in fills:
            fill.start()
        for fill in fills:
            fill.wait()

    @pl.when(used & fresh)
    def _():
        wg_s[...] = wg_ref[0].astype(_BF16)
        wu_s[...] = wu_ref[0].astype(_BF16)
        wd_s[...] = wd_ref[0].astype(_BF16)

    def step(p):
        @pl.when(i >= 1)
        def _():
            scatter_wait(p)

        gather_wait(p)
        gather(invn_ref, 1 - p)
        scatter(invp_ref, 1 - p)
        xb = _load_token_major(xbuf[p], rb, d).astype(_BF16)
        gate = jnp.dot(xb, wg_s[...], preferred_element_type=_F32)
        up = jnp.dot(xb, wu_s[...], preferred_element_type=_F32)
        hid = (jax.nn.silu(gate) * up).astype(_BF16)
        _store_token_major(ybuf[p], jnp.dot(hid, wd_s[...], preferred_element_type=_F32), zero_pad=False)

        @pl.when(i == last)
        def _():
            scatter(invc_ref, p)
            scatter_wait(p)
            scatter_wait(1 - p)
            gather_wait(1 - p)

    for p in range(2):
        pl.when(used & (lax.rem(i, 2) == p))(functools.partial(step, p))


def _experts(block_expert, n_used, inv3, n2, w_gate, w_up, w_down, *, n_tok):
    nblk, _, rb = inv3.shape
    nblk -= 1
    d, f = w_gate.shape[1:]
    pitch = _pitch(d)

    def inv_spec(shift):
        def index(i, be, nu):
            blk = jnp.clip(i + shift, 0, nu[0] - 1)
            return (jnp.where(i + shift < 0, nblk, blk), 0, 0)
        return pl.BlockSpec((None, 1, rb), index, memory_space=pltpu.SMEM)

    grid_spec = pltpu.PrefetchScalarGridSpec(
        num_scalar_prefetch=2,
        grid=(nblk,),
        in_specs=[inv_spec(-1), inv_spec(0), inv_spec(1),
                  pl.BlockSpec(memory_space=pl.ANY),
                  pl.BlockSpec((1, d, f), lambda i, be, nu: (be[i], 0, 0)),
                  pl.BlockSpec((1, d, f), lambda i, be, nu: (be[i], 0, 0)),
                  pl.BlockSpec((1, f, d), lambda i, be, nu: (be[i], 0, 0))],
        out_specs=pl.BlockSpec(memory_space=pl.ANY),
        scratch_shapes=[pltpu.VMEM((d, f), _BF16), pltpu.VMEM((d, f), _BF16), pltpu.VMEM((f, d), _BF16),
                        *[pltpu.VMEM((rb * pitch, LANES), _F32) for _ in range(4)],
                        pltpu.SemaphoreType.DMA((2,)), pltpu.SemaphoreType.DMA((2,))],
    )
    kern = functools.partial(_expert_kernel, rb=rb, n_tok=n_tok, d=d)
    n_chunks = n_tok * MOE_TOPK + N_TRASH_SLOTS * rb
    return pl.pallas_call(
        kern,
        grid_spec=grid_spec,
        out_shape=jax.ShapeDtypeStruct((n_chunks * pitch, LANES), _F32),
        compiler_params=_params(("arbitrary",)),
        name="experts",
    )(block_expert, n_used, inv3, inv3, inv3, n2, w_gate, w_up, w_down)


def _combine_kernel(h_ref, wc_ref, gf_ref, ng_ref, y_ref, o_ref):
    tk, d = h_ref.shape
    pitch = _pitch(d)
    y = [_load_token_major(y_ref, tk, d, first=k * pitch, pitch=MOE_TOPK * pitch) for k in range(MOE_TOPK)]
    moe = wc_ref[:, 0:1] * y[0] + wc_ref[:, 1:2] * y[1]
    h = h_ref[...] + gf_ref[0] * moe
    o_ref[...] = h * lax.rsqrt(jnp.mean(h * h, axis=-1, keepdims=True) + EPS) * ng_ref[...]


def _combine(h1, w_cols, ada3, norm_g, y_assign, *, seq, tk):
    t, d = h1.shape
    per_b = seq // tk
    assert y_assign.shape[0] % (tk * MOE_TOPK * _pitch(d)) == 0
    return pl.pallas_call(
        _combine_kernel,
        grid=(t // tk,),
        in_specs=[pl.BlockSpec((tk, d), lambda i: (i, 0)),
                  pl.BlockSpec((tk, LANES), lambda i: (i, 0)),
                  pl.BlockSpec((1, 1, d), lambda i: ((i // per_b) * 6 + 5, 0, 0)),
                  pl.BlockSpec((1, d), lambda i: (0, 0)),
                  pl.BlockSpec((tk * MOE_TOPK * _pitch(d), LANES), lambda i: (i, 0))],
        out_specs=pl.BlockSpec((tk, d), lambda i: (i, 0)),
        out_shape=jax.ShapeDtypeStruct((t, d), _F32),
        compiler_params=_params(("arbitrary",)),
        name="combine",
    )(h1, w_cols, ada3, norm_g.reshape(1, d), y_assign)


def _rotary_tables(positions, dh):
    inv_freq = ROPE_THETA ** (-jnp.arange(0, dh, 2, dtype=_F32) / dh)
    ang = positions.astype(_F32)[..., None] * inv_freq
    cos, sin = jnp.cos(ang), jnp.sin(ang)
    t = cos.shape[0] * cos.shape[1]
    cos = jnp.concatenate([cos, cos], axis=-1).reshape(t, dh)
    sin = jnp.concatenate([-sin, sin], axis=-1).reshape(t, dh)
    return cos, sin


def _tile(n, want):
    while n % want:
        want //= 2
    return want


def kernel(x, c, positions, w_ada, b_ada, norm_mix_g, w_in, sgu_ln_g, sgu_ln_b, sgu_w_s, sgu_b_s,
           w_sgu_out, w_moba_out, w_out, norm_ffn_g, w_route_group, b_route_group, w_route_expert,
           b_route_expert, w_exp_gate, w_exp_up, w_exp_down, norm_final_g):
    batch, seq, d = x.shape
    depth = w_ada.shape[0]
    t = batch * seq
    sw = sgu_ln_g.shape[1]
    mw = w_moba_out.shape[1]
    dh = mw // MOBA_HEADS
    groups = w_route_group.shape[2]
    n_exp = w_route_expert.shape[2]
    epg = n_exp // groups
    rb = EXPERT_ROW_BLOCK
    cap = (t * MOE_TOPK // rb + n_exp) * rb
    assert depth == 1, "the final RMSNorm is fused into the single layer's combine"
    assert dh == LANES and seq % MOBA_BLOCK == 0 and sw * 2 == d and mw % sw == 0
    assert groups + n_exp <= ROUTER_ROWS and (t * MOE_TOPK) % rb == 0

    cos, sin = _rotary_tables(positions, dh)
    c_pad = jnp.zeros((8, d), _F32).at[:batch].set(c)
    h = x.reshape(t, d)

    for l in range(depth):
        ada = _ada(c_pad, w_ada[l], b_ada[l], _tile(6 * d, 1024))
        ada3 = ada[:batch].reshape(batch * 6, 1, d)

        proj = _inproj(h, ada3, norm_mix_g[l], w_in[l].astype(_BF16), cos, sin, sgu_ln_g[l], sgu_ln_b[l],
                       seq=seq, sw=sw, mw=mw, dh=dh, tm=_tile(seq, 1024))
        bs_wide = jnp.repeat(sgu_b_s[l].T, sw // sgu_w_s.shape[1], axis=1)
        m_a = _sgu(proj, sgu_w_s[l], bs_wide, w_sgu_out[l].astype(_BF16), sw=sw, mw=mw, tm=_tile(seq, 512))
        attn = _moba(proj, batch=batch, seq=seq, sw=sw, mw=mw, dh=dh, hg=min(MOBA_HEAD_GROUP, MOBA_HEADS))

        wr = jnp.concatenate([w_route_group[l], w_route_expert[l]], axis=1).T
        wr = jnp.zeros((ROUTER_ROWS, d), _F32).at[:groups + n_exp].set(wr)
        wr_hi, wr_lo = _split_bf16(wr)
        br = jnp.concatenate([b_route_group[l], b_route_expert[l].reshape(-1)])
        br = jnp.broadcast_to(jnp.zeros((ROUTER_ROWS,), _F32).at[:groups + n_exp].set(br)[:, None],
                              (ROUTER_ROWS, LANES))
        h1, n2, eid, wt = _mixout(attn, m_a, proj, h, ada3, norm_ffn_g[l], w_moba_out[l].astype(_BF16),
                                  w_out[l].astype(_BF16), wr_hi, wr_lo, br,
                                  seq=seq, sw=sw, mw=mw, groups=groups, epg=epg, tm=_tile(seq, 256))

        dest, tab = _plan(eid, n_exp=n_exp, tc=_tile(t, 512), rb=rb, n_blocks=cap // rb)
        tch = _tile(t, 2048)
        inv = _invert(dest.reshape(MOE_TOPK, t // tch, tch).transpose(1, 0, 2), cap, rb, t * MOE_TOPK)
        y_assign = _experts(tab[0, :cap // rb], tab[1, :1], inv.reshape(cap // rb + 1, 1, rb), n2,
                            w_exp_gate[l], w_exp_up[l], w_exp_down[l], n_tok=t)
        w_cols = jnp.zeros((t, LANES), _F32).at[:, :MOE_TOPK].set(wt.T)
        h = _combine(h1, w_cols, ada3, norm_final_g, y_assign, seq=seq, tk=_tile(seq, 256))

    return h.reshape(batch, seq, d)
```

```python
import functools

import jax
import jax.numpy as jnp
import numpy as np
from jax import lax
from jax.experimental import pallas as pl
from jax.experimental.pallas import tpu as pltpu

MOBA_HEADS = 16
MOBA_BLOCK = 256
MOBA_TOPK = 3
SGU_CHUNK = 128
ROPE_THETA = 10000.0
MOE_GROUPS = 4
MOE_TOPK = 2
EPS = 1e-6
NEG_INF = -1e30
LOG2_E = 1.4426950408889634

LANES = 128
TOKEN_PAD_ROWS = 8
EXPERT_ROW_BLOCK = 256
INPROJ_CHUNK = 256
MOBA_HEAD_GROUP = 4
ROUTER_ROWS = 40
VMEM_LIMIT = 56 * 1024 * 1024

_F32 = jnp.float32
_BF16 = jnp.bfloat16
_NT = (((1,), (1,)), ((), ()))


def _params(semantics, vmem=VMEM_LIMIT):
    return pltpu.CompilerParams(dimension_semantics=semantics, vmem_limit_bytes=vmem)


def _pitch(d):
    return d // LANES + TOKEN_PAD_ROWS


def _store_token_major(ref, val, zero_pad=True):
    n, d = val.shape
    pitch = ref.shape[0] // n
    for c in range(pitch if zero_pad else d // LANES):
        piece = val[:, c * LANES:(c + 1) * LANES] if c < d // LANES else jnp.zeros((n, LANES), val.dtype)
        ref[pl.ds(c, n, stride=pitch), :] = piece


def _load_token_major(ref, n, d, first=0, pitch=None):
    pitch = pitch or ref.shape[0] // n
    return jnp.concatenate([ref[pl.ds(first + c, n, stride=pitch), :] for c in range(d // LANES)], axis=1)


def _split_bf16(a):
    hi = a.astype(_BF16)
    lo = (a - hi.astype(_F32)).astype(_BF16)
    return hi, lo


def _ada_kernel(c_ref, w_ref, b_ref, o_ref):
    ca_hi, ca_lo = _split_bf16(jax.nn.silu(c_ref[...]))
    w_hi, w_lo = _split_bf16(w_ref[...])
    acc = jnp.dot(ca_hi, w_hi, preferred_element_type=_F32)
    acc += jnp.dot(ca_hi, w_lo, preferred_element_type=_F32)
    acc += jnp.dot(ca_lo, w_hi, preferred_element_type=_F32)
    o_ref[...] = acc + b_ref[...]


def _ada(c_pad, w_ada, b_ada, tn):
    rows, d = c_pad.shape
    n = w_ada.shape[1]
    return pl.pallas_call(
        _ada_kernel,
        grid=(n // tn,),
        in_specs=[pl.BlockSpec((rows, d), lambda j: (0, 0)),
                  pl.BlockSpec((d, tn), lambda j: (0, j)),
                  pl.BlockSpec((1, tn), lambda j: (0, j))],
        out_specs=pl.BlockSpec((rows, tn), lambda j: (0, j)),
        out_shape=jax.ShapeDtypeStruct((rows, n), _F32),
        compiler_params=_params(("arbitrary",)),
        name="ada",
    )(c_pad, w_ada, b_ada.reshape(1, n))


def _rms_modulate(x, g, sc, sh):
    y = x * lax.rsqrt(jnp.mean(x * x, axis=-1, keepdims=True) + EPS) * g
    return y * (1.0 + sc) + sh


def _inproj_kernel(x_ref, g_ref, sc_ref, sh_ref, w_ref, cos_ref, sin_ref, lng_ref, lnb_ref,
                   o_ref, n_scr, z_scr, *, q0, k0, v0, dh, scale, cw):
    j = pl.program_id(1)
    tn = o_ref.shape[1]

    @pl.when(j == 0)
    def _():
        n_scr[...] = _rms_modulate(x_ref[...], g_ref[...], sc_ref[0], sh_ref[0]).astype(_BF16)

    def chunk(c):
        return jnp.dot(n_scr[...], w_ref[:, c * cw:(c + 1) * cw], preferred_element_type=_F32)

    @pl.when(j < q0)
    def _():
        for c in range(tn // cw):
            z_scr[:, c * cw:(c + 1) * cw] = jax.nn.gelu(chunk(c))
        z = z_scr[...]
        is_v = j == 1
        mu = jnp.where(is_v, jnp.mean(z, axis=-1, keepdims=True), 0.0)
        zc = z - mu
        var = jnp.mean(zc * zc, axis=-1, keepdims=True)
        gain = jnp.where(is_v, lax.rsqrt(var + EPS) * lng_ref[...], 1.0)
        o_ref[...] = (zc * gain + jnp.where(is_v, lnb_ref[...], 0.0)).astype(_BF16)

    @pl.when(j >= q0)
    def _():
        mult = jnp.where(j < k0, scale, 1.0)
        cos = jnp.where(j < v0, cos_ref[...], 1.0) * mult
        sin = jnp.where(j < v0, sin_ref[...], 0.0) * mult
        for c in range(tn // cw):
            acc = chunk(c)
            for h in range(cw // dh):
                xh = acc[:, h * dh:(h + 1) * dh]
                rot = xh * cos + pltpu.roll(xh, dh // 2, 1) * sin
                o_ref[:, c * cw + h * dh:c * cw + (h + 1) * dh] = rot.astype(_BF16)


def _inproj(x2, ada3, norm_g, w_in, cos, sin, ln_g, ln_b, *, seq, sw, mw, dh, tm):
    t, d = x2.shape
    n = w_in.shape[1]
    tn = sw
    per_b = seq // tm
    q0 = 2
    k0 = q0 + mw // tn
    v0 = k0 + mw // tn
    cw = _tile(tn, INPROJ_CHUNK)
    assert cw % dh == 0
    kern = functools.partial(_inproj_kernel, q0=q0, k0=k0, v0=v0, dh=dh, scale=dh ** -0.5 * LOG2_E, cw=cw)
    return pl.pallas_call(
        kern,
        grid=(t // tm, n // tn),
        in_specs=[pl.BlockSpec((tm, d), lambda i, j: (i, 0)),
                  pl.BlockSpec((1, d), lambda i, j: (0, 0)),
                  pl.BlockSpec((1, 1, d), lambda i, j: ((i // per_b) * 6 + 1, 0, 0)),
                  pl.BlockSpec((1, 1, d), lambda i, j: ((i // per_b) * 6 + 0, 0, 0)),
                  pl.BlockSpec((d, tn), lambda i, j: (0, j)),
                  pl.BlockSpec((tm, dh), lambda i, j: (i, 0)),
                  pl.BlockSpec((tm, dh), lambda i, j: (i, 0)),
                  pl.BlockSpec((1, sw), lambda i, j: (0, 0)),
                  pl.BlockSpec((1, sw), lambda i, j: (0, 0))],
        out_specs=pl.BlockSpec((tm, tn), lambda i, j: (i, j)),
        out_shape=jax.ShapeDtypeStruct((t, n), _BF16),
        scratch_shapes=[pltpu.VMEM((tm, d), _BF16), pltpu.VMEM((tm, tn), _F32)],
        compiler_params=_params(("arbitrary", "arbitrary")),
        name="inproj",
    )(x2, norm_g.reshape(1, d), ada3, ada3, w_in, cos, sin, ln_g.reshape(1, sw), ln_b.reshape(1, sw))


def _sgu_kernel(u_ref, v_ref, ga_ref, ws_ref, bs_ref, wo_ref, o_ref, gated_scr, *, groups, cg):
    c = SGU_CHUNK
    tri = lax.broadcasted_iota(jnp.int32, (c, c), 0) >= lax.broadcasted_iota(jnp.int32, (c, c), 1)
    for g in range(groups):
        wg = jnp.where(tri, ws_ref[g], 0.0).astype(_BF16)
        cols = slice(g * cg, (g + 1) * cg)
        for ci in range(u_ref.shape[0] // c):
            rows = slice(ci * c, (ci + 1) * c)
            sv = jnp.dot(wg, v_ref[rows, cols], preferred_element_type=_F32) + bs_ref[:, cols]
            gated_scr[rows, cols] = (u_ref[rows, cols].astype(_F32) * sv).astype(_BF16)
    ya = jnp.dot(gated_scr[...], wo_ref[...], preferred_element_type=_F32)
    o_ref[...] = (jax.nn.sigmoid(ga_ref[...].astype(_F32)) * ya).astype(_BF16)


def _sgu(proj, w_s, bs_wide, w_sgu_out, *, sw, mw, tm):
    t = proj.shape[0]
    d = w_sgu_out.shape[1]
    groups = w_s.shape[0]
    ga_blk = (2 * sw + 3 * mw) // d
    kern = functools.partial(_sgu_kernel, groups=groups, cg=sw // groups)
    return pl.pallas_call(
        kern,
        grid=(t // tm,),
        in_specs=[pl.BlockSpec((tm, sw), lambda i: (i, 0)),
                  pl.BlockSpec((tm, sw), lambda i: (i, 1)),
                  pl.BlockSpec((tm, d), lambda i: (i, ga_blk)),
                  pl.BlockSpec(w_s.shape, lambda i: (0, 0, 0)),
                  pl.BlockSpec(bs_wide.shape, lambda i: (0, 0)),
                  pl.BlockSpec(w_sgu_out.shape, lambda i: (0, 0))],
        out_specs=pl.BlockSpec((tm, d), lambda i: (i, 0)),
        out_shape=jax.ShapeDtypeStruct((t, d), _BF16),
        scratch_shapes=[pltpu.VMEM((tm, sw), _BF16)],
        compiler_params=_params(("arbitrary",)),
        name="sgu",
    )(proj, proj, proj, w_s, bs_wide, w_sgu_out)


def _moba_kernel(q_ref, k_ref, v_ref, o_ref, kmh_scr, kml_scr, vt_scr, sel_scr,
                 s_scr, p_scr, acc_scr, m_scr, l_scr, a_scr, *, nb, topk, hg, dh):
    qi = pl.program_id(2)
    blk = MOBA_BLOCK
    tq = q_ref.shape[0]

    @pl.when(qi == 0)
    def _():
        for h in range(hg):
            cols = slice(h * dh, (h + 1) * dh)
            kf = k_ref[:, cols].astype(_F32).reshape(nb, blk, dh)
            km_hi, km_lo = _split_bf16(jnp.mean(kf, axis=1))
            kmh_scr[h * nb:(h + 1) * nb, :] = km_hi
            kml_scr[h * nb:(h + 1) * nb, :] = km_lo
            for c in range(nb):
                rows = slice(c * blk, (c + 1) * blk)
                vt_scr[cols, rows] = v_ref[rows, cols].astype(_F32).T.astype(_BF16)

    row = lax.broadcasted_iota(jnp.int32, (nb, tq), 0)
    past = row < qi
    kpos = lax.broadcasted_iota(jnp.int32, (blk, tq), 0)
    qpos = lax.broadcasted_iota(jnp.int32, (blk, tq), 1)
    own = pl.multiple_of(qi * blk, blk)

    def scores(start, h):
        cols = slice(h * dh, (h + 1) * dh)
        return lax.dot_general(k_ref[pl.ds(start, blk), cols], q_ref[:, cols], _NT,
                               preferred_element_type=_F32)

    def weighted_values(start, h, p):
        return jnp.dot(vt_scr[h * dh:(h + 1) * dh, pl.ds(start, blk)], p, preferred_element_type=_F32)

    for h in range(hg):
        hrows = slice(h * nb, (h + 1) * nb)
        q = q_ref[:, h * dh:(h + 1) * dh]
        gate = (lax.dot_general(kmh_scr[hrows, :], q, _NT, preferred_element_type=_F32)
                + lax.dot_general(kml_scr[hrows, :], q, _NT, preferred_element_type=_F32))
        gm = jnp.where(past, gate, NEG_INF)
        beaten = jnp.zeros((nb, tq), jnp.int32)
        for jp in range(nb):
            other = gm[jp:jp + 1, :]
            wins = (other > gm) | ((other == gm) & (jp < row))
            beaten += wins.astype(jnp.int32)
        sel_scr[hrows, :] = (past & (beaten < topk)).astype(_F32)

        s = jnp.where(kpos <= qpos, scores(own, h), NEG_INF)
        m0 = jnp.max(s, axis=0, keepdims=True)
        p = jnp.exp2(s - m0)
        m_scr[h] = m0
        l_scr[h] = jnp.sum(p, axis=0, keepdims=True)
        a_scr[h] = jnp.ones_like(m0)
        acc_scr[h] = jnp.zeros(acc_scr.shape[1:], _F32)
        p_scr[0, h] = p.astype(_BF16)
        s_scr[0, h] = scores(0, h)
        s_scr[1, h] = scores(blk, h)

    def body(j, c):
        par = lax.rem(j, 2)
        cur = lax.rem(j, 3)
        prev = pl.multiple_of(jnp.where(j == 0, qi, j - 1) * blk, blk)
        nxt = pl.multiple_of(jnp.minimum(j + 2, nb - 1) * blk, blk)
        for h in range(hg):
            acc_scr[h] = a_scr[h] * acc_scr[h] + weighted_values(prev, h, p_scr[par, h])
        for h in range(hg):
            s = jnp.where(sel_scr[pl.ds(h * nb + j, 1), :] > 0.0, s_scr[cur, h], NEG_INF)
            m = m_scr[h]
            m_new = jnp.maximum(m, jnp.max(s, axis=0, keepdims=True))
            alpha = jnp.exp2(m - m_new)
            p = jnp.exp2(s - m_new)
            l_scr[h] = alpha * l_scr[h] + jnp.sum(p, axis=0, keepdims=True)
            m_scr[h] = m_new
            a_scr[h] = alpha
            p_scr[1 - par, h] = p.astype(_BF16)
        for h in range(hg):
            s_scr[lax.rem(j + 2, 3), h] = scores(nxt, h)
        return c

    lax.fori_loop(0, qi, body, 0)
    last = pl.multiple_of(jnp.where(qi == 0, qi, qi - 1) * blk, blk)
    for h in range(hg):
        acc = a_scr[h] * acc_scr[h] + weighted_values(last, h, p_scr[lax.rem(qi, 2), h])
        o_ref[:, h * dh:(h + 1) * dh] = (acc / l_scr[h]).T.astype(_BF16)


def _moba(proj, *, batch, seq, sw, mw, dh, hg):
    t = proj.shape[0]
    heads = mw // dh
    nb = seq // MOBA_BLOCK
    tq = MOBA_BLOCK
    nq = seq // tq
    gw = hg * dh
    qc = 2 * sw // gw
    kc = qc + heads // hg
    vc = kc + heads // hg
    kern = functools.partial(_moba_kernel, nb=nb, topk=MOBA_TOPK, hg=hg, dh=dh)
    return pl.pallas_call(
        kern,
        grid=(batch, heads // hg, nq),
        in_specs=[pl.BlockSpec((tq, gw), lambda b, h, i: (b * nq + i, qc + h)),
                  pl.BlockSpec((seq, gw), lambda b, h, i: (b, kc + h)),
                  pl.BlockSpec((seq, gw), lambda b, h, i: (b, vc + h))],
        out_specs=pl.BlockSpec((tq, gw), lambda b, h, i: (b * nq + i, h)),
        out_shape=jax.ShapeDtypeStruct((t, mw), _BF16),
        scratch_shapes=[pltpu.VMEM((hg * nb, dh), _BF16), pltpu.VMEM((hg * nb, dh), _BF16),
                        pltpu.VMEM((gw, seq), _BF16), pltpu.VMEM((hg * nb, tq), _F32),
                        pltpu.VMEM((3, hg, MOBA_BLOCK, tq), _F32), pltpu.VMEM((2, hg, MOBA_BLOCK, tq), _BF16),
                        pltpu.VMEM((hg, dh, tq), _F32), pltpu.VMEM((hg, 1, tq), _F32),
                        pltpu.VMEM((hg, 1, tq), _F32), pltpu.VMEM((hg, 1, tq), _F32)],
        compiler_params=_params(("arbitrary", "arbitrary", "arbitrary")),
        name="moba",
    )(proj, proj, proj)


def _mixout_kernel(attn_ref, ma_ref, gb_ref, x_ref, gm_ref, shf_ref, scf_ref, ng_ref,
                   wmo_ref, wo_ref, wrh_ref, wrl_ref, br_ref,
                   h_ref, n2_ref, eid_ref, wt_ref, *, groups, epg):
    yb = jnp.dot(attn_ref[...], wmo_ref[...], preferred_element_type=_F32)
    merged = ma_ref[...].astype(_F32) + jax.nn.sigmoid(gb_ref[...].astype(_F32)) * yb
    mix = jnp.dot(merged.astype(_BF16), wo_ref[...], preferred_element_type=_F32)
    h = x_ref[...] + gm_ref[0] * mix
    h_ref[...] = h
    n2 = _rms_modulate(h, ng_ref[...], scf_ref[0], shf_ref[0])
    _store_token_major(n2_ref, n2)

    n_hi, n_lo = _split_bf16(n2)
    lg = (lax.dot_general(wrh_ref[...], n_hi, _NT, preferred_element_type=_F32)
          + lax.dot_general(wrh_ref[...], n_lo, _NT, preferred_element_type=_F32)
          + lax.dot_general(wrl_ref[...], n_hi, _NT, preferred_element_type=_F32)) + br_ref[:, 0:1]

    gl = [lg[g:g + 1, :] for g in range(groups)]
    gmax = functools.reduce(jnp.maximum, gl)
    denom = functools.reduce(jnp.add, [jnp.exp(v - gmax) for v in gl])
    p_group = 1.0 / denom
    g_sel = jnp.full(gmax.shape, groups - 1, jnp.int32)
    for g in range(groups - 2, -1, -1):
        g_sel = jnp.where(gl[g] == gmax, g, g_sel)

    el = []
    for e in range(epg):
        v = lg[groups + e:groups + e + 1, :]
        for g in range(1, groups):
            r = groups + g * epg + e
            v = jnp.where(g_sel == g, lg[r:r + 1, :], v)
        el.append(v)

    def top1(vals):
        vmax = functools.reduce(jnp.maximum, vals)
        idx = jnp.full(vmax.shape, epg - 1, jnp.int32)
        for e in range(epg - 2, -1, -1):
            idx = jnp.where(vals[e] == vmax, e, idx)
        return vmax, idx

    v1, i1 = top1(el)
    v2, i2 = top1([jnp.where(i1 == e, -jnp.inf, el[e]) for e in range(epg)])
    b = jnp.exp(v2 - v1)
    eid_ref[0:1, :] = g_sel * epg + i1
    eid_ref[1:2, :] = g_sel * epg + i2
    wt_ref[0:1, :] = (1.0 / (1.0 + b)) * p_group
    wt_ref[1:2, :] = (b / (1.0 + b)) * p_group


def _mixout(attn, m_a, proj, x2, ada3, norm_g, w_moba_out, w_out, wr_hi, wr_lo, br,
            *, seq, sw, mw, groups, epg, tm):
    t, d = x2.shape
    per_b = seq // tm
    gb_blk = (2 * sw + 3 * mw) // d + 1
    once = pl.Buffered(1)
    kern = functools.partial(_mixout_kernel, groups=groups, epg=epg)

    def ada_spec(k):
        return pl.BlockSpec((1, 1, d), lambda i: ((i // per_b) * 6 + k, 0, 0))

    return pl.pallas_call(
        kern,
        grid=(t // tm,),
        in_specs=[pl.BlockSpec((tm, mw), lambda i: (i, 0)),
                  pl.BlockSpec((tm, d), lambda i: (i, 0)),
                  pl.BlockSpec((tm, d), lambda i: (i, gb_blk)),
                  pl.BlockSpec((tm, d), lambda i: (i, 0)),
                  ada_spec(2), ada_spec(3), ada_spec(4),
                  pl.BlockSpec((1, d), lambda i: (0, 0)),
                  pl.BlockSpec((mw, d), lambda i: (0, 0), pipeline_mode=once),
                  pl.BlockSpec((d, d), lambda i: (0, 0), pipeline_mode=once),
                  pl.BlockSpec((ROUTER_ROWS, d), lambda i: (0, 0)),
                  pl.BlockSpec((ROUTER_ROWS, d), lambda i: (0, 0)),
                  pl.BlockSpec((ROUTER_ROWS, LANES), lambda i: (0, 0))],
        out_specs=[pl.BlockSpec((tm, d), lambda i: (i, 0)),
                   pl.BlockSpec((tm * _pitch(d), LANES), lambda i: (i, 0)),
                   pl.BlockSpec((MOE_TOPK, tm), lambda i: (0, i)),
                   pl.BlockSpec((MOE_TOPK, tm), lambda i: (0, i))],
        out_shape=[jax.ShapeDtypeStruct((t, d), _F32),
                   jax.ShapeDtypeStruct((t * _pitch(d), LANES), _F32),
                   jax.ShapeDtypeStruct((MOE_TOPK, t), jnp.int32),
                   jax.ShapeDtypeStruct((MOE_TOPK, t), _F32)],
        compiler_params=_params(("arbitrary",)),
        name="mixout",
    )(attn, m_a, proj, x2, ada3, ada3, ada3, norm_g.reshape(1, d), w_moba_out, w_out, wr_hi, wr_lo, br)


def _plan_kernel(eid_ref, dest_ref, tab_ref, carry_scr, start_scr, tri_scr, *, n_exp, rb):
    phase = pl.program_id(0)
    first = pl.program_id(1) == 0
    tc = eid_ref.shape[1]
    nbp = tab_ref.shape[1]

    @pl.when((phase == 0) & first)
    def _():
        carry_scr[...] = jnp.zeros_like(carry_scr)
        earlier = lax.broadcasted_iota(jnp.int32, (tc, tc), 0) < lax.broadcasted_iota(jnp.int32, (tc, tc), 1)
        tri_scr[...] = earlier.astype(_BF16)

    eio = lax.broadcasted_iota(jnp.int32, (n_exp, tc), 0)
    oh0 = (eio == eid_ref[0:1, :]).astype(_F32)
    oh1 = (eio == eid_ref[1:2, :]).astype(_F32)
    oh = oh0 + oh1

    @pl.when((phase == 1) & first)
    def _():
        counts = carry_scr[...].astype(jnp.int32)
        blocks = lax.shift_right_logical(counts + (rb - 1), rb.bit_length() - 1).astype(_F32)
        upto = lax.broadcasted_iota(jnp.int32, (n_exp, n_exp), 0) >= lax.broadcasted_iota(jnp.int32, (n_exp, n_exp), 1)
        end_blk = jnp.dot(upto.astype(_BF16), blocks.astype(_BF16), preferred_element_type=_F32)
        start_scr[...] = (end_blk - blocks) * rb
        carry_scr[...] = jnp.zeros_like(carry_scr)
        blk_id = lax.broadcasted_iota(jnp.int32, (n_exp, nbp), 1).astype(_F32)
        owner = jnp.sum((end_blk[:, 0:1] <= blk_id).astype(_F32), axis=0, keepdims=True)
        owner = jnp.minimum(owner, n_exp - 1.0)
        used = jnp.broadcast_to(end_blk[n_exp - 1:n_exp, 0:1], (1, nbp))
        trow = lax.broadcasted_iota(jnp.int32, tab_ref.shape, 0)
        tab_ref[...] = jnp.where(trow == 0, owner, jnp.where(trow == 1, used, 0.0)).astype(jnp.int32)

    @pl.when(phase == 1)
    def _():
        before = jnp.dot(oh.astype(_BF16), tri_scr[...], preferred_element_type=_F32)
        base = start_scr[:, 0:1] + carry_scr[:, 0:1] + before
        dest_ref[0:1, :] = jnp.sum(oh0 * base, axis=0, keepdims=True).astype(jnp.int32)
        dest_ref[1:2, :] = jnp.sum(oh1 * base, axis=0, keepdims=True).astype(jnp.int32)

    carry_scr[...] = carry_scr[...] + jnp.sum(oh, axis=1, keepdims=True)


def _plan(eid, *, n_exp, tc, rb, n_blocks):
    t = eid.shape[1]
    assert rb & (rb - 1) == 0 and t * MOE_TOPK // rb + n_exp < 256
    nbp = -(-n_blocks // LANES) * LANES
    kern = functools.partial(_plan_kernel, n_exp=n_exp, rb=rb)
    return pl.pallas_call(
        kern,
        grid=(2, t // tc),
        in_specs=[pl.BlockSpec((MOE_TOPK, tc), lambda p, i: (0, i))],
        out_specs=[pl.BlockSpec((MOE_TOPK, tc), lambda p, i: (0, i * p)),
                   pl.BlockSpec((8, nbp), lambda p, i: (0, 0))],
        out_shape=[jax.ShapeDtypeStruct((MOE_TOPK, t), jnp.int32),
                   jax.ShapeDtypeStruct((8, nbp), jnp.int32)],
        scratch_shapes=[pltpu.VMEM((n_exp, LANES), _F32), pltpu.VMEM((n_exp, LANES), _F32),
                        pltpu.VMEM((tc, tc), _BF16)],
        compiler_params=_params(("arbitrary", "arbitrary")),
        name="plan",
    )(eid)


def _invert_kernel(dest_ref, pad_ref, inv_ref, sem):
    i = pl.program_id(0)
    tch = dest_ref.shape[1]

    @pl.when(i == 0)
    def _():
        fill = pltpu.make_async_copy(pad_ref, inv_ref, sem)
        fill.start()
        fill.wait()

    def body(t, c):
        a = (i * tch + t) * MOE_TOPK
        for k in range(MOE_TOPK):
            inv_ref[dest_ref[k, t]] = a + k
        return c

    lax.fori_loop(0, tch, body, 0, unroll=8)


def _pad_targets(cap, rb, n_assign):
    r = np.arange(cap + rb)
    blk = r // rb
    slot = np.where(blk == cap // rb, 3, blk % 3)
    return (n_assign + slot * rb + r % rb).astype(np.int32)


def _invert(dest3, cap, rb, n_assign):
    steps, _, tch = dest3.shape
    pad = _pad_targets(cap, rb, n_assign)
    return pl.pallas_call(
        _invert_kernel,
        grid=(steps,),
        in_specs=[pl.BlockSpec((None, MOE_TOPK, tch), lambda i: (i, 0, 0), memory_space=pltpu.SMEM),
                  pl.BlockSpec(memory_space=pl.ANY)],
        out_specs=pl.BlockSpec(pad.shape, lambda i: (0,), memory_space=pltpu.SMEM),
        out_shape=jax.ShapeDtypeStruct(pad.shape, jnp.int32),
        scratch_shapes=[pltpu.SemaphoreType.DMA(())],
        compiler_params=_params(("arbitrary",)),
        name="invert",
    )(dest3, jnp.asarray(pad))


N_TRASH_SLOTS = 4


def _expert_kernel(be_ref, nu_ref, invp_ref, invc_ref, invn_ref, n2_ref, wg_ref, wu_ref, wd_ref, y_ref,
                   wg_s, wu_s, wd_s, xbuf0, xbuf1, ybuf0, ybuf1, gsem, ssem, *, rb, n_tok, d):
    i = pl.program_id(0)
    last = nu_ref[0] - 1
    used = i <= last
    fresh = (i == 0) | (be_ref[i] != be_ref[jnp.maximum(i - 1, 0)])
    ch = d // LANES
    pitch = _pitch(d)
    n_assign = n_tok * MOE_TOPK
    xbuf = (xbuf0, xbuf1)
    ybuf = (ybuf0, ybuf1)

    def aligned(row):
        return pl.multiple_of(row, 8) if pitch % 8 == 0 else row

    def gather(inv_ref, p):
        for r in range(rb):
            tok = jnp.minimum(lax.shift_right_logical(inv_ref[0, r], 1), n_tok - 1)
            pltpu.make_async_copy(n2_ref.at[pl.ds(aligned(tok * pitch), ch)],
                                  xbuf[p].at[pl.ds(r * pitch, ch)], gsem.at[p]).start(priority=r % 2)

    def gather_wait(p):
        pltpu.make_async_copy(n2_ref.at[pl.ds(0, rb * ch)], xbuf[p].at[pl.ds(0, rb * ch)], gsem.at[p]).wait()

    def scatter(inv_ref, p):
        for r in range(rb):
            pltpu.make_async_copy(ybuf[p].at[pl.ds(r * pitch, pitch)],
                                  y_ref.at[pl.ds(aligned(inv_ref[0, r] * pitch), pitch)],
                                  ssem.at[p]).start(priority=r % 2)

    def scatter_wait(p):
        pltpu.make_async_copy(ybuf[p], y_ref.at[pl.ds(0, rb * pitch)], ssem.at[p]).wait()

    @pl.when(i == 0)
    def _():
        gather(invc_ref, 0)
        for buf in ybuf:
            buf[...] = jnp.zeros(buf.shape, _F32)
        fills = [pltpu.make_async_copy(ybuf[1], y_ref.at[pl.ds((n_assign + k * rb) * pitch, rb * pitch)],
                                       ssem.at[1]) for k in range(N_TRASH_SLOTS)]
        for fill in fills:
            fill.start()
        for fill in fills:
            fill.wait()

    @pl.when(used & fresh)
    def _():
        wg_s[...] = wg_ref[0].astype(_BF16)
        wu_s[...] = wu_ref[0].astype(_BF16)
        wd_s[...] = wd_ref[0].astype(_BF16)

    def step(p):
        @pl.when(i >= 1)
        def _():
            scatter_wait(p)

        gather_wait(p)
        gather(invn_ref, 1 - p)
        scatter(invp_ref, 1 - p)
        xb = _load_token_major(xbuf[p], rb, d).astype(_BF16)
        gate = jnp.dot(xb, wg_s[...], preferred_element_type=_F32)
        up = jnp.dot(xb, wu_s[...], preferred_element_type=_F32)
        hid = (jax.nn.silu(gate) * up).astype(_BF16)
        _store_token_major(ybuf[p], jnp.dot(hid, wd_s[...], preferred_element_type=_F32), zero_pad=False)

        @pl.when(i == last)
        def _():
            scatter(invc_ref, p)
            scatter_wait(p)
            scatter_wait(1 - p)
            gather_wait(1 - p)

    for p in range(2):
        pl.when(used & (lax.rem(i, 2) == p))(functools.partial(step, p))


def _experts(block_expert, n_used, inv3, n2, w_gate, w_up, w_down, *, n_tok):
    nblk, _, rb = inv3.shape
    nblk -= 1
    d, f = w_gate.shape[1:]
    pitch = _pitch(d)

    def inv_spec(shift):
        def index(i, be, nu):
            blk = jnp.clip(i + shift, 0, nu[0] - 1)
            return (jnp.where(i + shift < 0, nblk, blk), 0, 0)
        return pl.BlockSpec((None, 1, rb), index, memory_space=pltpu.SMEM)

    grid_spec = pltpu.PrefetchScalarGridSpec(
        num_scalar_prefetch=2,
        grid=(nblk,),
        in_specs=[inv_spec(-1), inv_spec(0), inv_spec(1),
                  pl.BlockSpec(memory_space=pl.ANY),
                  pl.BlockSpec((1, d, f), lambda i, be, nu: (be[i], 0, 0)),
                  pl.BlockSpec((1, d, f), lambda i, be, nu: (be[i], 0, 0)),
                  pl.BlockSpec((1, f, d), lambda i, be, nu: (be[i], 0, 0))],
        out_specs=pl.BlockSpec(memory_space=pl.ANY),
        scratch_shapes=[pltpu.VMEM((d, f), _BF16), pltpu.VMEM((d, f), _BF16), pltpu.VMEM((f, d), _BF16),
                        *[pltpu.VMEM((rb * pitch, LANES), _F32) for _ in range(4)],
                        pltpu.SemaphoreType.DMA((2,)), pltpu.SemaphoreType.DMA((2,))],
    )
    kern = functools.partial(_expert_kernel, rb=rb, n_tok=n_tok, d=d)
    n_chunks = n_tok * MOE_TOPK + N_TRASH_SLOTS * rb
    return pl.pallas_call(
        kern,
        grid_spec=grid_spec,
        out_shape=jax.ShapeDtypeStruct((n_chunks * pitch, LANES), _F32),
        compiler_params=_params(("arbitrary",)),
        name="experts",
    )(block_expert, n_used, inv3, inv3, inv3, n2, w_gate, w_up, w_down)


def _combine_kernel(h_ref, wc_ref, gf_ref, ng_ref, y_ref, o_ref):
    tk, d = h_ref.shape
    pitch = _pitch(d)
    y = [_load_token_major(y_ref, tk, d, first=k * pitch, pitch=MOE_TOPK * pitch) for k in range(MOE_TOPK)]
    moe = wc_ref[:, 0:1] * y[0] + wc_ref[:, 1:2] * y[1]
    h = h_ref[...] + gf_ref[0] * moe
    o_ref[...] = h * lax.rsqrt(jnp.mean(h * h, axis=-1, keepdims=True) + EPS) * ng_ref[...]


def _combine(h1, w_cols, ada3, norm_g, y_assign, *, seq, tk):
    t, d = h1.shape
    per_b = seq // tk
    assert y_assign.shape[0] % (tk * MOE_TOPK * _pitch(d)) == 0
    return pl.pallas_call(
        _combine_kernel,
        grid=(t // tk,),
        in_specs=[pl.BlockSpec((tk, d), lambda i: (i, 0)),
                  pl.BlockSpec((tk, LANES), lambda i: (i, 0)),
                  pl.BlockSpec((1, 1, d), lambda i: ((i // per_b) * 6 + 5, 0, 0)),
                  pl.BlockSpec((1, d), lambda i: (0, 0)),
                  pl.BlockSpec((tk * MOE_TOPK * _pitch(d), LANES), lambda i: (i, 0))],
        out_specs=pl.BlockSpec((tk, d), lambda i: (i, 0)),
        out_shape=jax.ShapeDtypeStruct((t, d), _F32),
        compiler_params=_params(("arbitrary",)),
        name="combine",
    )(h1, w_cols, ada3, norm_g.reshape(1, d), y_assign)


def _rotary_tables(positions, dh):
    inv_freq = ROPE_THETA ** (-jnp.arange(0, dh, 2, dtype=_F32) / dh)
    ang = positions.astype(_F32)[..., None] * inv_freq
    cos, sin = jnp.cos(ang), jnp.sin(ang)
    t = cos.shape[0] * cos.shape[1]
    cos = jnp.concatenate([cos, cos], axis=-1).reshape(t, dh)
    sin = jnp.concatenate([-sin, sin], axis=-1).reshape(t, dh)
    return cos, sin


def _tile(n, want):
    while n % want:
        want //= 2
    return want


def kernel(x, c, positions, w_ada, b_ada, norm_mix_g, w_in, sgu_ln_g, sgu_ln_b, sgu_w_s, sgu_b_s,
           w_sgu_out, w_moba_out, w_out, norm_ffn_g, w_route_group, b_route_group, w_route_expert,
           b_route_expert, w_exp_gate, w_exp_up, w_exp_down, norm_final_g):
    batch, seq, d = x.shape
    depth = w_ada.shape[0]
    t = batch * seq
    sw = sgu_ln_g.shape[1]
    mw = w_moba_out.shape[1]
    dh = mw // MOBA_HEADS
    groups = w_route_group.shape[2]
    n_exp = w_route_expert.shape[2]
    epg = n_exp // groups
    rb = EXPERT_ROW_BLOCK
    cap = (t * MOE_TOPK // rb + n_exp) * rb
    assert depth == 1, "the final RMSNorm is fused into the single layer's combine"
    assert dh == LANES and seq % MOBA_BLOCK == 0 and sw * 2 == d and mw % sw == 0
    assert groups + n_exp <= ROUTER_ROWS and (t * MOE_TOPK) % rb == 0

    cos, sin = _rotary_tables(positions, dh)
    c_pad = jnp.zeros((8, d), _F32).at[:batch].set(c)
    h = x.reshape(t, d)

    for l in range(depth):
        ada = _ada(c_pad, w_ada[l], b_ada[l], _tile(6 * d, 1024))
        ada3 = ada[:batch].reshape(batch * 6, 1, d)

        proj = _inproj(h, ada3, norm_mix_g[l], w_in[l].astype(_BF16), cos, sin, sgu_ln_g[l], sgu_ln_b[l],
                       seq=seq, sw=sw, mw=mw, dh=dh, tm=_tile(seq, 1024))
        bs_wide = jnp.repeat(sgu_b_s[l].T, sw // sgu_w_s.shape[1], axis=1)
        m_a = _sgu(proj, sgu_w_s[l], bs_wide, w_sgu_out[l].astype(_BF16), sw=sw, mw=mw, tm=_tile(seq, 512))
        attn = _moba(proj, batch=batch, seq=seq, sw=sw, mw=mw, dh=dh, hg=min(MOBA_HEAD_GROUP, MOBA_HEADS))

        wr = jnp.concatenate([w_route_group[l], w_route_expert[l]], axis=1).T
        wr = jnp.zeros((ROUTER_ROWS, d), _F32).at[:groups + n_exp].set(wr)
        wr_hi, wr_lo = _split_bf16(wr)
        br = jnp.concatenate([b_route_group[l], b_route_expert[l].reshape(-1)])
        br = jnp.broadcast_to(jnp.zeros((ROUTER_ROWS,), _F32).at[:groups + n_exp].set(br)[:, None],
                              (ROUTER_ROWS, LANES))
        h1, n2, eid, wt = _mixout(attn, m_a, proj, h, ada3, norm_ffn_g[l], w_moba_out[l].astype(_BF16),
                                  w_out[l].astype(_BF16), wr_hi, wr_lo, br,
                                  seq=seq, sw=sw, mw=mw, groups=groups, epg=epg, tm=_tile(seq, 256))

        dest, tab = _plan(eid, n_exp=n_exp, tc=_tile(t, 512), rb=rb, n_blocks=cap // rb)
        tch = _tile(t, 2048)
        inv = _invert(dest.reshape(MOE_TOPK, t // tch, tch).transpose(1, 0, 2), cap, rb, t * MOE_TOPK)
        y_assign = _experts(tab[0, :cap // rb], tab[1, :1], inv.reshape(cap // rb + 1, 1, rb), n2,
                            w_exp_gate[l], w_exp_up[l], w_exp_down[l], n_tok=t)
        w_cols = jnp.zeros((t, LANES), _F32).at[:, :MOE_TOPK].set(wt.T)
        h = _combine(h1, w_cols, ada3, norm_final_g, y_assign, seq=seq, tk=_tile(seq, 256))

    return h.reshape(batch, seq, d)
```

```python
import functools

import jax
import jax.numpy as jnp
import numpy as np
from jax import lax
from jax.experimental import pallas as pl
from jax.experimental.pallas import tpu as pltpu

MOBA_HEADS = 16
MOBA_BLOCK = 256
MOBA_TOPK = 3
SGU_CHUNK = 128
ROPE_THETA = 10000.0
MOE_GROUPS = 4
MOE_TOPK = 2
EPS = 1e-6
NEG_INF = -1e30
LOG2_E = 1.4426950408889634

LANES = 128
TOKEN_PAD_ROWS = 8
EXPERT_ROW_BLOCK = 256
INPROJ_CHUNK = 256
NORM_BANDS = 4
MOBA_HEAD_GROUP = 4
ROUTER_ROWS = 40
VMEM_LIMIT = 56 * 1024 * 1024

_F32 = jnp.float32
_BF16 = jnp.bfloat16
_NT = (((1,), (1,)), ((), ()))


def _params(semantics, vmem=VMEM_LIMIT):
    return pltpu.CompilerParams(dimension_semantics=semantics, vmem_limit_bytes=vmem)


def _pitch(d):
    return d // LANES + TOKEN_PAD_ROWS


def _store_token_major(ref, val, zero_pad=True):
    n, d = val.shape
    pitch = ref.shape[0] // n
    for c in range(pitch if zero_pad else d // LANES):
        piece = val[:, c * LANES:(c + 1) * LANES] if c < d // LANES else jnp.zeros((n, LANES), val.dtype)
        ref[pl.ds(c, n, stride=pitch), :] = piece


def _load_token_major(ref, n, d, first=0, pitch=None):
    pitch = pitch or ref.shape[0] // n
    return jnp.concatenate([ref[pl.ds(first + c, n, stride=pitch), :] for c in range(d // LANES)], axis=1)


def _split_bf16(a):
    hi = a.astype(_BF16)
    lo = (a - hi.astype(_F32)).astype(_BF16)
    return hi, lo


def _ada_kernel(c_ref, w_ref, b_ref, o_ref):
    ca_hi, ca_lo = _split_bf16(jax.nn.silu(c_ref[...]))
    w_hi, w_lo = _split_bf16(w_ref[...])
    acc = jnp.dot(ca_hi, w_hi, preferred_element_type=_F32)
    acc += jnp.dot(ca_hi, w_lo, preferred_element_type=_F32)
    acc += jnp.dot(ca_lo, w_hi, preferred_element_type=_F32)
    o_ref[...] = acc + b_ref[...]


def _ada(c_pad, w_ada, b_ada, tn):
    rows, d = c_pad.shape
    n = w_ada.shape[1]
    return pl.pallas_call(
        _ada_kernel,
        grid=(n // tn,),
        in_specs=[pl.BlockSpec((rows, d), lambda j: (0, 0)),
                  pl.BlockSpec((d, tn), lambda j: (0, j)),
                  pl.BlockSpec((1, tn), lambda j: (0, j))],
        out_specs=pl.BlockSpec((rows, tn), lambda j: (0, j)),
        out_shape=jax.ShapeDtypeStruct((rows, n), _F32),
        compiler_params=_params(("arbitrary",)),
        name="ada",
    )(c_pad, w_ada, b_ada.reshape(1, n))


def _rms_modulate(x, g, sc, sh):
    y = x * lax.rsqrt(jnp.mean(x * x, axis=-1, keepdims=True) + EPS) * g
    return y * (1.0 + sc) + sh


def _inproj_kernel(x_ref, g_ref, sc_ref, sh_ref, w_ref, cos_ref, sin_ref, lng_ref, lnb_ref,
                   o_ref, n_scr, z_scr, *, q0, k0, v0, dh, scale, cw):
    j = pl.program_id(1)
    tm, tn = o_ref.shape

    def chunk(c):
        return jnp.dot(n_scr[...], w_ref[:, c * cw:(c + 1) * cw], preferred_element_type=_F32)

    @pl.when(j == 0)
    def _():
        band = tm // NORM_BANDS
        for r in range(NORM_BANDS):
            rows = slice(r * band, (r + 1) * band)
            nb = _rms_modulate(x_ref[rows, :], g_ref[...], sc_ref[0], sh_ref[0]).astype(_BF16)
            n_scr[rows, :] = nb
            for c in range(tn // cw):
                cols = slice(c * cw, (c + 1) * cw)
                o_ref[rows, cols] = jax.nn.gelu(
                    jnp.dot(nb, w_ref[:, cols], preferred_element_type=_F32)).astype(_BF16)

    @pl.when(j == 1)
    def _():
        for c in range(tn // cw):
            z_scr[:, c * cw:(c + 1) * cw] = jax.nn.gelu(chunk(c))
        z = z_scr[...]
        zc = z - jnp.mean(z, axis=-1, keepdims=True)
        var = jnp.mean(zc * zc, axis=-1, keepdims=True)
        o_ref[...] = (zc * lax.rsqrt(var + EPS) * lng_ref[...] + lnb_ref[...]).astype(_BF16)

    @pl.when(j >= q0)
    def _():
        mult = jnp.where(j < k0, scale, 1.0)
        cos = jnp.where(j < v0, cos_ref[...], 1.0) * mult
        sin = jnp.where(j < v0, sin_ref[...], 0.0) * mult
        for c in range(tn // cw):
            acc = chunk(c)
            for h in range(cw // dh):
                xh = acc[:, h * dh:(h + 1) * dh]
                rot = xh * cos + pltpu.roll(xh, dh // 2, 1) * sin
                o_ref[:, c * cw + h * dh:c * cw + (h + 1) * dh] = rot.astype(_BF16)


def _inproj(x2, ada3, norm_g, w_in, cos, sin, ln_g, ln_b, *, seq, sw, mw, dh, tm):
    t, d = x2.shape
    n = w_in.shape[1]
    tn = sw
    per_b = seq // tm
    q0 = 2
    k0 = q0 + mw // tn
    v0 = k0 + mw // tn
    cw = _tile(tn, INPROJ_CHUNK)
    assert cw % dh == 0
    kern = functools.partial(_inproj_kernel, q0=q0, k0=k0, v0=v0, dh=dh, scale=dh ** -0.5 * LOG2_E, cw=cw)
    return pl.pallas_call(
        kern,
        grid=(t // tm, n // tn),
        in_specs=[pl.BlockSpec((tm, d), lambda i, j: (i, 0)),
                  pl.BlockSpec((1, d), lambda i, j: (0, 0)),
                  pl.BlockSpec((1, 1, d), lambda i, j: ((i // per_b) * 6 + 1, 0, 0)),
                  pl.BlockSpec((1, 1, d), lambda i, j: ((i // per_b) * 6 + 0, 0, 0)),
                  pl.BlockSpec((d, tn), lambda i, j: (0, j)),
                  pl.BlockSpec((tm, dh), lambda i, j: (i, 0)),
                  pl.BlockSpec((tm, dh), lambda i, j: (i, 0)),
                  pl.BlockSpec((1, sw), lambda i, j: (0, 0)),
                  pl.BlockSpec((1, sw), lambda i, j: (0, 0))],
        out_specs=pl.BlockSpec((tm, tn), lambda i, j: (i, j)),
        out_shape=jax.ShapeDtypeStruct((t, n), _BF16),
        scratch_shapes=[pltpu.VMEM((tm, d), _BF16), pltpu.VMEM((tm, tn), _F32)],
        compiler_params=_params(("arbitrary", "arbitrary")),
        name="inproj",
    )(x2, norm_g.reshape(1, d), ada3, ada3, w_in, cos, sin, ln_g.reshape(1, sw), ln_b.reshape(1, sw))


def _sgu_kernel(u_ref, v_ref, ga_ref, ws_ref, bs_ref, wo_ref, o_ref, gated_scr, *, groups, cg):
    c = SGU_CHUNK
    tri = lax.broadcasted_iota(jnp.int32, (c, c), 0) >= lax.broadcasted_iota(jnp.int32, (c, c), 1)
    for g in range(groups):
        wg = jnp.where(tri, ws_ref[g], 0.0).astype(_BF16)
        cols = slice(g * cg, (g + 1) * cg)
        for ci in range(u_ref.shape[0] // c):
            rows = slice(ci * c, (ci + 1) * c)
            sv = jnp.dot(wg, v_ref[rows, cols], preferred_element_type=_F32) + bs_ref[:, cols]
            gated_scr[rows, cols] = (u_ref[rows, cols].astype(_F32) * sv).astype(_BF16)
    ya = jnp.dot(gated_scr[...], wo_ref[...], preferred_element_type=_F32)
    o_ref[...] = (jax.nn.sigmoid(ga_ref[...].astype(_F32)) * ya).astype(_BF16)


def _sgu(proj, w_s, bs_wide, w_sgu_out, *, sw, mw, tm):
    t = proj.shape[0]
    d = w_sgu_out.shape[1]
    groups = w_s.shape[0]
    ga_blk = (2 * sw + 3 * mw) // d
    kern = functools.partial(_sgu_kernel, groups=groups, cg=sw // groups)
    return pl.pallas_call(
        kern,
        grid=(t // tm,),
        in_specs=[pl.BlockSpec((tm, sw), lambda i: (i, 0)),
                  pl.BlockSpec((tm, sw), lambda i: (i, 1)),
                  pl.BlockSpec((tm, d), lambda i: (i, ga_blk)),
                  pl.BlockSpec(w_s.shape, lambda i: (0, 0, 0)),
                  pl.BlockSpec(bs_wide.shape, lambda i: (0, 0)),
                  pl.BlockSpec(w_sgu_out.shape, lambda i: (0, 0))],
        out_specs=pl.BlockSpec((tm, d), lambda i: (i, 0)),
        out_shape=jax.ShapeDtypeStruct((t, d), _BF16),
        scratch_shapes=[pltpu.VMEM((tm, sw), _BF16)],
        compiler_params=_params(("arbitrary",)),
        name="sgu",
    )(proj, proj, proj, w_s, bs_wide, w_sgu_out)


def _moba_kernel(q_ref, k_ref, v_ref, o_ref, kmh_scr, kml_scr, vt_scr, sel_scr,
                 s_scr, p_scr, acc_scr, m_scr, l_scr, a_scr, *, nb, topk, hg, dh):
    qi = pl.program_id(2)
    blk = MOBA_BLOCK
    tq = q_ref.shape[0]

    @pl.when(qi == 0)
    def _():
        for h in range(hg):
            cols = slice(h * dh, (h + 1) * dh)
            kf = k_ref[:, cols].astype(_F32).reshape(nb, blk, dh)
            km_hi, km_lo = _split_bf16(jnp.mean(kf, axis=1))
            kmh_scr[h * nb:(h + 1) * nb, :] = km_hi
            kml_scr[h * nb:(h + 1) * nb, :] = km_lo
            for c in range(nb):
                rows = slice(c * blk, (c + 1) * blk)
                vt_scr[cols, rows] = v_ref[rows, cols].astype(_F32).T.astype(_BF16)

    row = lax.broadcasted_iota(jnp.int32, (nb, tq), 0)
    past = row < qi
    kpos = lax.broadcasted_iota(jnp.int32, (blk, tq), 0)
    qpos = lax.broadcasted_iota(jnp.int32, (blk, tq), 1)
    own = pl.multiple_of(qi * blk, blk)

    def scores(start, h):
        cols = slice(h * dh, (h + 1) * dh)
        return lax.dot_general(k_ref[pl.ds(start, blk), cols], q_ref[:, cols], _NT,
                               preferred_element_type=_F32)

    def weighted_values(start, h, p):
        return jnp.dot(vt_scr[h * dh:(h + 1) * dh, pl.ds(start, blk)], p, preferred_element_type=_F32)

    for h in range(hg):
        hrows = slice(h * nb, (h + 1) * nb)
        q = q_ref[:, h * dh:(h + 1) * dh]
        gate = (lax.dot_general(kmh_scr[hrows, :], q, _NT, preferred_element_type=_F32)
                + lax.dot_general(kml_scr[hrows, :], q, _NT, preferred_element_type=_F32))
        gm = jnp.where(past, gate, NEG_INF)
        beaten = jnp.zeros((nb, tq), jnp.int32)
        for jp in range(nb):
            other = gm[jp:jp + 1, :]
            wins = (other > gm) | ((other == gm) & (jp < row))
            beaten += wins.astype(jnp.int32)
        sel_scr[hrows, :] = (past & (beaten < topk)).astype(_F32)

        s = jnp.where(kpos <= qpos, scores(own, h), NEG_INF)
        m0 = jnp.max(s, axis=0, keepdims=True)
        p = jnp.exp2(s - m0)
        m_scr[h] = m0
        l_scr[h] = jnp.sum(p, axis=0, keepdims=True)
        a_scr[h] = jnp.ones_like(m0)
        acc_scr[h] = jnp.zeros(acc_scr.shape[1:], _F32)
        p_scr[0, h] = p.astype(_BF16)
        s_scr[0, h] = scores(0, h)

    def body(j, c):
        par = lax.rem(j, 2)
        prev = pl.multiple_of(jnp.where(j == 0, qi, j - 1) * blk, blk)
        nxt = pl.multiple_of(jnp.minimum(j + 1, qi - 1) * blk, blk)
        for h in range(hg):
            acc_scr[h] = a_scr[h] * acc_scr[h] + weighted_values(prev, h, p_scr[par, h])
        for h in range(hg):
            s = jnp.where(sel_scr[pl.ds(h * nb + j, 1), :] > 0.0, s_scr[par, h], NEG_INF)
            m = m_scr[h]
            m_new = jnp.maximum(m, jnp.max(s, axis=0, keepdims=True))
            alpha = jnp.exp2(m - m_new)
            p = jnp.exp2(s - m_new)
            l_scr[h] = alpha * l_scr[h] + jnp.sum(p, axis=0, keepdims=True)
            m_scr[h] = m_new
            a_scr[h] = alpha
            p_scr[1 - par, h] = p.astype(_BF16)
        for h in range(hg):
            s_scr[1 - par, h] = scores(nxt, h)
        return c

    lax.fori_loop(0, qi, body, 0)
    last = pl.multiple_of(jnp.where(qi == 0, qi, qi - 1) * blk, blk)
    for h in range(hg):
        acc = a_scr[h] * acc_scr[h] + weighted_values(last, h, p_scr[lax.rem(qi, 2), h])
        o_ref[:, h * dh:(h + 1) * dh] = (acc / l_scr[h]).T.astype(_BF16)


def _moba(proj, *, batch, seq, sw, mw, dh, hg):
    t = proj.shape[0]
    heads = mw // dh
    nb = seq // MOBA_BLOCK
    tq = MOBA_BLOCK
    nq = seq // tq
    gw = hg * dh
    qc = 2 * sw // gw
    kc = qc + heads // hg
    vc = kc + heads // hg
    kern = functools.partial(_moba_kernel, nb=nb, topk=MOBA_TOPK, hg=hg, dh=dh)
    return pl.pallas_call(
        kern,
        grid=(batch, heads // hg, nq),
        in_specs=[pl.BlockSpec((tq, gw), lambda b, h, i: (b * nq + i, qc + h)),
                  pl.BlockSpec((seq, gw), lambda b, h, i: (b, kc + h)),
                  pl.BlockSpec((seq, gw), lambda b, h, i: (b, vc + h))],
        out_specs=pl.BlockSpec((tq, gw), lambda b, h, i: (b * nq + i, h)),
        out_shape=jax.ShapeDtypeStruct((t, mw), _BF16),
        scratch_shapes=[pltpu.VMEM((hg * nb, dh), _BF16), pltpu.VMEM((hg * nb, dh), _BF16),
                        pltpu.VMEM((gw, seq), _BF16), pltpu.VMEM((hg * nb, tq), _F32),
                        pltpu.VMEM((2, hg, MOBA_BLOCK, tq), _F32), pltpu.VMEM((2, hg, MOBA_BLOCK, tq), _BF16),
                        pltpu.VMEM((hg, dh, tq), _F32), pltpu.VMEM((hg, 1, tq), _F32),
                        pltpu.VMEM((hg, 1, tq), _F32), pltpu.VMEM((hg, 1, tq), _F32)],
        compiler_params=_params(("arbitrary", "arbitrary", "arbitrary")),
        name="moba",
    )(proj, proj, proj)


def _mixout_kernel(attn_ref, ma_ref, gb_ref, x_ref, gm_ref, shf_ref, scf_ref, ng_ref,
                   wmo_ref, wo_ref, wrh_ref, wrl_ref, br_ref,
                   h_ref, n2_ref, eid_ref, wt_ref, *, groups, epg):
    yb = jnp.dot(attn_ref[...], wmo_ref[...], preferred_element_type=_F32)
    merged = ma_ref[...].astype(_F32) + jax.nn.sigmoid(gb_ref[...].astype(_F32)) * yb
    mix = jnp.dot(merged.astype(_BF16), wo_ref[...], preferred_element_type=_F32)
    h = x_ref[...] + gm_ref[0] * mix
    h_ref[...] = h
    n2 = _rms_modulate(h, ng_ref[...], scf_ref[0], shf_ref[0])
    _store_token_major(n2_ref, n2)

    n_hi, n_lo = _split_bf16(n2)
    lg = (jnp.dot(n_hi, wrh_ref[...], preferred_element_type=_F32)
          + jnp.dot(n_lo, wrh_ref[...], preferred_element_type=_F32)
          + jnp.dot(n_hi, wrl_ref[...], preferred_element_type=_F32))
    lg = lg.T[:ROUTER_ROWS, :] + br_ref[:, 0:1]

    gl = [lg[g:g + 1, :] for g in range(groups)]
    gmax = functools.reduce(jnp.maximum, gl)
    denom = functools.reduce(jnp.add, [jnp.exp(v - gmax) for v in gl])
    p_group = 1.0 / denom
    g_sel = jnp.full(gmax.shape, groups - 1, jnp.int32)
    for g in range(groups - 2, -1, -1):
        g_sel = jnp.where(gl[g] == gmax, g, g_sel)

    el = []
    for e in range(epg):
        v = lg[groups + e:groups + e + 1, :]
        for g in range(1, groups):
            r = groups + g * epg + e
            v = jnp.where(g_sel == g, lg[r:r + 1, :], v)
        el.append(v)

    def top1(vals):
        vmax = functools.reduce(jnp.maximum, vals)
        idx = jnp.full(vmax.shape, epg - 1, jnp.int32)
        for e in range(epg - 2, -1, -1):
            idx = jnp.where(vals[e] == vmax, e, idx)
        return vmax, idx

    v1, i1 = top1(el)
    v2, i2 = top1([jnp.where(i1 == e, -jnp.inf, el[e]) for e in range(epg)])
    b = jnp.exp(v2 - v1)
    eid_ref[0:1, :] = g_sel * epg + i1
    eid_ref[1:2, :] = g_sel * epg + i2
    wt_ref[0:1, :] = (1.0 / (1.0 + b)) * p_group
    wt_ref[1:2, :] = (b / (1.0 + b)) * p_group


def _mixout(attn, m_a, proj, x2, ada3, norm_g, w_moba_out, w_out, wr_hi, wr_lo, br,
            *, seq, sw, mw, groups, epg, tm):
    t, d = x2.shape
    per_b = seq // tm
    gb_blk = (2 * sw + 3 * mw) // d + 1
    once = pl.Buffered(1)
    kern = functools.partial(_mixout_kernel, groups=groups, epg=epg)

    def ada_spec(k):
        return pl.BlockSpec((1, 1, d), lambda i: ((i // per_b) * 6 + k, 0, 0))

    return pl.pallas_call(
        kern,
        grid=(t // tm,),
        in_specs=[pl.BlockSpec((tm, mw), lambda i: (i, 0)),
                  pl.BlockSpec((tm, d), lambda i: (i, 0)),
                  pl.BlockSpec((tm, d), lambda i: (i, gb_blk)),
                  pl.BlockSpec((tm, d), lambda i: (i, 0)),
                  ada_spec(2), ada_spec(3), ada_spec(4),
                  pl.BlockSpec((1, d), lambda i: (0, 0)),
                  pl.BlockSpec((mw, d), lambda i: (0, 0), pipeline_mode=once),
                  pl.BlockSpec((d, d), lambda i: (0, 0), pipeline_mode=once),
                  pl.BlockSpec((d, LANES), lambda i: (0, 0)),
                  pl.BlockSpec((d, LANES), lambda i: (0, 0)),
                  pl.BlockSpec((ROUTER_ROWS, LANES), lambda i: (0, 0))],
        out_specs=[pl.BlockSpec((tm, d), lambda i: (i, 0)),
                   pl.BlockSpec((tm * _pitch(d), LANES), lambda i: (i, 0)),
                   pl.BlockSpec((MOE_TOPK, tm), lambda i: (0, i)),
                   pl.BlockSpec((MOE_TOPK, tm), lambda i: (0, i))],
        out_shape=[jax.ShapeDtypeStruct((t, d), _F32),
                   jax.ShapeDtypeStruct((t * _pitch(d), LANES), _F32),
                   jax.ShapeDtypeStruct((MOE_TOPK, t), jnp.int32),
                   jax.ShapeDtypeStruct((MOE_TOPK, t), _F32)],
        compiler_params=_params(("arbitrary",)),
        name="mixout",
    )(attn, m_a, proj, x2, ada3, ada3, ada3, norm_g.reshape(1, d), w_moba_out, w_out, wr_hi, wr_lo, br)


def _plan_kernel(eid_ref, dest_ref, tab_ref, carry_scr, start_scr, tri_scr, *, n_exp, rb):
    phase = pl.program_id(0)
    first = pl.program_id(1) == 0
    tc = eid_ref.shape[1]
    nbp = tab_ref.shape[1]

    @pl.when((phase == 0) & first)
    def _():
        carry_scr[...] = jnp.zeros_like(carry_scr)
        earlier = lax.broadcasted_iota(jnp.int32, (tc, tc), 0) < lax.broadcasted_iota(jnp.int32, (tc, tc), 1)
        tri_scr[...] = earlier.astype(_BF16)

    eio = lax.broadcasted_iota(jnp.int32, (n_exp, tc), 0)
    oh0 = (eio == eid_ref[0:1, :]).astype(_F32)
    oh1 = (eio == eid_ref[1:2, :]).astype(_F32)
    oh = oh0 + oh1

    @pl.when((phase == 1) & first)
    def _():
        counts = carry_scr[...].astype(jnp.int32)
        blocks = lax.shift_right_logical(counts + (rb - 1), rb.bit_length() - 1).astype(_F32)
        upto = lax.broadcasted_iota(jnp.int32, (n_exp, n_exp), 0) >= lax.broadcasted_iota(jnp.int32, (n_exp, n_exp), 1)
        end_blk = jnp.dot(upto.astype(_BF16), blocks.astype(_BF16), preferred_element_type=_F32)
        start_scr[...] = (end_blk - blocks) * rb
        carry_scr[...] = jnp.zeros_like(carry_scr)
        blk_id = lax.broadcasted_iota(jnp.int32, (n_exp, nbp), 1).astype(_F32)
        owner = jnp.sum((end_blk[:, 0:1] <= blk_id).astype(_F32), axis=0, keepdims=True)
        owner = jnp.minimum(owner, n_exp - 1.0)
        used = jnp.broadcast_to(end_blk[n_exp - 1:n_exp, 0:1], (1, nbp))
        trow = lax.broadcasted_iota(jnp.int32, tab_ref.shape, 0)
        tab_ref[...] = jnp.where(trow == 0, owner, jnp.where(trow == 1, used, 0.0)).astype(jnp.int32)

    @pl.when(phase == 1)
    def _():
        before = jnp.dot(oh.astype(_BF16), tri_scr[...], preferred_element_type=_F32)
        base = start_scr[:, 0:1] + carry_scr[:, 0:1] + before
        dest_ref[0:1, :] = jnp.sum(oh0 * base, axis=0, keepdims=True).astype(jnp.int32)
        dest_ref[1:2, :] = jnp.sum(oh1 * base, axis=0, keepdims=True).astype(jnp.int32)

    carry_scr[...] = carry_scr[...] + jnp.sum(oh, axis=1, keepdims=True)


def _plan(eid, *, n_exp, tc, rb, n_blocks):
    t = eid.shape[1]
    assert rb & (rb - 1) == 0 and t * MOE_TOPK // rb + n_exp < 256
    nbp = -(-n_blocks // LANES) * LANES
    kern = functools.partial(_plan_kernel, n_exp=n_exp, rb=rb)
    return pl.pallas_call(
        kern,
        grid=(2, t // tc),
        in_specs=[pl.BlockSpec((MOE_TOPK, tc), lambda p, i: (0, i))],
        out_specs=[pl.BlockSpec((MOE_TOPK, tc), lambda p, i: (0, i * p)),
                   pl.BlockSpec((8, nbp), lambda p, i: (0, 0))],
        out_shape=[jax.ShapeDtypeStruct((MOE_TOPK, t), jnp.int32),
                   jax.ShapeDtypeStruct((8, nbp), jnp.int32)],
        scratch_shapes=[pltpu.VMEM((n_exp, LANES), _F32), pltpu.VMEM((n_exp, LANES), _F32),
                        pltpu.VMEM((tc, tc), _BF16)],
        compiler_params=_params(("arbitrary", "arbitrary")),
        name="plan",
    )(eid)


def _invert_kernel(dest_ref, pad_ref, inv_ref, sem):
    i = pl.program_id(0)
    tch = dest_ref.shape[1]

    @pl.when(i == 0)
    def _():
        fill = pltpu.make_async_copy(pad_ref, inv_ref, sem)
        fill.start()
        fill.wait()

    def body(t, c):
        a = (i * tch + t) * MOE_TOPK
        for k in range(MOE_TOPK):
            inv_ref[dest_ref[k, t]] = a + k
        return c

    lax.fori_loop(0, tch, body, 0, unroll=8)


def _pad_targets(cap, rb, n_assign):
    r = np.arange(cap + rb)
    blk = r // rb
    slot = np.where(blk == cap // rb, 3, blk % 3)
    return (n_assign + slot * rb + r % rb).astype(np.int32)


def _invert(dest3, cap, rb, n_assign):
    steps, _, tch = dest3.shape
    pad = _pad_targets(cap, rb, n_assign)
    return pl.pallas_call(
        _invert_kernel,
        grid=(steps,),
        in_specs=[pl.BlockSpec((None, MOE_TOPK, tch), lambda i: (i, 0, 0), memory_space=pltpu.SMEM),
                  pl.BlockSpec(memory_space=pl.ANY)],
        out_specs=pl.BlockSpec(pad.shape, lambda i: (0,), memory_space=pltpu.SMEM),
        out_shape=jax.ShapeDtypeStruct(pad.shape, jnp.int32),
        scratch_shapes=[pltpu.SemaphoreType.DMA(())],
        compiler_params=_params(("arbitrary",)),
        name="invert",
    )(dest3, jnp.asarray(pad))


N_TRASH_SLOTS = 4


def _expert_kernel(be_ref, nu_ref, invp_ref, invc_ref, invn_ref, n2_ref, wg_ref, wu_ref, wd_ref, y_ref,
                   wg_s, wu_s, wd_s, xbuf0, xbuf1, ybuf0, ybuf1, gsem, ssem, *, rb, n_tok, d):
    i = pl.program_id(0)
    last = nu_ref[0] - 1
    used = i <= last
    fresh = (i == 0) | (be_ref[i] != be_ref[jnp.maximum(i - 1, 0)])
    ch = d // LANES
    pitch = _pitch(d)
    n_assign = n_tok * MOE_TOPK
    xbuf = (xbuf0, xbuf1)
    ybuf = (ybuf0, ybuf1)

    def aligned(row):
        return pl.multiple_of(row, 8) if pitch % 8 == 0 else row

    def gather(inv_ref, p):
        for r in range(rb):
            tok = jnp.minimum(lax.shift_right_logical(inv_ref[0, r], 1), n_tok - 1)
            pltpu.make_async_copy(n2_ref.at[pl.ds(aligned(tok * pitch), ch)],
                                  xbuf[p].at[pl.ds(r * pitch, ch)], gsem.at[p]).start(priority=r % 2)

    def gather_wait(p):
        pltpu.make_async_copy(n2_ref.at[pl.ds(0, rb * ch)], xbuf[p].at[pl.ds(0, rb * ch)], gsem.at[p]).wait()

    def scatter(inv_ref, p):
        for r in range(rb):
            pltpu.make_async_copy(ybuf[p].at[pl.ds(r * pitch, pitch)],
                                  y_ref.at[pl.ds(aligned(inv_ref[0, r] * pitch), pitch)],
                                  ssem.at[p]).start(priority=r % 2)

    def scatter_wait(p):
        pltpu.make_async_copy(ybuf[p], y_ref.at[pl.ds(0, rb * pitch)], ssem.at[p]).wait()

    @pl.when(i == 0)
    def _():
        gather(invc_ref, 0)
        for buf in ybuf:
            buf[...] = jnp.zeros(buf.shape, _F32)
        fills = [pltpu.make_async_copy(ybuf[1], y_ref.at[pl.ds((n_assign + k * rb) * pitch, rb * pitch)],
                                       ssem.at[1]) for k in range(N_TRASH_SLOTS)]
        for fill in fills:
            fill.start()
        for fill in fills:
            fill.wait()

    @pl.when(used & fresh)
    def _():
        wg_s[...] = wg_ref[0].astype(_BF16)
        wu_s[...] = wu_ref[0].astype(_BF16)
        wd_s[...] = wd_ref[0].astype(_BF16)

    def step(p):
        @pl.when(i >= 1)
        def _():
            scatter_wait(p)

        gather_wait(p)
        gather(invn_ref, 1 - p)
        scatter(invp_ref, 1 - p)
        xb = _load_token_major(xbuf[p], rb, d).astype(_BF16)
        gate = jnp.dot(xb, wg_s[...], preferred_element_type=_F32)
        up = jnp.dot(xb, wu_s[...], preferred_element_type=_F32)
        hid = (jax.nn.silu(gate) * up).astype(_BF16)
        _store_token_major(ybuf[p], jnp.dot(hid, wd_s[...], preferred_element_type=_F32), zero_pad=False)

        @pl.when(i == last)
        def _():
            scatter(invc_ref, p)
            scatter_wait(p)
            scatter_wait(1 - p)
            gather_wait(1 - p)

    for p in range(2):
        pl.when(used & (lax.rem(i, 2) == p))(functools.partial(step, p))


def _experts(block_expert, n_used, inv3, n2, w_gate, w_up, w_down, *, n_tok):
    nblk, _, rb = inv3.shape
    nblk -= 1
    d, f = w_gate.shape[1:]
    pitch = _pitch(d)

    def inv_spec(shift):
        def index(i, be, nu):
            blk = jnp.clip(i + shift, 0, nu[0] - 1)
            return (jnp.where(i + shift < 0, nblk, blk), 0, 0)
        return pl.BlockSpec((None, 1, rb), index, memory_space=pltpu.SMEM)

    grid_spec = pltpu.PrefetchScalarGridSpec(
        num_scalar_prefetch=2,
        grid=(nblk,),
        in_specs=[inv_spec(-1), inv_spec(0), inv_spec(1),
                  pl.BlockSpec(memory_space=pl.ANY),
                  pl.BlockSpec((1, d, f), lambda i, be, nu: (be[i], 0, 0)),
                  pl.BlockSpec((1, d, f), lambda i, be, nu: (be[i], 0, 0)),
                  pl.BlockSpec((1, f, d), lambda i, be, nu: (be[i], 0, 0))],
        out_specs=pl.BlockSpec(memory_space=pl.ANY),
        scratch_shapes=[pltpu.VMEM((d, f), _BF16), pltpu.VMEM((d, f), _BF16), pltpu.VMEM((f, d), _BF16),
                        *[pltpu.VMEM((rb * pitch, LANES), _F32) for _ in range(4)],
                        pltpu.SemaphoreType.DMA((2,)), pltpu.SemaphoreType.DMA((2,))],
    )
    kern = functools.partial(_expert_kernel, rb=rb, n_tok=n_tok, d=d)
    n_chunks = n_tok * MOE_TOPK + N_TRASH_SLOTS * rb
    return pl.pallas_call(
        kern,
        grid_spec=grid_spec,
        out_shape=jax.ShapeDtypeStruct((n_chunks * pitch, LANES), _F32),
        compiler_params=_params(("arbitrary",)),
        name="experts",
    )(block_expert, n_used, inv3, inv3, inv3, n2, w_gate, w_up, w_down)


def _combine_kernel(h_ref, wc_ref, gf_ref, ng_ref, y_ref, o_ref):
    tk, d = h_ref.shape
    pitch = _pitch(d)
    y = [_load_token_major(y_ref, tk, d, first=k * pitch, pitch=MOE_TOPK * pitch) for k in range(MOE_TOPK)]
    moe = wc_ref[:, 0:1] * y[0] + wc_ref[:, 1:2] * y[1]
    h = h_ref[...] + gf_ref[0] * moe
    o_ref[...] = h * lax.rsqrt(jnp.mean(h * h, axis=-1, keepdims=True) + EPS) * ng_ref[...]


def _combine(h1, w_cols, ada3, norm_g, y_assign, *, seq, tk):
    t, d = h1.shape
    per_b = seq // tk
    assert y_assign.shape[0] % (tk * MOE_TOPK * _pitch(d)) == 0
    return pl.pallas_call(
        _combine_kernel,
        grid=(t // tk,),
        in_specs=[pl.BlockSpec((tk, d), lambda i: (i, 0)),
                  pl.BlockSpec((tk, LANES), lambda i: (i, 0)),
                  pl.BlockSpec((1, 1, d), lambda i: ((i // per_b) * 6 + 5, 0, 0)),
                  pl.BlockSpec((1, d), lambda i: (0, 0)),
                  pl.BlockSpec((tk * MOE_TOPK * _pitch(d), LANES), lambda i: (i, 0))],
        out_specs=pl.BlockSpec((tk, d), lambda i: (i, 0)),
        out_shape=jax.ShapeDtypeStruct((t, d), _F32),
        compiler_params=_params(("arbitrary",)),
        name="combine",
    )(h1, w_cols, ada3, norm_g.reshape(1, d), y_assign)


def _rotary_tables(positions, dh):
    inv_freq = ROPE_THETA ** (-jnp.arange(0, dh, 2, dtype=_F32) / dh)
    ang = positions.astype(_F32)[..., None] * inv_freq
    cos, sin = jnp.cos(ang), jnp.sin(ang)
    t = cos.shape[0] * cos.shape[1]
    cos = jnp.concatenate([cos, cos], axis=-1).reshape(t, dh)
    sin = jnp.concatenate([-sin, sin], axis=-1).reshape(t, dh)
    return cos, sin


def _tile(n, want):
    while n % want:
        want //= 2
    return want


def kernel(x, c, positions, w_ada, b_ada, norm_mix_g, w_in, sgu_ln_g, sgu_ln_b, sgu_w_s, sgu_b_s,
           w_sgu_out, w_moba_out, w_out, norm_ffn_g, w_route_group, b_route_group, w_route_expert,
           b_route_expert, w_exp_gate, w_exp_up, w_exp_down, norm_final_g):
    batch, seq, d = x.shape
    depth = w_ada.shape[0]
    t = batch * seq
    sw = sgu_ln_g.shape[1]
    mw = w_moba_out.shape[1]
    dh = mw // MOBA_HEADS
    groups = w_route_group.shape[2]
    n_exp = w_route_expert.shape[2]
    epg = n_exp // groups
    rb = EXPERT_ROW_BLOCK
    cap = (t * MOE_TOPK // rb + n_exp) * rb
    assert depth == 1, "the final RMSNorm is fused into the single layer's combine"
    assert dh == LANES and seq % MOBA_BLOCK == 0 and sw * 2 == d and mw % sw == 0
    assert groups + n_exp <= ROUTER_ROWS and (t * MOE_TOPK) % rb == 0

    cos, sin = _rotary_tables(positions, dh)
    c_pad = jnp.zeros((8, d), _F32).at[:batch].set(c)
    h = x.reshape(t, d)

    for l in range(depth):
        ada = _ada(c_pad, w_ada[l], b_ada[l], _tile(6 * d, 1024))
        ada3 = ada[:batch].reshape(batch * 6, 1, d)

        proj = _inproj(h, ada3, norm_mix_g[l], w_in[l].astype(_BF16), cos, sin, sgu_ln_g[l], sgu_ln_b[l],
                       seq=seq, sw=sw, mw=mw, dh=dh, tm=_tile(seq, 1024))
        bs_wide = jnp.repeat(sgu_b_s[l].T, sw // sgu_w_s.shape[1], axis=1)
        m_a = _sgu(proj, sgu_w_s[l], bs_wide, w_sgu_out[l].astype(_BF16), sw=sw, mw=mw, tm=_tile(seq, 512))
        attn = _moba(proj, batch=batch, seq=seq, sw=sw, mw=mw, dh=dh, hg=min(MOBA_HEAD_GROUP, MOBA_HEADS))

        wr = jnp.concatenate([w_route_group[l], w_route_expert[l]], axis=1)
        wr_hi, wr_lo = _split_bf16(jnp.zeros((d, LANES), _F32).at[:, :groups + n_exp].set(wr))
        br = jnp.concatenate([b_route_group[l], b_route_expert[l].reshape(-1)])
        br = jnp.broadcast_to(jnp.zeros((ROUTER_ROWS,), _F32).at[:groups + n_exp].set(br)[:, None],
                              (ROUTER_ROWS, LANES))
        h1, n2, eid, wt = _mixout(attn, m_a, proj, h, ada3, norm_ffn_g[l], w_moba_out[l].astype(_BF16),
                                  w_out[l].astype(_BF16), wr_hi, wr_lo, br,
                                  seq=seq, sw=sw, mw=mw, groups=groups, epg=epg, tm=_tile(seq, 256))

        dest, tab = _plan(eid, n_exp=n_exp, tc=_tile(t, 512), rb=rb, n_blocks=cap // rb)
        tch = _tile(t, 2048)
        inv = _invert(dest.reshape(MOE_TOPK, t // tch, tch).transpose(1, 0, 2), cap, rb, t * MOE_TOPK)
        y_assign = _experts(tab[0, :cap // rb], tab[1, :1], inv.reshape(cap // rb + 1, 1, rb), n2,
                            w_exp_gate[l], w_exp_up[l], w_exp_down[l], n_tok=t)
        w_cols = jnp.zeros((t, LANES), _F32).at[:, :MOE_TOPK].set(wt.T)
        h = _combine(h1, w_cols, ada3, norm_final_g, y_assign, seq=seq, tk=_tile(seq, 256))

    return h.reshape(batch, seq, d)
```

```python
import functools

import jax
import jax.numpy as jnp
import numpy as np
from jax import lax
from jax.experimental import pallas as pl
from jax.experimental.pallas import tpu as pltpu

MOBA_HEADS = 16
MOBA_BLOCK = 256
MOBA_TOPK = 3
SGU_CHUNK = 128
ROPE_THETA = 10000.0
MOE_GROUPS = 4
MOE_TOPK = 2
EPS = 1e-6
NEG_INF = -1e30
LOG2_E = 1.4426950408889634

LANES = 128
TOKEN_PAD_ROWS = 8
EXPERT_ROW_BLOCK = 256
INPROJ_CHUNK = 256
NORM_BANDS = 4
MOBA_HEAD_GROUP = 4
ROUTER_ROWS = 40
VMEM_LIMIT = 56 * 1024 * 1024

_F32 = jnp.float32
_BF16 = jnp.bfloat16
_NT = (((1,), (1,)), ((), ()))


def _params(semantics, vmem=VMEM_LIMIT):
    return pltpu.CompilerParams(dimension_semantics=semantics, vmem_limit_bytes=vmem)


def _pitch(d):
    return d // LANES + TOKEN_PAD_ROWS


def _store_token_major(ref, val, zero_pad=True):
    n, d = val.shape
    pitch = ref.shape[0] // n
    for c in range(pitch if zero_pad else d // LANES):
        piece = val[:, c * LANES:(c + 1) * LANES] if c < d // LANES else jnp.zeros((n, LANES), val.dtype)
        ref[pl.ds(c, n, stride=pitch), :] = piece


def _load_token_major(ref, n, d, first=0, pitch=None):
    pitch = pitch or ref.shape[0] // n
    return jnp.concatenate([ref[pl.ds(first + c, n, stride=pitch), :] for c in range(d // LANES)], axis=1)


def _split_bf16(a):
    hi = a.astype(_BF16)
    lo = (a - hi.astype(_F32)).astype(_BF16)
    return hi, lo


def _ada_kernel(c_ref, w_ref, b_ref, o_ref):
    ca_hi, ca_lo = _split_bf16(jax.nn.silu(c_ref[...]))
    w_hi, w_lo = _split_bf16(w_ref[...])
    acc = jnp.dot(ca_hi, w_hi, preferred_element_type=_F32)
    acc += jnp.dot(ca_hi, w_lo, preferred_element_type=_F32)
    acc += jnp.dot(ca_lo, w_hi, preferred_element_type=_F32)
    o_ref[...] = acc + b_ref[...]


def _ada(c_pad, w_ada, b_ada, tn):
    rows, d = c_pad.shape
    n = w_ada.shape[1]
    return pl.pallas_call(
        _ada_kernel,
        grid=(n // tn,),
        in_specs=[pl.BlockSpec((rows, d), lambda j: (0, 0)),
                  pl.BlockSpec((d, tn), lambda j: (0, j)),
                  pl.BlockSpec((1, tn), lambda j: (0, j))],
        out_specs=pl.BlockSpec((rows, tn), lambda j: (0, j)),
        out_shape=jax.ShapeDtypeStruct((rows, n), _F32),
        compiler_params=_params(("arbitrary",)),
        name="ada",
    )(c_pad, w_ada, b_ada.reshape(1, n))


def _rms_modulate(x, g, sc, sh):
    y = x * lax.rsqrt(jnp.mean(x * x, axis=-1, keepdims=True) + EPS) * g
    return y * (1.0 + sc) + sh


def _inproj_kernel(x_ref, g_ref, sc_ref, sh_ref, w_ref, cos_ref, sin_ref, lng_ref, lnb_ref,
                   o_ref, n_scr, z_scr, *, q0, k0, v0, dh, scale, cw):
    j = pl.program_id(1)
    tm, tn = o_ref.shape

    def chunk(c):
        return jnp.dot(n_scr[...], w_ref[:, c * cw:(c + 1) * cw], preferred_element_type=_F32)

    @pl.when(j == 0)
    def _():
        band = tm // NORM_BANDS
        for r in range(NORM_BANDS):
            rows = slice(r * band, (r + 1) * band)
            nb = _rms_modulate(x_ref[rows, :], g_ref[...], sc_ref[0], sh_ref[0]).astype(_BF16)
            n_scr[rows, :] = nb
            for c in range(tn // cw):
                cols = slice(c * cw, (c + 1) * cw)
                o_ref[rows, cols] = jax.nn.gelu(
                    jnp.dot(nb, w_ref[:, cols], preferred_element_type=_F32)).astype(_BF16)

    @pl.when(j == 1)
    def _():
        for c in range(tn // cw):
            z_scr[:, c * cw:(c + 1) * cw] = jax.nn.gelu(chunk(c))
        z = z_scr[...]
        zc = z - jnp.mean(z, axis=-1, keepdims=True)
        var = jnp.mean(zc * zc, axis=-1, keepdims=True)
        o_ref[...] = (zc * lax.rsqrt(var + EPS) * lng_ref[...] + lnb_ref[...]).astype(_BF16)

    @pl.when(j >= q0)
    def _():
        mult = jnp.where(j < k0, scale, 1.0)
        cos = jnp.where(j < v0, cos_ref[...], 1.0) * mult
        sin = jnp.where(j < v0, sin_ref[...], 0.0) * mult
        for c in range(tn // cw):
            acc = chunk(c)
            for h in range(cw // dh):
                xh = acc[:, h * dh:(h + 1) * dh]
                rot = xh * cos + pltpu.roll(xh, dh // 2, 1) * sin
                o_ref[:, c * cw + h * dh:c * cw + (h + 1) * dh] = rot.astype(_BF16)


def _inproj(x2, ada3, norm_g, w_in, cos, sin, ln_g, ln_b, *, seq, sw, mw, dh, tm):
    t, d = x2.shape
    n = w_in.shape[1]
    tn = sw
    per_b = seq // tm
    q0 = 2
    k0 = q0 + mw // tn
    v0 = k0 + mw // tn
    cw = _tile(tn, INPROJ_CHUNK)
    assert cw % dh == 0
    kern = functools.partial(_inproj_kernel, q0=q0, k0=k0, v0=v0, dh=dh, scale=dh ** -0.5 * LOG2_E, cw=cw)
    return pl.pallas_call(
        kern,
        grid=(t // tm, n // tn),
        in_specs=[pl.BlockSpec((tm, d), lambda i, j: (i, 0)),
                  pl.BlockSpec((1, d), lambda i, j: (0, 0)),
                  pl.BlockSpec((1, 1, d), lambda i, j: ((i // per_b) * 6 + 1, 0, 0)),
                  pl.BlockSpec((1, 1, d), lambda i, j: ((i // per_b) * 6 + 0, 0, 0)),
                  pl.BlockSpec((d, tn), lambda i, j: (0, j)),
                  pl.BlockSpec((tm, dh), lambda i, j: (i, 0)),
                  pl.BlockSpec((tm, dh), lambda i, j: (i, 0)),
                  pl.BlockSpec((1, sw), lambda i, j: (0, 0)),
                  pl.BlockSpec((1, sw), lambda i, j: (0, 0))],
        out_specs=pl.BlockSpec((tm, tn), lambda i, j: (i, j)),
        out_shape=jax.ShapeDtypeStruct((t, n), _BF16),
        scratch_shapes=[pltpu.VMEM((tm, d), _BF16), pltpu.VMEM((tm, tn), _F32)],
        compiler_params=_params(("arbitrary", "arbitrary")),
        name="inproj",
    )(x2, norm_g.reshape(1, d), ada3, ada3, w_in, cos, sin, ln_g.reshape(1, sw), ln_b.reshape(1, sw))


def _sgu_kernel(u_ref, v_ref, ga_ref, ws_ref, bs_ref, wo_ref, o_ref, gated_scr, *, groups, cg):
    c = SGU_CHUNK
    tri = lax.broadcasted_iota(jnp.int32, (c, c), 0) >= lax.broadcasted_iota(jnp.int32, (c, c), 1)
    for g in range(groups):
        wg = jnp.where(tri, ws_ref[g], 0.0).astype(_BF16)
        cols = slice(g * cg, (g + 1) * cg)
        for ci in range(u_ref.shape[0] // c):
            rows = slice(ci * c, (ci + 1) * c)
            sv = jnp.dot(wg, v_ref[rows, cols], preferred_element_type=_F32) + bs_ref[:, cols]
            gated_scr[rows, cols] = (u_ref[rows, cols].astype(_F32) * sv).astype(_BF16)
    ya = jnp.dot(gated_scr[...], wo_ref[...], preferred_element_type=_F32)
    o_ref[...] = (jax.nn.sigmoid(ga_ref[...].astype(_F32)) * ya).astype(_BF16)


def _sgu(proj, w_s, bs_wide, w_sgu_out, *, sw, mw, tm):
    t = proj.shape[0]
    d = w_sgu_out.shape[1]
    groups = w_s.shape[0]
    ga_blk = (2 * sw + 3 * mw) // d
    kern = functools.partial(_sgu_kernel, groups=groups, cg=sw // groups)
    return pl.pallas_call(
        kern,
        grid=(t // tm,),
        in_specs=[pl.BlockSpec((tm, sw), lambda i: (i, 0)),
                  pl.BlockSpec((tm, sw), lambda i: (i, 1)),
                  pl.BlockSpec((tm, d), lambda i: (i, ga_blk)),
                  pl.BlockSpec(w_s.shape, lambda i: (0, 0, 0)),
                  pl.BlockSpec(bs_wide.shape, lambda i: (0, 0)),
                  pl.BlockSpec(w_sgu_out.shape, lambda i: (0, 0))],
        out_specs=pl.BlockSpec((tm, d), lambda i: (i, 0)),
        out_shape=jax.ShapeDtypeStruct((t, d), _BF16),
        scratch_shapes=[pltpu.VMEM((tm, sw), _BF16)],
        compiler_params=_params(("arbitrary",)),
        name="sgu",
    )(proj, proj, proj, w_s, bs_wide, w_sgu_out)


def _moba_kernel(q_ref, k_ref, v_ref, o_ref, kmh_scr, kml_scr, vt_scr, sel_scr,
                 s_scr, p_scr, acc_scr, m_scr, l_scr, a_scr, *, nb, topk, hg, dh):
    qi = pl.program_id(2)
    blk = MOBA_BLOCK
    tq = q_ref.shape[0]

    @pl.when(qi == 0)
    def _():
        for h in range(hg):
            cols = slice(h * dh, (h + 1) * dh)
            kf = k_ref[:, cols].astype(_F32).reshape(nb, blk, dh)
            km_hi, km_lo = _split_bf16(jnp.mean(kf, axis=1))
            kmh_scr[h * nb:(h + 1) * nb, :] = km_hi
            kml_scr[h * nb:(h + 1) * nb, :] = km_lo
            for c in range(nb):
                rows = slice(c * blk, (c + 1) * blk)
                vt_scr[cols, rows] = v_ref[rows, cols].astype(_F32).T.astype(_BF16)

    row = lax.broadcasted_iota(jnp.int32, (nb, tq), 0)
    past = row < qi
    kpos = lax.broadcasted_iota(jnp.int32, (blk, tq), 0)
    qpos = lax.broadcasted_iota(jnp.int32, (blk, tq), 1)
    own = pl.multiple_of(qi * blk, blk)

    def scores(start, h):
        cols = slice(h * dh, (h + 1) * dh)
        return lax.dot_general(k_ref[pl.ds(start, blk), cols], q_ref[:, cols], _NT,
                               preferred_element_type=_F32)

    def weighted_values(start, h, p):
        return jnp.dot(vt_scr[h * dh:(h + 1) * dh, pl.ds(start, blk)], p, preferred_element_type=_F32)

    for h in range(hg):
        hrows = slice(h * nb, (h + 1) * nb)
        q = q_ref[:, h * dh:(h + 1) * dh]
        gate = (lax.dot_general(kmh_scr[hrows, :], q, _NT, preferred_element_type=_F32)
                + lax.dot_general(kml_scr[hrows, :], q, _NT, preferred_element_type=_F32))
        gm = jnp.where(past, gate, NEG_INF)
        beaten = jnp.zeros((nb, tq), jnp.int32)
        for jp in range(nb):
            other = gm[jp:jp + 1, :]
            wins = (other > gm) | ((other == gm) & (jp < row))
            beaten += wins.astype(jnp.int32)
        sel_scr[hrows, :] = (past & (beaten < topk)).astype(_F32)

        s = jnp.where(kpos <= qpos, scores(own, h), NEG_INF)
        m0 = jnp.max(s, axis=0, keepdims=True)
        p = jnp.exp2(s - m0)
        m_scr[h] = m0
        l_scr[h] = jnp.sum(p, axis=0, keepdims=True)
        a_scr[h] = jnp.ones_like(m0)
        acc_scr[h] = jnp.zeros(acc_scr.shape[1:], _F32)
        p_scr[0, h] = p.astype(_BF16)
        s_scr[0, h] = scores(0, h)

    def body(j, c):
        par = lax.rem(j, 2)
        prev = pl.multiple_of(jnp.where(j == 0, qi, j - 1) * blk, blk)
        nxt = pl.multiple_of(jnp.minimum(j + 1, qi - 1) * blk, blk)
        for h in range(hg):
            acc_scr[h] = a_scr[h] * acc_scr[h] + weighted_values(prev, h, p_scr[par, h])
        for h in range(hg):
            s = jnp.where(sel_scr[pl.ds(h * nb + j, 1), :] > 0.0, s_scr[par, h], NEG_INF)
            m = m_scr[h]
            m_new = jnp.maximum(m, jnp.max(s, axis=0, keepdims=True))
            alpha = jnp.exp2(m - m_new)
            p = jnp.exp2(s - m_new)
            l_scr[h] = alpha * l_scr[h] + jnp.sum(p, axis=0, keepdims=True)
            m_scr[h] = m_new
            a_scr[h] = alpha
            p_scr[1 - par, h] = p.astype(_BF16)
        for h in range(hg):
            s_scr[1 - par, h] = scores(nxt, h)
        return c

    lax.fori_loop(0, qi, body, 0)
    last = pl.multiple_of(jnp.where(qi == 0, qi, qi - 1) * blk, blk)
    for h in range(hg):
        acc = a_scr[h] * acc_scr[h] + weighted_values(last, h, p_scr[lax.rem(qi, 2), h])
        o_ref[:, h * dh:(h + 1) * dh] = (acc / l_scr[h]).T.astype(_BF16)


def _moba(proj, *, batch, seq, sw, mw, dh, hg):
    t = proj.shape[0]
    heads = mw // dh
    nb = seq // MOBA_BLOCK
    tq = MOBA_BLOCK
    nq = seq // tq
    gw = hg * dh
    qc = 2 * sw // gw
    kc = qc + heads // hg
    vc = kc + heads // hg
    kern = functools.partial(_moba_kernel, nb=nb, topk=MOBA_TOPK, hg=hg, dh=dh)
    return pl.pallas_call(
        kern,
        grid=(batch, heads // hg, nq),
        in_specs=[pl.BlockSpec((tq, gw), lambda b, h, i: (b * nq + i, qc + h)),
                  pl.BlockSpec((seq, gw), lambda b, h, i: (b, kc + h)),
                  pl.BlockSpec((seq, gw), lambda b, h, i: (b, vc + h))],
        out_specs=pl.BlockSpec((tq, gw), lambda b, h, i: (b * nq + i, h)),
        out_shape=jax.ShapeDtypeStruct((t, mw), _BF16),
        scratch_shapes=[pltpu.VMEM((hg * nb, dh), _BF16), pltpu.VMEM((hg * nb, dh), _BF16),
                        pltpu.VMEM((gw, seq), _BF16), pltpu.VMEM((hg * nb, tq), _F32),
                        pltpu.VMEM((2, hg, MOBA_BLOCK, tq), _F32), pltpu.VMEM((2, hg, MOBA_BLOCK, tq), _BF16),
                        pltpu.VMEM((hg, dh, tq), _F32), pltpu.VMEM((hg, 1, tq), _F32),
                        pltpu.VMEM((hg, 1, tq), _F32), pltpu.VMEM((hg, 1, tq), _F32)],
        compiler_params=_params(("arbitrary", "arbitrary", "arbitrary")),
        name="moba",
    )(proj, proj, proj)


def _mixout_kernel(attn_ref, ma_ref, gb_ref, x_ref, gm_ref, shf_ref, scf_ref, ng_ref,
                   wmo_ref, wo_ref, wrh_ref, wrl_ref, br_ref,
                   h_ref, n2_ref, eid_ref, wt_ref, *, groups, epg):
    yb = jnp.dot(attn_ref[...], wmo_ref[...], preferred_element_type=_F32)
    merged = ma_ref[...].astype(_F32) + jax.nn.sigmoid(gb_ref[...].astype(_F32)) * yb
    mix = jnp.dot(merged.astype(_BF16), wo_ref[...], preferred_element_type=_F32)
    h = x_ref[...] + gm_ref[0] * mix
    h_ref[...] = h
    n2 = _rms_modulate(h, ng_ref[...], scf_ref[0], shf_ref[0])
    _store_token_major(n2_ref, n2)

    n_hi, n_lo = _split_bf16(n2)
    lg = (jnp.dot(n_hi, wrh_ref[...], preferred_element_type=_F32)
          + jnp.dot(n_lo, wrh_ref[...], preferred_element_type=_F32)
          + jnp.dot(n_hi, wrl_ref[...], preferred_element_type=_F32))
    lg = lg.T[:ROUTER_ROWS, :] + br_ref[:, 0:1]

    gl = [lg[g:g + 1, :] for g in range(groups)]
    gmax = functools.reduce(jnp.maximum, gl)
    denom = functools.reduce(jnp.add, [jnp.exp(v - gmax) for v in gl])
    p_group = 1.0 / denom
    g_sel = jnp.full(gmax.shape, groups - 1, jnp.int32)
    for g in range(groups - 2, -1, -1):
        g_sel = jnp.where(gl[g] == gmax, g, g_sel)

    el = []
    for e in range(epg):
        v = lg[groups + e:groups + e + 1, :]
        for g in range(1, groups):
            r = groups + g * epg + e
            v = jnp.where(g_sel == g, lg[r:r + 1, :], v)
        el.append(v)

    def top1(vals):
        vmax = functools.reduce(jnp.maximum, vals)
        idx = jnp.full(vmax.shape, epg - 1, jnp.int32)
        for e in range(epg - 2, -1, -1):
            idx = jnp.where(vals[e] == vmax, e, idx)
        return vmax, idx

    v1, i1 = top1(el)
    v2, i2 = top1([jnp.where(i1 == e, -jnp.inf, el[e]) for e in range(epg)])
    b = jnp.exp(v2 - v1)
    eid_ref[0:1, :] = g_sel * epg + i1
    eid_ref[1:2, :] = g_sel * epg + i2
    wt_ref[0:1, :] = (1.0 / (1.0 + b)) * p_group
    wt_ref[1:2, :] = (b / (1.0 + b)) * p_group


def _mixout(attn, m_a, proj, x2, ada3, norm_g, w_moba_out, w_out, wr_hi, wr_lo, br,
            *, seq, sw, mw, groups, epg, tm):
    t, d = x2.shape
    per_b = seq // tm
    gb_blk = (2 * sw + 3 * mw) // d + 1
    once = pl.Buffered(1)
    kern = functools.partial(_mixout_kernel, groups=groups, epg=epg)

    def ada_spec(k):
        return pl.BlockSpec((1, 1, d), lambda i: ((i // per_b) * 6 + k, 0, 0))

    return pl.pallas_call(
        kern,
        grid=(t // tm,),
        in_specs=[pl.BlockSpec((tm, mw), lambda i: (i, 0)),
                  pl.BlockSpec((tm, d), lambda i: (i, 0)),
                  pl.BlockSpec((tm, d), lambda i: (i, gb_blk)),
                  pl.BlockSpec((tm, d), lambda i: (i, 0)),
                  ada_spec(2), ada_spec(3), ada_spec(4),
                  pl.BlockSpec((1, d), lambda i: (0, 0)),
                  pl.BlockSpec((mw, d), lambda i: (0, 0), pipeline_mode=once),
                  pl.BlockSpec((d, d), lambda i: (0, 0), pipeline_mode=once),
                  pl.BlockSpec((d, LANES), lambda i: (0, 0)),
                  pl.BlockSpec((d, LANES), lambda i: (0, 0)),
                  pl.BlockSpec((ROUTER_ROWS, LANES), lambda i: (0, 0))],
        out_specs=[pl.BlockSpec((tm, d), lambda i: (i, 0)),
                   pl.BlockSpec((tm * _pitch(d), LANES), lambda i: (i, 0)),
                   pl.BlockSpec((MOE_TOPK, tm), lambda i: (0, i)),
                   pl.BlockSpec((MOE_TOPK, tm), lambda i: (0, i))],
        out_shape=[jax.ShapeDtypeStruct((t, d), _F32),
                   jax.ShapeDtypeStruct((t * _pitch(d), LANES), _F32),
                   jax.ShapeDtypeStruct((MOE_TOPK, t), jnp.int32),
                   jax.ShapeDtypeStruct((MOE_TOPK, t), _F32)],
        compiler_params=_params(("arbitrary",)),
        name="mixout",
    )(attn, m_a, proj, x2, ada3, ada3, ada3, norm_g.reshape(1, d), w_moba_out, w_out, wr_hi, wr_lo, br)


def _plan_kernel(eid_ref, dest_ref, tab_ref, carry_scr, start_scr, tri_scr, *, n_exp, rb):
    phase = pl.program_id(0)
    first = pl.program_id(1) == 0
    tc = eid_ref.shape[1]
    nbp = tab_ref.shape[1]

    @pl.when((phase == 0) & first)
    def _():
        carry_scr[...] = jnp.zeros_like(carry_scr)
        earlier = lax.broadcasted_iota(jnp.int32, (tc, tc), 0) < lax.broadcasted_iota(jnp.int32, (tc, tc), 1)
        tri_scr[...] = earlier.astype(_BF16)

    eio = lax.broadcasted_iota(jnp.int32, (n_exp, tc), 0)
    oh0 = (eio == eid_ref[0:1, :]).astype(_F32)
    oh1 = (eio == eid_ref[1:2, :]).astype(_F32)
    oh = oh0 + oh1

    @pl.when((phase == 1) & first)
    def _():
        counts = carry_scr[...].astype(jnp.int32)
        blocks = lax.shift_right_logical(counts + (rb - 1), rb.bit_length() - 1).astype(_F32)
        upto = lax.broadcasted_iota(jnp.int32, (n_exp, n_exp), 0) >= lax.broadcasted_iota(jnp.int32, (n_exp, n_exp), 1)
        end_blk = jnp.dot(upto.astype(_BF16), blocks.astype(_BF16), preferred_element_type=_F32)
        start_scr[...] = (end_blk - blocks) * rb
        carry_scr[...] = jnp.zeros_like(carry_scr)
        blk_id = lax.broadcasted_iota(jnp.int32, (n_exp, nbp), 1).astype(_F32)
        owner = jnp.sum((end_blk[:, 0:1] <= blk_id).astype(_F32), axis=0, keepdims=True)
        owner = jnp.minimum(owner, n_exp - 1.0)
        used = jnp.broadcast_to(end_blk[n_exp - 1:n_exp, 0:1], (1, nbp))
        trow = lax.broadcasted_iota(jnp.int32, tab_ref.shape, 0)
        tab_ref[...] = jnp.where(trow == 0, owner, jnp.where(trow == 1, used, 0.0)).astype(jnp.int32)

    @pl.when(phase == 1)
    def _():
        before = jnp.dot(oh.astype(_BF16), tri_scr[...], preferred_element_type=_F32)
        base = start_scr[:, 0:1] + carry_scr[:, 0:1] + before
        dest_ref[0:1, :] = jnp.sum(oh0 * base, axis=0, keepdims=True).astype(jnp.int32)
        dest_ref[1:2, :] = jnp.sum(oh1 * base, axis=0, keepdims=True).astype(jnp.int32)

    carry_scr[...] = carry_scr[...] + jnp.sum(oh, axis=1, keepdims=True)


def _plan(eid, *, n_exp, tc, rb, n_blocks):
    t = eid.shape[1]
    assert rb & (rb - 1) == 0 and t * MOE_TOPK // rb + n_exp < 256
    nbp = -(-n_blocks // LANES) * LANES
    kern = functools.partial(_plan_kernel, n_exp=n_exp, rb=rb)
    return pl.pallas_call(
        kern,
        grid=(2, t // tc),
        in_specs=[pl.BlockSpec((MOE_TOPK, tc), lambda p, i: (0, i))],
        out_specs=[pl.BlockSpec((MOE_TOPK, tc), lambda p, i: (0, i * p)),
                   pl.BlockSpec((8, nbp), lambda p, i: (0, 0))],
        out_shape=[jax.ShapeDtypeStruct((MOE_TOPK, t), jnp.int32),
                   jax.ShapeDtypeStruct((8, nbp), jnp.int32)],
        scratch_shapes=[pltpu.VMEM((n_exp, LANES), _F32), pltpu.VMEM((n_exp, LANES), _F32),
                        pltpu.VMEM((tc, tc), _BF16)],
        compiler_params=_params(("arbitrary", "arbitrary")),
        name="plan",
    )(eid)


def _invert_kernel(dest_ref, pad_ref, inv_ref, sem):
    i = pl.program_id(0)
    tch = dest_ref.shape[1]

    @pl.when(i == 0)
    def _():
        fill = pltpu.make_async_copy(pad_ref, inv_ref, sem)
        fill.start()
        fill.wait()

    def body(t, c):
        a = (i * tch + t) * MOE_TOPK
        for k in range(MOE_TOPK):
            inv_ref[dest_ref[k, t]] = a + k
        return c

    lax.fori_loop(0, tch, body, 0, unroll=8)


def _invert(dest3, cap):
    steps, _, tch = dest3.shape
    pad = np.zeros((cap,), np.int32)
    return pl.pallas_call(
        _invert_kernel,
        grid=(steps,),
        in_specs=[pl.BlockSpec((None, MOE_TOPK, tch), lambda i: (i, 0, 0), memory_space=pltpu.SMEM),
                  pl.BlockSpec(memory_space=pl.ANY)],
        out_specs=pl.BlockSpec(pad.shape, lambda i: (0,), memory_space=pltpu.SMEM),
        out_shape=jax.ShapeDtypeStruct(pad.shape, jnp.int32),
        scratch_shapes=[pltpu.SemaphoreType.DMA(())],
        compiler_params=_params(("arbitrary",)),
        name="invert",
    )(dest3, jnp.asarray(pad))


def _aligned_rows(chunk, rows_per_chunk):
    row = chunk * rows_per_chunk
    return pl.multiple_of(row, 8) if rows_per_chunk % 8 == 0 else row


def _expert_kernel(be_ref, nu_ref, invc_ref, invn_ref, n2_ref, wg_ref, wu_ref, wd_ref, y_ref,
                   wg_s, wu_s, wd_s, xbuf0, xbuf1, gsem, *, rb, d):
    i = pl.program_id(0)
    last = nu_ref[0] - 1
    used = i <= last
    fresh = (i == 0) | (be_ref[i] != be_ref[jnp.maximum(i - 1, 0)])
    ch = d // LANES
    pitch = _pitch(d)
    xbuf = (xbuf0, xbuf1)

    def gather(inv_ref, p):
        for r in range(rb):
            tok = lax.shift_right_logical(inv_ref[0, r], 1)
            pltpu.make_async_copy(n2_ref.at[pl.ds(_aligned_rows(tok, pitch), ch)],
                                  xbuf[p].at[pl.ds(r * pitch, ch)], gsem.at[p]).start(priority=r % 2)

    def gather_wait(p):
        pltpu.make_async_copy(n2_ref.at[pl.ds(0, rb * ch)], xbuf[p].at[pl.ds(0, rb * ch)], gsem.at[p]).wait()

    @pl.when(i == 0)
    def _():
        gather(invc_ref, 0)

    @pl.when(used & fresh)
    def _():
        wg_s[...] = wg_ref[0].astype(_BF16)
        wu_s[...] = wu_ref[0].astype(_BF16)
        wd_s[...] = wd_ref[0].astype(_BF16)

    def step(p):
        gather_wait(p)
        gather(invn_ref, 1 - p)
        xb = _load_token_major(xbuf[p], rb, d).astype(_BF16)
        gate = jnp.dot(xb, wg_s[...], preferred_element_type=_F32)
        up = jnp.dot(xb, wu_s[...], preferred_element_type=_F32)
        hid = (jax.nn.silu(gate) * up).astype(_BF16)
        _store_token_major(y_ref, jnp.dot(hid, wd_s[...], preferred_element_type=_F32))

        @pl.when(i == last)
        def _():
            gather_wait(1 - p)

    for p in range(2):
        pl.when(used & (lax.rem(i, 2) == p))(functools.partial(step, p))

    @pl.when(jnp.logical_not(used))
    def _():
        y_ref[...] = jnp.zeros_like(y_ref)


def _experts(block_expert, n_used, inv3, n2, w_gate, w_up, w_down):
    nblk, _, rb = inv3.shape
    d, f = w_gate.shape[1:]
    ch = d // LANES

    def inv_spec(shift):
        return pl.BlockSpec((None, 1, rb), lambda i, be, nu: (jnp.clip(i + shift, 0, nu[0] - 1), 0, 0),
                            memory_space=pltpu.SMEM)

    grid_spec = pltpu.PrefetchScalarGridSpec(
        num_scalar_prefetch=2,
        grid=(nblk,),
        in_specs=[inv_spec(0), inv_spec(1),
                  pl.BlockSpec(memory_space=pl.ANY),
                  pl.BlockSpec((1, d, f), lambda i, be, nu: (be[i], 0, 0)),
                  pl.BlockSpec((1, d, f), lambda i, be, nu: (be[i], 0, 0)),
                  pl.BlockSpec((1, f, d), lambda i, be, nu: (be[i], 0, 0))],
        out_specs=pl.BlockSpec((rb * ch, LANES), lambda i, be, nu: (i, 0)),
        scratch_shapes=[pltpu.VMEM((d, f), _BF16), pltpu.VMEM((d, f), _BF16), pltpu.VMEM((f, d), _BF16),
                        pltpu.VMEM((rb * _pitch(d), LANES), _F32), pltpu.VMEM((rb * _pitch(d), LANES), _F32),
                        pltpu.SemaphoreType.DMA((2,))],
    )
    kern = functools.partial(_expert_kernel, rb=rb, d=d)
    return pl.pallas_call(
        kern,
        grid_spec=grid_spec,
        out_shape=jax.ShapeDtypeStruct((nblk * rb * ch, LANES), _F32),
        compiler_params=_params(("arbitrary",)),
        name="experts",
    )(block_expert, n_used, inv3, inv3, n2, w_gate, w_up, w_down)


def _combine_kernel(destc_ref, destn_ref, h_ref, wc_ref, gf_ref, ng_ref, y_ref, o_ref,
                    b00, b01, b10, b11, sem):
    i = pl.program_id(0)
    tk, d = h_ref.shape
    ch = d // LANES
    pitch = b00.shape[0] // tk
    bufs = ((b00, b01), (b10, b11))

    def gather(dest_ref, p):
        for r in range(tk):
            for k in range(MOE_TOPK):
                pltpu.make_async_copy(y_ref.at[pl.ds(_aligned_rows(dest_ref[k, r], ch), ch)],
                                      bufs[p][k].at[pl.ds(r * pitch, ch)], sem.at[p]).start(priority=r % 2)

    def gather_wait(p):
        for k in range(MOE_TOPK):
            pltpu.make_async_copy(y_ref.at[pl.ds(0, tk * ch)], bufs[p][k].at[pl.ds(0, tk * ch)], sem.at[p]).wait()

    @pl.when(i == 0)
    def _():
        gather(destc_ref, 0)

    def step(p):
        gather_wait(p)
        gather(destn_ref, 1 - p)
        y = [_load_token_major(bufs[p][k], tk, d) for k in range(MOE_TOPK)]
        moe = wc_ref[:, 0:1] * y[0] + wc_ref[:, 1:2] * y[1]
        h = h_ref[...] + gf_ref[0] * moe
        o_ref[...] = h * lax.rsqrt(jnp.mean(h * h, axis=-1, keepdims=True) + EPS) * ng_ref[...]

        @pl.when(i == pl.num_programs(0) - 1)
        def _():
            gather_wait(1 - p)

    for p in range(2):
        pl.when(lax.rem(i, 2) == p)(functools.partial(step, p))


def _combine(dest3, h1, w_cols, ada3, norm_g, y_rows, *, seq):
    t, d = h1.shape
    steps, _, tk = dest3.shape
    per_b = seq // tk

    def dest_spec(shift):
        return pl.BlockSpec((None, MOE_TOPK, tk), lambda i: (jnp.minimum(i + shift, steps - 1), 0, 0),
                            memory_space=pltpu.SMEM)

    return pl.pallas_call(
        _combine_kernel,
        grid=(steps,),
        in_specs=[dest_spec(0), dest_spec(1),
                  pl.BlockSpec((tk, d), lambda i: (i, 0)),
                  pl.BlockSpec((tk, LANES), lambda i: (i, 0)),
                  pl.BlockSpec((1, 1, d), lambda i: ((i // per_b) * 6 + 5, 0, 0)),
                  pl.BlockSpec((1, d), lambda i: (0, 0)),
                  pl.BlockSpec(memory_space=pl.ANY)],
        out_specs=pl.BlockSpec((tk, d), lambda i: (i, 0)),
        out_shape=jax.ShapeDtypeStruct((t, d), _F32),
        scratch_shapes=[*[pltpu.VMEM((tk * _pitch(d), LANES), _F32) for _ in range(2 * MOE_TOPK)],
                        pltpu.SemaphoreType.DMA((2,))],
        compiler_params=_params(("arbitrary",)),
        name="combine",
    )(dest3, dest3, h1, w_cols, ada3, norm_g.reshape(1, d), y_rows)


def _rotary_tables(positions, dh):
    inv_freq = ROPE_THETA ** (-jnp.arange(0, dh, 2, dtype=_F32) / dh)
    ang = positions.astype(_F32)[..., None] * inv_freq
    cos, sin = jnp.cos(ang), jnp.sin(ang)
    t = cos.shape[0] * cos.shape[1]
    cos = jnp.concatenate([cos, cos], axis=-1).reshape(t, dh)
    sin = jnp.concatenate([-sin, sin], axis=-1).reshape(t, dh)
    return cos, sin


def _tile(n, want):
    while n % want:
        want //= 2
    return want


def kernel(x, c, positions, w_ada, b_ada, norm_mix_g, w_in, sgu_ln_g, sgu_ln_b, sgu_w_s, sgu_b_s,
           w_sgu_out, w_moba_out, w_out, norm_ffn_g, w_route_group, b_route_group, w_route_expert,
           b_route_expert, w_exp_gate, w_exp_up, w_exp_down, norm_final_g):
    batch, seq, d = x.shape
    depth = w_ada.shape[0]
    t = batch * seq
    sw = sgu_ln_g.shape[1]
    mw = w_moba_out.shape[1]
    dh = mw // MOBA_HEADS
    groups = w_route_group.shape[2]
    n_exp = w_route_expert.shape[2]
    epg = n_exp // groups
    rb = EXPERT_ROW_BLOCK
    cap = (t * MOE_TOPK // rb + n_exp) * rb
    assert depth == 1, "the final RMSNorm is fused into the single layer's combine"
    assert dh == LANES and seq % MOBA_BLOCK == 0 and sw * 2 == d and mw % sw == 0
    assert groups + n_exp <= ROUTER_ROWS and (t * MOE_TOPK) % rb == 0

    cos, sin = _rotary_tables(positions, dh)
    c_pad = jnp.zeros((8, d), _F32).at[:batch].set(c)
    h = x.reshape(t, d)

    for l in range(depth):
        ada = _ada(c_pad, w_ada[l], b_ada[l], _tile(6 * d, 1024))
        ada3 = ada[:batch].reshape(batch * 6, 1, d)

        proj = _inproj(h, ada3, norm_mix_g[l], w_in[l].astype(_BF16), cos, sin, sgu_ln_g[l], sgu_ln_b[l],
                       seq=seq, sw=sw, mw=mw, dh=dh, tm=_tile(seq, 1024))
        bs_wide = jnp.repeat(sgu_b_s[l].T, sw // sgu_w_s.shape[1], axis=1)
        m_a = _sgu(proj, sgu_w_s[l], bs_wide, w_sgu_out[l].astype(_BF16), sw=sw, mw=mw, tm=_tile(seq, 512))
        attn = _moba(proj, batch=batch, seq=seq, sw=sw, mw=mw, dh=dh, hg=min(MOBA_HEAD_GROUP, MOBA_HEADS))

        wr = jnp.concatenate([w_route_group[l], w_route_expert[l]], axis=1)
        wr_hi, wr_lo = _split_bf16(jnp.zeros((d, LANES), _F32).at[:, :groups + n_exp].set(wr))
        br = jnp.concatenate([b_route_group[l], b_route_expert[l].reshape(-1)])
        br = jnp.broadcast_to(jnp.zeros((ROUTER_ROWS,), _F32).at[:groups + n_exp].set(br)[:, None],
                              (ROUTER_ROWS, LANES))
        h1, n2, eid, wt = _mixout(attn, m_a, proj, h, ada3, norm_ffn_g[l], w_moba_out[l].astype(_BF16),
                                  w_out[l].astype(_BF16), wr_hi, wr_lo, br,
                                  seq=seq, sw=sw, mw=mw, groups=groups, epg=epg, tm=_tile(seq, 256))

        dest, tab = _plan(eid, n_exp=n_exp, tc=_tile(t, 512), rb=rb, n_blocks=cap // rb)
        def token_blocks(size):
            return dest.reshape(MOE_TOPK, t // size, size).transpose(1, 0, 2)

        inv = _invert(token_blocks(_tile(t, 2048)), cap)
        y_rows = _experts(tab[0, :cap // rb], tab[1, :1], inv.reshape(cap // rb, 1, rb), n2,
                          w_exp_gate[l], w_exp_up[l], w_exp_down[l])
        w_cols = jnp.zeros((t, LANES), _F32).at[:, :MOE_TOPK].set(wt.T)
        h = _combine(token_blocks(_tile(seq, 256)), h1, w_cols, ada3, norm_final_g, y_rows, seq=seq)

    return h.reshape(batch, seq, d)
```

```python
import functools

import jax
import jax.numpy as jnp
import numpy as np
from jax import lax
from jax.experimental import pallas as pl
from jax.experimental.pallas import tpu as pltpu

MOBA_HEADS = 16
MOBA_BLOCK = 256
MOBA_TOPK = 3
SGU_CHUNK = 128
ROPE_THETA = 10000.0
MOE_GROUPS = 4
MOE_TOPK = 2
EPS = 1e-6
NEG_INF = -1e30
LOG2_E = 1.4426950408889634

LANES = 128
TOKEN_PAD_ROWS = 8
EXPERT_ROW_BLOCK = 256
INPROJ_CHUNK = 256
EXPERT_CHUNK = 256
NORM_BANDS = 4
MOBA_HEAD_GROUP = 4
ROUTER_ROWS = 40
VMEM_LIMIT = 56 * 1024 * 1024

_F32 = jnp.float32
_BF16 = jnp.bfloat16
_NT = (((1,), (1,)), ((), ()))


def _params(semantics, vmem=VMEM_LIMIT):
    return pltpu.CompilerParams(dimension_semantics=semantics, vmem_limit_bytes=vmem)


def _pitch(d):
    return d // LANES + TOKEN_PAD_ROWS


def _store_token_major(ref, val, zero_pad=True):
    n, d = val.shape
    pitch = ref.shape[0] // n
    for c in range(pitch if zero_pad else d // LANES):
        piece = val[:, c * LANES:(c + 1) * LANES] if c < d // LANES else jnp.zeros((n, LANES), val.dtype)
        ref[pl.ds(c, n, stride=pitch), :] = piece


def _load_token_major(ref, n, d, first=0, pitch=None):
    pitch = pitch or ref.shape[0] // n
    return jnp.concatenate([ref[pl.ds(first + c, n, stride=pitch), :] for c in range(d // LANES)], axis=1)


def _split_bf16(a):
    hi = a.astype(_BF16)
    lo = (a - hi.astype(_F32)).astype(_BF16)
    return hi, lo


def _ada_kernel(c_ref, w_ref, b_ref, o_ref):
    ca_hi, ca_lo = _split_bf16(jax.nn.silu(c_ref[...]))
    w_hi, w_lo = _split_bf16(w_ref[...])
    acc = jnp.dot(ca_hi, w_hi, preferred_element_type=_F32)
    acc += jnp.dot(ca_hi, w_lo, preferred_element_type=_F32)
    acc += jnp.dot(ca_lo, w_hi, preferred_element_type=_F32)
    o_ref[...] = acc + b_ref[...]


def _ada(c_pad, w_ada, b_ada, tn):
    rows, d = c_pad.shape
    n = w_ada.shape[1]
    return pl.pallas_call(
        _ada_kernel,
        grid=(n // tn,),
        in_specs=[pl.BlockSpec((rows, d), lambda j: (0, 0)),
                  pl.BlockSpec((d, tn), lambda j: (0, j)),
                  pl.BlockSpec((1, tn), lambda j: (0, j))],
        out_specs=pl.BlockSpec((rows, tn), lambda j: (0, j)),
        out_shape=jax.ShapeDtypeStruct((rows, n), _F32),
        compiler_params=_params(("arbitrary",)),
        name="ada",
    )(c_pad, w_ada, b_ada.reshape(1, n))


def _rms_modulate(x, g, sc, sh):
    y = x * lax.rsqrt(jnp.mean(x * x, axis=-1, keepdims=True) + EPS) * g
    return y * (1.0 + sc) + sh


def _inproj_kernel(x_ref, g_ref, sc_ref, sh_ref, w_ref, cos_ref, sin_ref, lng_ref, lnb_ref,
                   o_ref, n_scr, z_scr, *, q0, k0, v0, dh, scale, cw):
    j = pl.program_id(1)
    tm, tn = o_ref.shape

    def chunk(c):
        return jnp.dot(n_scr[...], w_ref[:, c * cw:(c + 1) * cw], preferred_element_type=_F32)

    @pl.when(j == 0)
    def _():
        band = tm // NORM_BANDS
        for r in range(NORM_BANDS):
            rows = slice(r * band, (r + 1) * band)
            nb = _rms_modulate(x_ref[rows, :], g_ref[...], sc_ref[0], sh_ref[0]).astype(_BF16)
            n_scr[rows, :] = nb
            for c in range(tn // cw):
                cols = slice(c * cw, (c + 1) * cw)
                o_ref[rows, cols] = jax.nn.gelu(
                    jnp.dot(nb, w_ref[:, cols], preferred_element_type=_F32)).astype(_BF16)

    @pl.when(j == 1)
    def _():
        for c in range(tn // cw):
            z_scr[:, c * cw:(c + 1) * cw] = jax.nn.gelu(chunk(c))
        z = z_scr[...]
        zc = z - jnp.mean(z, axis=-1, keepdims=True)
        var = jnp.mean(zc * zc, axis=-1, keepdims=True)
        o_ref[...] = (zc * lax.rsqrt(var + EPS) * lng_ref[...] + lnb_ref[...]).astype(_BF16)

    @pl.when(j >= q0)
    def _():
        mult = jnp.where(j < k0, scale, 1.0)
        cos = jnp.where(j < v0, cos_ref[...], 1.0) * mult
        sin = jnp.where(j < v0, sin_ref[...], 0.0) * mult
        for c in range(tn // cw):
            acc = chunk(c)
            for h in range(cw // dh):
                xh = acc[:, h * dh:(h + 1) * dh]
                rot = xh * cos + pltpu.roll(xh, dh // 2, 1) * sin
                o_ref[:, c * cw + h * dh:c * cw + (h + 1) * dh] = rot.astype(_BF16)


def _inproj(x2, ada3, norm_g, w_in, cos, sin, ln_g, ln_b, *, seq, sw, mw, dh, tm):
    t, d = x2.shape
    n = w_in.shape[1]
    tn = sw
    per_b = seq // tm
    q0 = 2
    k0 = q0 + mw // tn
    v0 = k0 + mw // tn
    cw = _tile(tn, INPROJ_CHUNK)
    assert cw % dh == 0
    kern = functools.partial(_inproj_kernel, q0=q0, k0=k0, v0=v0, dh=dh, scale=dh ** -0.5 * LOG2_E, cw=cw)
    return pl.pallas_call(
        kern,
        grid=(t // tm, n // tn),
        in_specs=[pl.BlockSpec((tm, d), lambda i, j: (i, 0)),
                  pl.BlockSpec((1, d), lambda i, j: (0, 0)),
                  pl.BlockSpec((1, 1, d), lambda i, j: ((i // per_b) * 6 + 1, 0, 0)),
                  pl.BlockSpec((1, 1, d), lambda i, j: ((i // per_b) * 6 + 0, 0, 0)),
                  pl.BlockSpec((d, tn), lambda i, j: (0, j)),
                  pl.BlockSpec((tm, dh), lambda i, j: (i, 0)),
                  pl.BlockSpec((tm, dh), lambda i, j: (i, 0)),
                  pl.BlockSpec((1, sw), lambda i, j: (0, 0)),
                  pl.BlockSpec((1, sw), lambda i, j: (0, 0))],
        out_specs=pl.BlockSpec((tm, tn), lambda i, j: (i, j)),
        out_shape=jax.ShapeDtypeStruct((t, n), _BF16),
        scratch_shapes=[pltpu.VMEM((tm, d), _BF16), pltpu.VMEM((tm, tn), _F32)],
        compiler_params=_params(("arbitrary", "arbitrary")),
        name="inproj",
    )(x2, norm_g.reshape(1, d), ada3, ada3, w_in, cos, sin, ln_g.reshape(1, sw), ln_b.reshape(1, sw))


def _sgu_kernel(u_ref, v_ref, ga_ref, ws_ref, bs_ref, wo_ref, o_ref, gated_scr, *, groups, cg):
    c = SGU_CHUNK
    tri = lax.broadcasted_iota(jnp.int32, (c, c), 0) >= lax.broadcasted_iota(jnp.int32, (c, c), 1)
    for g in range(groups):
        wg = jnp.where(tri, ws_ref[g], 0.0).astype(_BF16)
        cols = slice(g * cg, (g + 1) * cg)
        for ci in range(u_ref.shape[0] // c):
            rows = slice(ci * c, (ci + 1) * c)
            sv = jnp.dot(wg, v_ref[rows, cols], preferred_element_type=_F32) + bs_ref[:, cols]
            gated_scr[rows, cols] = (u_ref[rows, cols].astype(_F32) * sv).astype(_BF16)
    ya = jnp.dot(gated_scr[...], wo_ref[...], preferred_element_type=_F32)
    o_ref[...] = (jax.nn.sigmoid(ga_ref[...].astype(_F32)) * ya).astype(_BF16)


def _sgu(proj, w_s, bs_wide, w_sgu_out, *, sw, mw, tm):
    t = proj.shape[0]
    d = w_sgu_out.shape[1]
    groups = w_s.shape[0]
    ga_blk = (2 * sw + 3 * mw) // d
    kern = functools.partial(_sgu_kernel, groups=groups, cg=sw // groups)
    return pl.pallas_call(
        kern,
        grid=(t // tm,),
        in_specs=[pl.BlockSpec((tm, sw), lambda i: (i, 0)),
                  pl.BlockSpec((tm, sw), lambda i: (i, 1)),
                  pl.BlockSpec((tm, d), lambda i: (i, ga_blk)),
                  pl.BlockSpec(w_s.shape, lambda i: (0, 0, 0)),
                  pl.BlockSpec(bs_wide.shape, lambda i: (0, 0)),
                  pl.BlockSpec(w_sgu_out.shape, lambda i: (0, 0))],
        out_specs=pl.BlockSpec((tm, d), lambda i: (i, 0)),
        out_shape=jax.ShapeDtypeStruct((t, d), _BF16),
        scratch_shapes=[pltpu.VMEM((tm, sw), _BF16)],
        compiler_params=_params(("arbitrary",)),
        name="sgu",
    )(proj, proj, proj, w_s, bs_wide, w_sgu_out)


def _moba_kernel(q_ref, k_ref, v_ref, o_ref, kmh_scr, kml_scr, vt_scr, sel_scr,
                 s_scr, p_scr, acc_scr, m_scr, l_scr, a_scr, *, nb, topk, hg, dh):
    qi = pl.program_id(2)
    blk = MOBA_BLOCK
    tq = q_ref.shape[0]

    @pl.when(qi == 0)
    def _():
        for h in range(hg):
            cols = slice(h * dh, (h + 1) * dh)
            kf = k_ref[:, cols].astype(_F32).reshape(nb, blk, dh)
            km_hi, km_lo = _split_bf16(jnp.mean(kf, axis=1))
            kmh_scr[h * nb:(h + 1) * nb, :] = km_hi
            kml_scr[h * nb:(h + 1) * nb, :] = km_lo
            for c in range(nb):
                rows = slice(c * blk, (c + 1) * blk)
                vt_scr[cols, rows] = v_ref[rows, cols].astype(_F32).T.astype(_BF16)

    row = lax.broadcasted_iota(jnp.int32, (nb, tq), 0)
    past = row < qi
    kpos = lax.broadcasted_iota(jnp.int32, (blk, tq), 0)
    qpos = lax.broadcasted_iota(jnp.int32, (blk, tq), 1)
    own = pl.multiple_of(qi * blk, blk)

    def scores(start, h):
        cols = slice(h * dh, (h + 1) * dh)
        return lax.dot_general(k_ref[pl.ds(start, blk), cols], q_ref[:, cols], _NT,
                               preferred_element_type=_F32)

    def weighted_values(start, h, p):
        return jnp.dot(vt_scr[h * dh:(h + 1) * dh, pl.ds(start, blk)], p, preferred_element_type=_F32)

    for h in range(hg):
        hrows = slice(h * nb, (h + 1) * nb)
        q = q_ref[:, h * dh:(h + 1) * dh]
        gate = (lax.dot_general(kmh_scr[hrows, :], q, _NT, preferred_element_type=_F32)
                + lax.dot_general(kml_scr[hrows, :], q, _NT, preferred_element_type=_F32))
        gm = jnp.where(past, gate, NEG_INF)
        beaten = jnp.zeros((nb, tq), jnp.int32)
        for jp in range(nb):
            other = gm[jp:jp + 1, :]
            wins = (other > gm) | ((other == gm) & (jp < row))
            beaten += wins.astype(jnp.int32)
        sel_scr[hrows, :] = (past & (beaten < topk)).astype(_F32)

        s = jnp.where(kpos <= qpos, scores(own, h), NEG_INF)
        m0 = jnp.max(s, axis=0, keepdims=True)
        p = jnp.exp2(s - m0)
        m_scr[h] = m0
        l_scr[h] = jnp.sum(p, axis=0, keepdims=True)
        a_scr[h] = jnp.ones_like(m0)
        acc_scr[h] = jnp.zeros(acc_scr.shape[1:], _F32)
        p_scr[0, h] = p.astype(_BF16)
        s_scr[0, h] = scores(0, h)

    def body(j, c):
        par = lax.rem(j, 2)
        prev = pl.multiple_of(jnp.where(j == 0, qi, j - 1) * blk, blk)
        nxt = pl.multiple_of(jnp.minimum(j + 1, qi - 1) * blk, blk)
        for h in range(hg):
            acc_scr[h] = a_scr[h] * acc_scr[h] + weighted_values(prev, h, p_scr[par, h])
        for h in range(hg):
            s = jnp.where(sel_scr[pl.ds(h * nb + j, 1), :] > 0.0, s_scr[par, h], NEG_INF)
            m = m_scr[h]
            m_new = jnp.maximum(m, jnp.max(s, axis=0, keepdims=True))
            alpha = jnp.exp2(m - m_new)
            p = jnp.exp2(s - m_new)
            l_scr[h] = alpha * l_scr[h] + jnp.sum(p, axis=0, keepdims=True)
            m_scr[h] = m_new
            a_scr[h] = alpha
            p_scr[1 - par, h] = p.astype(_BF16)
        for h in range(hg):
            s_scr[1 - par, h] = scores(nxt, h)
        return c

    lax.fori_loop(0, qi, body, 0)
    last = pl.multiple_of(jnp.where(qi == 0, qi, qi - 1) * blk, blk)
    for h in range(hg):
        acc = a_scr[h] * acc_scr[h] + weighted_values(last, h, p_scr[lax.rem(qi, 2), h])
        o_ref[:, h * dh:(h + 1) * dh] = (acc / l_scr[h]).T.astype(_BF16)


def _moba(proj, *, batch, seq, sw, mw, dh, hg):
    t = proj.shape[0]
    heads = mw // dh
    nb = seq // MOBA_BLOCK
    tq = MOBA_BLOCK
    nq = seq // tq
    gw = hg * dh
    qc = 2 * sw // gw
    kc = qc + heads // hg
    vc = kc + heads // hg
    kern = functools.partial(_moba_kernel, nb=nb, topk=MOBA_TOPK, hg=hg, dh=dh)
    return pl.pallas_call(
        kern,
        grid=(batch, heads // hg, nq),
        in_specs=[pl.BlockSpec((tq, gw), lambda b, h, i: (b * nq + i, qc + h)),
                  pl.BlockSpec((seq, gw), lambda b, h, i: (b, kc + h)),
                  pl.BlockSpec((seq, gw), lambda b, h, i: (b, vc + h))],
        out_specs=pl.BlockSpec((tq, gw), lambda b, h, i: (b * nq + i, h)),
        out_shape=jax.ShapeDtypeStruct((t, mw), _BF16),
        scratch_shapes=[pltpu.VMEM((hg * nb, dh), _BF16), pltpu.VMEM((hg * nb, dh), _BF16),
                        pltpu.VMEM((gw, seq), _BF16), pltpu.VMEM((hg * nb, tq), _F32),
                        pltpu.VMEM((2, hg, MOBA_BLOCK, tq), _F32), pltpu.VMEM((2, hg, MOBA_BLOCK, tq), _BF16),
                        pltpu.VMEM((hg, dh, tq), _F32), pltpu.VMEM((hg, 1, tq), _F32),
                        pltpu.VMEM((hg, 1, tq), _F32), pltpu.VMEM((hg, 1, tq), _F32)],
        compiler_params=_params(("arbitrary", "arbitrary", "arbitrary")),
        name="moba",
    )(proj, proj, proj)


def _mixout_kernel(attn_ref, ma_ref, gb_ref, x_ref, gm_ref, shf_ref, scf_ref, ng_ref,
                   wmo_ref, wo_ref, wrh_ref, wrl_ref, br_ref,
                   h_ref, n2_ref, eid_ref, wt_ref, *, groups, epg):
    yb = jnp.dot(attn_ref[...], wmo_ref[...], preferred_element_type=_F32)
    merged = ma_ref[...].astype(_F32) + jax.nn.sigmoid(gb_ref[...].astype(_F32)) * yb
    mix = jnp.dot(merged.astype(_BF16), wo_ref[...], preferred_element_type=_F32)
    h = x_ref[...] + gm_ref[0] * mix
    h_ref[...] = h
    n2 = _rms_modulate(h, ng_ref[...], scf_ref[0], shf_ref[0])
    _store_token_major(n2_ref, n2)

    n_hi, n_lo = _split_bf16(n2)
    lg = (jnp.dot(n_hi, wrh_ref[...], preferred_element_type=_F32)
          + jnp.dot(n_lo, wrh_ref[...], preferred_element_type=_F32)
          + jnp.dot(n_hi, wrl_ref[...], preferred_element_type=_F32))
    lg = lg.T[:ROUTER_ROWS, :] + br_ref[:, 0:1]

    gl = [lg[g:g + 1, :] for g in range(groups)]
    gmax = functools.reduce(jnp.maximum, gl)
    denom = functools.reduce(jnp.add, [jnp.exp(v - gmax) for v in gl])
    p_group = 1.0 / denom
    g_sel = jnp.full(gmax.shape, groups - 1, jnp.int32)
    for g in range(groups - 2, -1, -1):
        g_sel = jnp.where(gl[g] == gmax, g, g_sel)

    el = []
    for e in range(epg):
        v = lg[groups + e:groups + e + 1, :]
        for g in range(1, groups):
            r = groups + g * epg + e
            v = jnp.where(g_sel == g, lg[r:r + 1, :], v)
        el.append(v)

    def top1(vals):
        vmax = functools.reduce(jnp.maximum, vals)
        idx = jnp.full(vmax.shape, epg - 1, jnp.int32)
        for e in range(epg - 2, -1, -1):
            idx = jnp.where(vals[e] == vmax, e, idx)
        return vmax, idx

    v1, i1 = top1(el)
    v2, i2 = top1([jnp.where(i1 == e, -jnp.inf, el[e]) for e in range(epg)])
    b = jnp.exp(v2 - v1)
    eid_ref[0:1, :] = g_sel * epg + i1
    eid_ref[1:2, :] = g_sel * epg + i2
    wt_ref[0:1, :] = (1.0 / (1.0 + b)) * p_group
    wt_ref[1:2, :] = (b / (1.0 + b)) * p_group


def _mixout(attn, m_a, proj, x2, ada3, norm_g, w_moba_out, w_out, wr_hi, wr_lo, br,
            *, seq, sw, mw, groups, epg, tm):
    t, d = x2.shape
    per_b = seq // tm
    gb_blk = (2 * sw + 3 * mw) // d + 1
    once = pl.Buffered(1)
    kern = functools.partial(_mixout_kernel, groups=groups, epg=epg)

    def ada_spec(k):
        return pl.BlockSpec((1, 1, d), lambda i: ((i // per_b) * 6 + k, 0, 0))

    return pl.pallas_call(
        kern,
        grid=(t // tm,),
        in_specs=[pl.BlockSpec((tm, mw), lambda i: (i, 0)),
                  pl.BlockSpec((tm, d), lambda i: (i, 0)),
                  pl.BlockSpec((tm, d), lambda i: (i, gb_blk)),
                  pl.BlockSpec((tm, d), lambda i: (i, 0)),
                  ada_spec(2), ada_spec(3), ada_spec(4),
                  pl.BlockSpec((1, d), lambda i: (0, 0)),
                  pl.BlockSpec((mw, d), lambda i: (0, 0), pipeline_mode=once),
                  pl.BlockSpec((d, d), lambda i: (0, 0), pipeline_mode=once),
                  pl.BlockSpec((d, LANES), lambda i: (0, 0)),
                  pl.BlockSpec((d, LANES), lambda i: (0, 0)),
                  pl.BlockSpec((ROUTER_ROWS, LANES), lambda i: (0, 0))],
        out_specs=[pl.BlockSpec((tm, d), lambda i: (i, 0)),
                   pl.BlockSpec((tm * _pitch(d), LANES), lambda i: (i, 0)),
                   pl.BlockSpec((MOE_TOPK, tm), lambda i: (0, i)),
                   pl.BlockSpec((MOE_TOPK, tm), lambda i: (0, i))],
        out_shape=[jax.ShapeDtypeStruct((t, d), _F32),
                   jax.ShapeDtypeStruct((t * _pitch(d), LANES), _F32),
                   jax.ShapeDtypeStruct((MOE_TOPK, t), jnp.int32),
                   jax.ShapeDtypeStruct((MOE_TOPK, t), _F32)],
        compiler_params=_params(("arbitrary",)),
        name="mixout",
    )(attn, m_a, proj, x2, ada3, ada3, ada3, norm_g.reshape(1, d), w_moba_out, w_out, wr_hi, wr_lo, br)


def _plan_kernel(eid_ref, dest_ref, tab_ref, carry_scr, start_scr, tri_scr, *, n_exp, rb):
    phase = pl.program_id(0)
    first = pl.program_id(1) == 0
    tc = eid_ref.shape[1]
    nbp = tab_ref.shape[1]

    @pl.when((phase == 0) & first)
    def _():
        carry_scr[...] = jnp.zeros_like(carry_scr)
        earlier = lax.broadcasted_iota(jnp.int32, (tc, tc), 0) < lax.broadcasted_iota(jnp.int32, (tc, tc), 1)
        tri_scr[...] = earlier.astype(_BF16)

    eio = lax.broadcasted_iota(jnp.int32, (n_exp, tc), 0)
    oh0 = (eio == eid_ref[0:1, :]).astype(_F32)
    oh1 = (eio == eid_ref[1:2, :]).astype(_F32)
    oh = oh0 + oh1

    @pl.when((phase == 1) & first)
    def _():
        counts = carry_scr[...].astype(jnp.int32)
        blocks = lax.shift_right_logical(counts + (rb - 1), rb.bit_length() - 1).astype(_F32)
        upto = lax.broadcasted_iota(jnp.int32, (n_exp, n_exp), 0) >= lax.broadcasted_iota(jnp.int32, (n_exp, n_exp), 1)
        end_blk = jnp.dot(upto.astype(_BF16), blocks.astype(_BF16), preferred_element_type=_F32)
        start_scr[...] = (end_blk - blocks) * rb
        carry_scr[...] = jnp.zeros_like(carry_scr)
        blk_id = lax.broadcasted_iota(jnp.int32, (n_exp, nbp), 1).astype(_F32)
        owner = jnp.sum((end_blk[:, 0:1] <= blk_id).astype(_F32), axis=0, keepdims=True)
        owner = jnp.minimum(owner, n_exp - 1.0)
        used = jnp.broadcast_to(end_blk[n_exp - 1:n_exp, 0:1], (1, nbp))
        trow = lax.broadcasted_iota(jnp.int32, tab_ref.shape, 0)
        tab_ref[...] = jnp.where(trow == 0, owner, jnp.where(trow == 1, used, 0.0)).astype(jnp.int32)

    @pl.when(phase == 1)
    def _():
        before = jnp.dot(oh.astype(_BF16), tri_scr[...], preferred_element_type=_F32)
        base = start_scr[:, 0:1] + carry_scr[:, 0:1] + before
        dest_ref[0:1, :] = jnp.sum(oh0 * base, axis=0, keepdims=True).astype(jnp.int32)
        dest_ref[1:2, :] = jnp.sum(oh1 * base, axis=0, keepdims=True).astype(jnp.int32)

    carry_scr[...] = carry_scr[...] + jnp.sum(oh, axis=1, keepdims=True)


def _plan(eid, *, n_exp, tc, rb, n_blocks):
    t = eid.shape[1]
    assert rb & (rb - 1) == 0 and t * MOE_TOPK // rb + n_exp < 256
    nbp = -(-n_blocks // LANES) * LANES
    kern = functools.partial(_plan_kernel, n_exp=n_exp, rb=rb)
    return pl.pallas_call(
        kern,
        grid=(2, t // tc),
        in_specs=[pl.BlockSpec((MOE_TOPK, tc), lambda p, i: (0, i))],
        out_specs=[pl.BlockSpec((MOE_TOPK, tc), lambda p, i: (0, i * p)),
                   pl.BlockSpec((8, nbp), lambda p, i: (0, 0))],
        out_shape=[jax.ShapeDtypeStruct((MOE_TOPK, t), jnp.int32),
                   jax.ShapeDtypeStruct((8, nbp), jnp.int32)],
        scratch_shapes=[pltpu.VMEM((n_exp, LANES), _F32), pltpu.VMEM((n_exp, LANES), _F32),
                        pltpu.VMEM((tc, tc), _BF16)],
        compiler_params=_params(("arbitrary", "arbitrary")),
        name="plan",
    )(eid)


def _invert_kernel(dest_ref, pad_ref, inv_ref, sem):
    i = pl.program_id(0)
    tch = dest_ref.shape[1]

    @pl.when(i == 0)
    def _():
        fill = pltpu.make_async_copy(pad_ref, inv_ref, sem)
        fill.start()
        fill.wait()

    def body(t, c):
        a = (i * tch + t) * MOE_TOPK
        for k in range(MOE_TOPK):
            inv_ref[dest_ref[k, t]] = a + k
        return c

    lax.fori_loop(0, tch, body, 0, unroll=8)


def _invert(dest3, cap):
    steps, _, tch = dest3.shape
    pad = np.zeros((cap,), np.int32)
    return pl.pallas_call(
        _invert_kernel,
        grid=(steps,),
        in_specs=[pl.BlockSpec((None, MOE_TOPK, tch), lambda i: (i, 0, 0), memory_space=pltpu.SMEM),
                  pl.BlockSpec(memory_space=pl.ANY)],
        out_specs=pl.BlockSpec(pad.shape, lambda i: (0,), memory_space=pltpu.SMEM),
        out_shape=jax.ShapeDtypeStruct(pad.shape, jnp.int32),
        scratch_shapes=[pltpu.SemaphoreType.DMA(())],
        compiler_params=_params(("arbitrary",)),
        name="invert",
    )(dest3, jnp.asarray(pad))


def _aligned_rows(chunk, rows_per_chunk):
    row = chunk * rows_per_chunk
    return pl.multiple_of(row, 8) if rows_per_chunk % 8 == 0 else row


def _expert_kernel(be_ref, nu_ref, invc_ref, invn_ref, n2_ref, wg_ref, wu_ref, wd_ref, y_ref,
                   wg_s, wu_s, wd_s, xbuf0, xbuf1, gsem, *, rb, d):
    i = pl.program_id(0)
    last = nu_ref[0] - 1
    used = i <= last
    fresh = (i == 0) | (be_ref[i] != be_ref[jnp.maximum(i - 1, 0)])
    ch = d // LANES
    pitch = _pitch(d)
    xbuf = (xbuf0, xbuf1)

    def gather(inv_ref, p, rows=range(rb)):
        for r in rows:
            tok = lax.shift_right_logical(inv_ref[0, r], 1)
            pltpu.make_async_copy(n2_ref.at[pl.ds(_aligned_rows(tok, pitch), ch)],
                                  xbuf[p].at[pl.ds(r * pitch, ch)], gsem.at[p]).start(priority=r % 2)

    def gather_wait(p):
        pltpu.make_async_copy(n2_ref.at[pl.ds(0, rb * ch)], xbuf[p].at[pl.ds(0, rb * ch)], gsem.at[p]).wait()

    @pl.when(i == 0)
    def _():
        gather(invc_ref, 0)

    @pl.when(used & fresh)
    def _():
        wg_s[...] = wg_ref[0].astype(_BF16)
        wu_s[...] = wu_ref[0].astype(_BF16)
        wd_s[...] = wd_ref[0].astype(_BF16)

    f = wg_s.shape[1]
    hid_w = min(f, EXPERT_CHUNK)
    out_w = min(d, EXPERT_CHUNK)
    n_seg = f // hid_w + d // out_w
    seg_rows = [range(s * rb // n_seg, (s + 1) * rb // n_seg) for s in range(n_seg)]

    def paced_gather(p, seg, after):
        if after is not None:
            xbuf[p][pl.ds(seg_rows[seg][0] * pitch, 8), :] = after[0:8, 0:LANES]
        gather(invn_ref, p, seg_rows[seg])

    def step(p):
        gather_wait(p)
        xb = _load_token_major(xbuf[p], rb, d).astype(_BF16)
        hid, done = [], None
        for c in range(f // hid_w):
            paced_gather(1 - p, c, done)
            cols = slice(c * hid_w, (c + 1) * hid_w)
            gate = jnp.dot(xb, wg_s[:, cols], preferred_element_type=_F32)
            up = jnp.dot(xb, wu_s[:, cols], preferred_element_type=_F32)
            hid.append((jax.nn.silu(gate) * up).astype(_BF16))
            done = up
        hid = jnp.concatenate(hid, axis=1)
        for c in range(d // out_w):
            paced_gather(1 - p, f // hid_w + c, done)
            y = done = jnp.dot(hid, wd_s[:, c * out_w:(c + 1) * out_w], preferred_element_type=_F32)
            for k in range(out_w // LANES):
                y_ref[pl.ds(c * (out_w // LANES) + k, rb, stride=ch), :] = y[:, k * LANES:(k + 1) * LANES]

        @pl.when(i == last)
        def _():
            gather_wait(1 - p)

    for p in range(2):
        pl.when(used & (lax.rem(i, 2) == p))(functools.partial(step, p))

    @pl.when(jnp.logical_not(used))
    def _():
        y_ref[...] = jnp.zeros_like(y_ref)


def _experts(block_expert, n_used, inv3, n2, w_gate, w_up, w_down):
    nblk, _, rb = inv3.shape
    d, f = w_gate.shape[1:]
    ch = d // LANES

    def inv_spec(shift):
        return pl.BlockSpec((None, 1, rb), lambda i, be, nu: (jnp.clip(i + shift, 0, nu[0] - 1), 0, 0),
                            memory_space=pltpu.SMEM)

    grid_spec = pltpu.PrefetchScalarGridSpec(
        num_scalar_prefetch=2,
        grid=(nblk,),
        in_specs=[inv_spec(0), inv_spec(1),
                  pl.BlockSpec(memory_space=pl.ANY),
                  pl.BlockSpec((1, d, f), lambda i, be, nu: (be[i], 0, 0)),
                  pl.BlockSpec((1, d, f), lambda i, be, nu: (be[i], 0, 0)),
                  pl.BlockSpec((1, f, d), lambda i, be, nu: (be[i], 0, 0))],
        out_specs=pl.BlockSpec((rb * ch, LANES), lambda i, be, nu: (i, 0)),
        scratch_shapes=[pltpu.VMEM((d, f), _BF16), pltpu.VMEM((d, f), _BF16), pltpu.VMEM((f, d), _BF16),
                        pltpu.VMEM((rb * _pitch(d), LANES), _F32), pltpu.VMEM((rb * _pitch(d), LANES), _F32),
                        pltpu.SemaphoreType.DMA((2,))],
    )
    kern = functools.partial(_expert_kernel, rb=rb, d=d)
    return pl.pallas_call(
        kern,
        grid_spec=grid_spec,
        out_shape=jax.ShapeDtypeStruct((nblk * rb * ch, LANES), _F32),
        compiler_params=_params(("arbitrary",)),
        name="experts",
    )(block_expert, n_used, inv3, inv3, n2, w_gate, w_up, w_down)


def _combine_kernel(destc_ref, destn_ref, h_ref, wc_ref, gf_ref, ng_ref, y_ref, o_ref,
                    b00, b01, b10, b11, sem):
    i = pl.program_id(0)
    tk, d = h_ref.shape
    ch = d // LANES
    pitch = b00.shape[0] // tk
    bufs = ((b00, b01), (b10, b11))

    def gather(dest_ref, p):
        for r in range(tk):
            for k in range(MOE_TOPK):
                pltpu.make_async_copy(y_ref.at[pl.ds(_aligned_rows(dest_ref[k, r], ch), ch)],
                                      bufs[p][k].at[pl.ds(r * pitch, ch)], sem.at[p]).start(priority=r % 2)

    def gather_wait(p):
        for k in range(MOE_TOPK):
            pltpu.make_async_copy(y_ref.at[pl.ds(0, tk * ch)], bufs[p][k].at[pl.ds(0, tk * ch)], sem.at[p]).wait()

    @pl.when(i == 0)
    def _():
        gather(destc_ref, 0)

    def step(p):
        gather_wait(p)
        gather(destn_ref, 1 - p)
        y = [_load_token_major(bufs[p][k], tk, d) for k in range(MOE_TOPK)]
        moe = wc_ref[:, 0:1] * y[0] + wc_ref[:, 1:2] * y[1]
        h = h_ref[...] + gf_ref[0] * moe
        o_ref[...] = h * lax.rsqrt(jnp.mean(h * h, axis=-1, keepdims=True) + EPS) * ng_ref[...]

        @pl.when(i == pl.num_programs(0) - 1)
        def _():
            gather_wait(1 - p)

    for p in range(2):
        pl.when(lax.rem(i, 2) == p)(functools.partial(step, p))


def _combine(dest3, h1, w_cols, ada3, norm_g, y_rows, *, seq):
    t, d = h1.shape
    steps, _, tk = dest3.shape
    per_b = seq // tk

    def dest_spec(shift):
        return pl.BlockSpec((None, MOE_TOPK, tk), lambda i: (jnp.minimum(i + shift, steps - 1), 0, 0),
                            memory_space=pltpu.SMEM)

    return pl.pallas_call(
        _combine_kernel,
        grid=(steps,),
        in_specs=[dest_spec(0), dest_spec(1),
                  pl.BlockSpec((tk, d), lambda i: (i, 0)),
                  pl.BlockSpec((tk, LANES), lambda i: (i, 0)),
                  pl.BlockSpec((1, 1, d), lambda i: ((i // per_b) * 6 + 5, 0, 0)),
                  pl.BlockSpec((1, d), lambda i: (0, 0)),
                  pl.BlockSpec(memory_space=pl.ANY)],
        out_specs=pl.BlockSpec((tk, d), lambda i: (i, 0)),
        out_shape=jax.ShapeDtypeStruct((t, d), _F32),
        scratch_shapes=[*[pltpu.VMEM((tk * _pitch(d), LANES), _F32) for _ in range(2 * MOE_TOPK)],
                        pltpu.SemaphoreType.DMA((2,))],
        compiler_params=_params(("arbitrary",)),
        name="combine",
    )(dest3, dest3, h1, w_cols, ada3, norm_g.reshape(1, d), y_rows)


def _rotary_tables(positions, dh):
    inv_freq = ROPE_THETA ** (-jnp.arange(0, dh, 2, dtype=_F32) / dh)
    ang = positions.astype(_F32)[..., None] * inv_freq
    cos, sin = jnp.cos(ang), jnp.sin(ang)
    t = cos.shape[0] * cos.shape[1]
    cos = jnp.concatenate([cos, cos], axis=-1).reshape(t, dh)
    sin = jnp.concatenate([-sin, sin], axis=-1).reshape(t, dh)
    return cos, sin


def _tile(n, want):
    while n % want:
        want //= 2
    return want


def kernel(x, c, positions, w_ada, b_ada, norm_mix_g, w_in, sgu_ln_g, sgu_ln_b, sgu_w_s, sgu_b_s,
           w_sgu_out, w_moba_out, w_out, norm_ffn_g, w_route_group, b_route_group, w_route_expert,
           b_route_expert, w_exp_gate, w_exp_up, w_exp_down, norm_final_g):
    batch, seq, d = x.shape
    depth = w_ada.shape[0]
    t = batch * seq
    sw = sgu_ln_g.shape[1]
    mw = w_moba_out.shape[1]
    dh = mw // MOBA_HEADS
    groups = w_route_group.shape[2]
    n_exp = w_route_expert.shape[2]
    epg = n_exp // groups
    rb = EXPERT_ROW_BLOCK
    cap = (t * MOE_TOPK // rb + n_exp) * rb
    assert depth == 1, "the final RMSNorm is fused into the single layer's combine"
    assert dh == LANES and seq % MOBA_BLOCK == 0 and sw * 2 == d and mw % sw == 0
    assert groups + n_exp <= ROUTER_ROWS and (t * MOE_TOPK) % rb == 0

    cos, sin = _rotary_tables(positions, dh)
    c_pad = jnp.zeros((8, d), _F32).at[:batch].set(c)
    h = x.reshape(t, d)

    for l in range(depth):
        ada = _ada(c_pad, w_ada[l], b_ada[l], _tile(6 * d, 1024))
        ada3 = ada[:batch].reshape(batch * 6, 1, d)

        proj = _inproj(h, ada3, norm_mix_g[l], w_in[l].astype(_BF16), cos, sin, sgu_ln_g[l], sgu_ln_b[l],
                       seq=seq, sw=sw, mw=mw, dh=dh, tm=_tile(seq, 1024))
        bs_wide = jnp.repeat(sgu_b_s[l].T, sw // sgu_w_s.shape[1], axis=1)
        m_a = _sgu(proj, sgu_w_s[l], bs_wide, w_sgu_out[l].astype(_BF16), sw=sw, mw=mw, tm=_tile(seq, 512))
        attn = _moba(proj, batch=batch, seq=seq, sw=sw, mw=mw, dh=dh, hg=min(MOBA_HEAD_GROUP, MOBA_HEADS))

        wr = jnp.concatenate([w_route_group[l], w_route_expert[l]], axis=1)
        wr_hi, wr_lo = _split_bf16(jnp.zeros((d, LANES), _F32).at[:, :groups + n_exp].set(wr))
        br = jnp.concatenate([b_route_group[l], b_route_expert[l].reshape(-1)])
        br = jnp.broadcast_to(jnp.zeros((ROUTER_ROWS,), _F32).at[:groups + n_exp].set(br)[:, None],
                              (ROUTER_ROWS, LANES))
        h1, n2, eid, wt = _mixout(attn, m_a, proj, h, ada3, norm_ffn_g[l], w_moba_out[l].astype(_BF16),
                                  w_out[l].astype(_BF16), wr_hi, wr_lo, br,
                                  seq=seq, sw=sw, mw=mw, groups=groups, epg=epg, tm=_tile(seq, 256))

        dest, tab = _plan(eid, n_exp=n_exp, tc=_tile(t, 512), rb=rb, n_blocks=cap // rb)
        def token_blocks(size):
            return dest.reshape(MOE_TOPK, t // size, size).transpose(1, 0, 2)

        inv = _invert(token_blocks(_tile(t, 2048)), cap)
        y_rows = _experts(tab[0, :cap // rb], tab[1, :1], inv.reshape(cap // rb, 1, rb), n2,
                          w_exp_gate[l], w_exp_up[l], w_exp_down[l])
        w_cols = jnp.zeros((t, LANES), _F32).at[:, :MOE_TOPK].set(wt.T)
        h = _combine(token_blocks(_tile(seq, 256)), h1, w_cols, ada3, norm_final_g, y_rows, seq=seq)

    return h.reshape(batch, seq, d)
```

```python
import functools

import jax
import jax.numpy as jnp
import numpy as np
from jax import lax
from jax.experimental import pallas as pl
from jax.experimental.pallas import tpu as pltpu

MOBA_HEADS = 16
MOBA_BLOCK = 256
MOBA_TOPK = 3
SGU_CHUNK = 128
ROPE_THETA = 10000.0
MOE_GROUPS = 4
MOE_TOPK = 2
EPS = 1e-6
NEG_INF = -1e30
LOG2_E = 1.4426950408889634

LANES = 128
TOKEN_PAD_ROWS = 8
EXPERT_ROW_BLOCK = 256
INPROJ_CHUNK = 256
NORM_BANDS = 4
MOBA_HEAD_GROUP = 4
ROUTER_ROWS = 40
VMEM_LIMIT = 56 * 1024 * 1024

_F32 = jnp.float32
_BF16 = jnp.bfloat16
_NT = (((1,), (1,)), ((), ()))


def _params(semantics, vmem=VMEM_LIMIT):
    return pltpu.CompilerParams(dimension_semantics=semantics, vmem_limit_bytes=vmem)


def _pitch(d):
    return d // LANES + TOKEN_PAD_ROWS


def _store_token_major(ref, val, zero_pad=True):
    n, d = val.shape
    pitch = ref.shape[0] // n
    for c in range(pitch if zero_pad else d // LANES):
        piece = val[:, c * LANES:(c + 1) * LANES] if c < d // LANES else jnp.zeros((n, LANES), val.dtype)
        ref[pl.ds(c, n, stride=pitch), :] = piece


def _load_token_major(ref, n, d, first=0, pitch=None):
    pitch = pitch or ref.shape[0] // n
    return jnp.concatenate([ref[pl.ds(first + c, n, stride=pitch), :] for c in range(d // LANES)], axis=1)


def _split_bf16(a):
    hi = a.astype(_BF16)
    lo = (a - hi.astype(_F32)).astype(_BF16)
    return hi, lo


def _ada_kernel(c_ref, w_ref, b_ref, o_ref):
    ca_hi, ca_lo = _split_bf16(jax.nn.silu(c_ref[...]))
    w_hi, w_lo = _split_bf16(w_ref[...])
    acc = jnp.dot(ca_hi, w_hi, preferred_element_type=_F32)
    acc += jnp.dot(ca_hi, w_lo, preferred_element_type=_F32)
    acc += jnp.dot(ca_lo, w_hi, preferred_element_type=_F32)
    o_ref[...] = acc + b_ref[...]


def _ada(c_pad, w_ada, b_ada, tn):
    rows, d = c_pad.shape
    n = w_ada.shape[1]
    return pl.pallas_call(
        _ada_kernel,
        grid=(n // tn,),
        in_specs=[pl.BlockSpec((rows, d), lambda j: (0, 0)),
                  pl.BlockSpec((d, tn), lambda j: (0, j)),
                  pl.BlockSpec((1, tn), lambda j: (0, j))],
        out_specs=pl.BlockSpec((rows, tn), lambda j: (0, j)),
        out_shape=jax.ShapeDtypeStruct((rows, n), _F32),
        compiler_params=_params(("arbitrary",)),
        name="ada",
    )(c_pad, w_ada, b_ada.reshape(1, n))


def _rms_modulate(x, g, sc, sh):
    y = x * lax.rsqrt(jnp.mean(x * x, axis=-1, keepdims=True) + EPS) * g
    return y * (1.0 + sc) + sh


def _inproj_kernel(x_ref, g_ref, sc_ref, sh_ref, w_ref, cos_ref, sin_ref, lng_ref, lnb_ref,
                   o_ref, n_scr, z_scr, *, q0, k0, v0, dh, scale, cw):
    j = pl.program_id(1)
    tm, tn = o_ref.shape

    def chunk(c):
        return jnp.dot(n_scr[...], w_ref[:, c * cw:(c + 1) * cw], preferred_element_type=_F32)

    @pl.when(j == 0)
    def _():
        band = tm // NORM_BANDS
        for r in range(NORM_BANDS):
            rows = slice(r * band, (r + 1) * band)
            nb = _rms_modulate(x_ref[rows, :], g_ref[...], sc_ref[0], sh_ref[0]).astype(_BF16)
            n_scr[rows, :] = nb
            for c in range(tn // cw):
                cols = slice(c * cw, (c + 1) * cw)
                o_ref[rows, cols] = jax.nn.gelu(
                    jnp.dot(nb, w_ref[:, cols], preferred_element_type=_F32)).astype(_BF16)

    @pl.when(j == 1)
    def _():
        for c in range(tn // cw):
            z_scr[:, c * cw:(c + 1) * cw] = jax.nn.gelu(chunk(c))
        z = z_scr[...]
        zc = z - jnp.mean(z, axis=-1, keepdims=True)
        var = jnp.mean(zc * zc, axis=-1, keepdims=True)
        o_ref[...] = (zc * lax.rsqrt(var + EPS) * lng_ref[...] + lnb_ref[...]).astype(_BF16)

    @pl.when(j >= q0)
    def _():
        mult = jnp.where(j < k0, scale, 1.0)
        cos = jnp.where(j < v0, cos_ref[...], 1.0) * mult
        sin = jnp.where(j < v0, sin_ref[...], 0.0) * mult
        for c in range(tn // cw):
            acc = chunk(c)
            for h in range(cw // dh):
                xh = acc[:, h * dh:(h + 1) * dh]
                rot = xh * cos + pltpu.roll(xh, dh // 2, 1) * sin
                o_ref[:, c * cw + h * dh:c * cw + (h + 1) * dh] = rot.astype(_BF16)


def _inproj(x2, ada3, norm_g, w_in, cos, sin, ln_g, ln_b, *, seq, sw, mw, dh, tm):
    t, d = x2.shape
    n = w_in.shape[1]
    tn = sw
    per_b = seq // tm
    q0 = 2
    k0 = q0 + mw // tn
    v0 = k0 + mw // tn
    cw = _tile(tn, INPROJ_CHUNK)
    assert cw % dh == 0
    kern = functools.partial(_inproj_kernel, q0=q0, k0=k0, v0=v0, dh=dh, scale=dh ** -0.5 * LOG2_E, cw=cw)
    return pl.pallas_call(
        kern,
        grid=(t // tm, n // tn),
        in_specs=[pl.BlockSpec((tm, d), lambda i, j: (i, 0)),
                  pl.BlockSpec((1, d), lambda i, j: (0, 0)),
                  pl.BlockSpec((1, 1, d), lambda i, j: ((i // per_b) * 6 + 1, 0, 0)),
                  pl.BlockSpec((1, 1, d), lambda i, j: ((i // per_b) * 6 + 0, 0, 0)),
                  pl.BlockSpec((d, tn), lambda i, j: (0, j)),
                  pl.BlockSpec((tm, dh), lambda i, j: (i, 0)),
                  pl.BlockSpec((tm, dh), lambda i, j: (i, 0)),
                  pl.BlockSpec((1, sw), lambda i, j: (0, 0)),
                  pl.BlockSpec((1, sw), lambda i, j: (0, 0))],
        out_specs=pl.BlockSpec((tm, tn), lambda i, j: (i, j)),
        out_shape=jax.ShapeDtypeStruct((t, n), _BF16),
        scratch_shapes=[pltpu.VMEM((tm, d), _BF16), pltpu.VMEM((tm, tn), _F32)],
        compiler_params=_params(("arbitrary", "arbitrary")),
        name="inproj",
    )(x2, norm_g.reshape(1, d), ada3, ada3, w_in, cos, sin, ln_g.reshape(1, sw), ln_b.reshape(1, sw))


def _sgu_kernel(u_ref, v_ref, ga_ref, ws_ref, bs_ref, wo_ref, o_ref, gated_scr, *, groups, cg):
    c = SGU_CHUNK
    tri = lax.broadcasted_iota(jnp.int32, (c, c), 0) >= lax.broadcasted_iota(jnp.int32, (c, c), 1)
    for g in range(groups):
        wg = jnp.where(tri, ws_ref[g], 0.0).astype(_BF16)
        cols = slice(g * cg, (g + 1) * cg)
        for ci in range(u_ref.shape[0] // c):
            rows = slice(ci * c, (ci + 1) * c)
            sv = jnp.dot(wg, v_ref[rows, cols], preferred_element_type=_F32) + bs_ref[:, cols]
            gated_scr[rows, cols] = (u_ref[rows, cols].astype(_F32) * sv).astype(_BF16)
    ya = jnp.dot(gated_scr[...], wo_ref[...], preferred_element_type=_F32)
    o_ref[...] = (jax.nn.sigmoid(ga_ref[...].astype(_F32)) * ya).astype(_BF16)


def _sgu(proj, w_s, bs_wide, w_sgu_out, *, sw, mw, tm):
    t = proj.shape[0]
    d = w_sgu_out.shape[1]
    groups = w_s.shape[0]
    ga_blk = (2 * sw + 3 * mw) // d
    kern = functools.partial(_sgu_kernel, groups=groups, cg=sw // groups)
    return pl.pallas_call(
        kern,
        grid=(t // tm,),
        in_specs=[pl.BlockSpec((tm, sw), lambda i: (i, 0)),
                  pl.BlockSpec((tm, sw), lambda i: (i, 1)),
                  pl.BlockSpec((tm, d), lambda i: (i, ga_blk)),
                  pl.BlockSpec(w_s.shape, lambda i: (0, 0, 0)),
                  pl.BlockSpec(bs_wide.shape, lambda i: (0, 0)),
                  pl.BlockSpec(w_sgu_out.shape, lambda i: (0, 0))],
        out_specs=pl.BlockSpec((tm, d), lambda i: (i, 0)),
        out_shape=jax.ShapeDtypeStruct((t, d), _BF16),
        scratch_shapes=[pltpu.VMEM((tm, sw), _BF16)],
        compiler_params=_params(("arbitrary",)),
        name="sgu",
    )(proj, proj, proj, w_s, bs_wide, w_sgu_out)


def _moba_kernel(q_ref, k_ref, v_ref, o_ref, kmh_scr, kml_scr, vt_scr, sel_scr,
                 p_scr, acc_scr, m_scr, l_scr, a_scr, *, nb, topk, hg, dh):
    qi = pl.program_id(2)
    blk = MOBA_BLOCK
    tq = q_ref.shape[0]

    @pl.when(qi == 0)
    def _():
        for h in range(hg):
            cols = slice(h * dh, (h + 1) * dh)
            kf = k_ref[:, cols].astype(_F32).reshape(nb, blk, dh)
            km_hi, km_lo = _split_bf16(jnp.mean(kf, axis=1))
            kmh_scr[h * nb:(h + 1) * nb, :] = km_hi
            kml_scr[h * nb:(h + 1) * nb, :] = km_lo
            for c in range(nb):
                rows = slice(c * blk, (c + 1) * blk)
                vt_scr[cols, rows] = v_ref[rows, cols].astype(_F32).T.astype(_BF16)

    row = lax.broadcasted_iota(jnp.int32, (nb, tq), 0)
    past = row < qi
    kpos = lax.broadcasted_iota(jnp.int32, (blk, tq), 0)
    qpos = lax.broadcasted_iota(jnp.int32, (blk, tq), 1)
    own = pl.multiple_of(qi * blk, blk)

    def scores(start, h):
        cols = slice(h * dh, (h + 1) * dh)
        return lax.dot_general(k_ref[pl.ds(start, blk), cols], q_ref[:, cols], _NT,
                               preferred_element_type=_F32)

    def weighted_values(start, h, p):
        return jnp.dot(vt_scr[h * dh:(h + 1) * dh, pl.ds(start, blk)], p, preferred_element_type=_F32)

    for h in range(hg):
        hrows = slice(h * nb, (h + 1) * nb)
        q = q_ref[:, h * dh:(h + 1) * dh]
        gate = (lax.dot_general(kmh_scr[hrows, :], q, _NT, preferred_element_type=_F32)
                + lax.dot_general(kml_scr[hrows, :], q, _NT, preferred_element_type=_F32))
        gm = jnp.where(past, gate, NEG_INF)
        beaten = jnp.zeros((nb, tq), jnp.int32)
        for jp in range(nb):
            other = gm[jp:jp + 1, :]
            wins = (other > gm) | ((other == gm) & (jp < row))
            beaten += wins.astype(jnp.int32)
        sel_scr[hrows, :] = (past & (beaten < topk)).astype(_F32)

        s = jnp.where(kpos <= qpos, scores(own, h), NEG_INF)
        m0 = jnp.max(s, axis=0, keepdims=True)
        p = jnp.exp2(s - m0)
        m_scr[h] = m0
        l_scr[h] = jnp.sum(p, axis=0, keepdims=True)
        a_scr[h] = jnp.ones_like(m0)
        acc_scr[h] = jnp.zeros(acc_scr.shape[1:], _F32)
        p_scr[0, h] = p.astype(_BF16)

    def body(j, c):
        par = lax.rem(j, 2)
        prev = pl.multiple_of(jnp.where(j == 0, qi, j - 1) * blk, blk)
        for h in range(hg):
            acc_scr[h] = a_scr[h] * acc_scr[h] + weighted_values(prev, h, p_scr[par, h])
        cur = pl.multiple_of(j * blk, blk)
        for h in range(hg):
            s = jnp.where(sel_scr[pl.ds(h * nb + j, 1), :] > 0.0, scores(cur, h), NEG_INF)
            m = m_scr[h]
            m_new = jnp.maximum(m, jnp.max(s, axis=0, keepdims=True))
            alpha = jnp.exp2(m - m_new)
            p = jnp.exp2(s - m_new)
            l_scr[h] = alpha * l_scr[h] + jnp.sum(p, axis=0, keepdims=True)
            m_scr[h] = m_new
            a_scr[h] = alpha
            p_scr[1 - par, h] = p.astype(_BF16)
        return c

    lax.fori_loop(0, qi, body, 0)
    last = pl.multiple_of(jnp.where(qi == 0, qi, qi - 1) * blk, blk)
    for h in range(hg):
        acc = a_scr[h] * acc_scr[h] + weighted_values(last, h, p_scr[lax.rem(qi, 2), h])
        o_ref[:, h * dh:(h + 1) * dh] = (acc / l_scr[h]).T.astype(_BF16)


def _moba(proj, *, batch, seq, sw, mw, dh, hg):
    t = proj.shape[0]
    heads = mw // dh
    nb = seq // MOBA_BLOCK
    tq = MOBA_BLOCK
    nq = seq // tq
    gw = hg * dh
    qc = 2 * sw // gw
    kc = qc + heads // hg
    vc = kc + heads // hg
    kern = functools.partial(_moba_kernel, nb=nb, topk=MOBA_TOPK, hg=hg, dh=dh)
    return pl.pallas_call(
        kern,
        grid=(batch, heads // hg, nq),
        in_specs=[pl.BlockSpec((tq, gw), lambda b, h, i: (b * nq + i, qc + h)),
                  pl.BlockSpec((seq, gw), lambda b, h, i: (b, kc + h)),
                  pl.BlockSpec((seq, gw), lambda b, h, i: (b, vc + h))],
        out_specs=pl.BlockSpec((tq, gw), lambda b, h, i: (b * nq + i, h)),
        out_shape=jax.ShapeDtypeStruct((t, mw), _BF16),
        scratch_shapes=[pltpu.VMEM((hg * nb, dh), _BF16), pltpu.VMEM((hg * nb, dh), _BF16),
                        pltpu.VMEM((gw, seq), _BF16), pltpu.VMEM((hg * nb, tq), _F32),
                        pltpu.VMEM((2, hg, MOBA_BLOCK, tq), _BF16),
                        pltpu.VMEM((hg, dh, tq), _F32), pltpu.VMEM((hg, 1, tq), _F32),
                        pltpu.VMEM((hg, 1, tq), _F32), pltpu.VMEM((hg, 1, tq), _F32)],
        compiler_params=_params(("arbitrary", "arbitrary", "arbitrary")),
        name="moba",
    )(proj, proj, proj)


def _mixout_kernel(attn_ref, ma_ref, gb_ref, x_ref, gm_ref, shf_ref, scf_ref, ng_ref,
                   wmo_ref, wo_ref, wrh_ref, wrl_ref, br_ref,
                   h_ref, n2_ref, eid_ref, wt_ref, *, groups, epg):
    yb = jnp.dot(attn_ref[...], wmo_ref[...], preferred_element_type=_F32)
    merged = ma_ref[...].astype(_F32) + jax.nn.sigmoid(gb_ref[...].astype(_F32)) * yb
    mix = jnp.dot(merged.astype(_BF16), wo_ref[...], preferred_element_type=_F32)
    h = x_ref[...] + gm_ref[0] * mix
    h_ref[...] = h
    n2 = _rms_modulate(h, ng_ref[...], scf_ref[0], shf_ref[0])
    _store_token_major(n2_ref, n2)

    n_hi, n_lo = _split_bf16(n2)
    lg = (jnp.dot(n_hi, wrh_ref[...], preferred_element_type=_F32)
          + jnp.dot(n_lo, wrh_ref[...], preferred_element_type=_F32)
          + jnp.dot(n_hi, wrl_ref[...], preferred_element_type=_F32))
    lg = lg.T[:ROUTER_ROWS, :] + br_ref[:, 0:1]

    gl = [lg[g:g + 1, :] for g in range(groups)]
    gmax = functools.reduce(jnp.maximum, gl)
    denom = functools.reduce(jnp.add, [jnp.exp(v - gmax) for v in gl])
    p_group = 1.0 / denom
    g_sel = jnp.full(gmax.shape, groups - 1, jnp.int32)
    for g in range(groups - 2, -1, -1):
        g_sel = jnp.where(gl[g] == gmax, g, g_sel)

    el = []
    for e in range(epg):
        v = lg[groups + e:groups + e + 1, :]
        for g in range(1, groups):
            r = groups + g * epg + e
            v = jnp.where(g_sel == g, lg[r:r + 1, :], v)
        el.append(v)

    def top1(vals):
        vmax = functools.reduce(jnp.maximum, vals)
        idx = jnp.full(vmax.shape, epg - 1, jnp.int32)
        for e in range(epg - 2, -1, -1):
            idx = jnp.where(vals[e] == vmax, e, idx)
        return vmax, idx

    v1, i1 = top1(el)
    v2, i2 = top1([jnp.where(i1 == e, -jnp.inf, el[e]) for e in range(epg)])
    b = jnp.exp(v2 - v1)
    eid_ref[0:1, :] = g_sel * epg + i1
    eid_ref[1:2, :] = g_sel * epg + i2
    wt_ref[0:1, :] = (1.0 / (1.0 + b)) * p_group
    wt_ref[1:2, :] = (b / (1.0 + b)) * p_group


def _mixout(attn, m_a, proj, x2, ada3, norm_g, w_moba_out, w_out, wr_hi, wr_lo, br,
            *, seq, sw, mw, groups, epg, tm):
    t, d = x2.shape
    per_b = seq // tm
    gb_blk = (2 * sw + 3 * mw) // d + 1
    once = pl.Buffered(1)
    kern = functools.partial(_mixout_kernel, groups=groups, epg=epg)

    def ada_spec(k):
        return pl.BlockSpec((1, 1, d), lambda i: ((i // per_b) * 6 + k, 0, 0))

    return pl.pallas_call(
        kern,
        grid=(t // tm,),
        in_specs=[pl.BlockSpec((tm, mw), lambda i: (i, 0)),
                  pl.BlockSpec((tm, d), lambda i: (i, 0)),
                  pl.BlockSpec((tm, d), lambda i: (i, gb_blk)),
                  pl.BlockSpec((tm, d), lambda i: (i, 0)),
                  ada_spec(2), ada_spec(3), ada_spec(4),
                  pl.BlockSpec((1, d), lambda i: (0, 0)),
                  pl.BlockSpec((mw, d), lambda i: (0, 0), pipeline_mode=once),
                  pl.BlockSpec((d, d), lambda i: (0, 0), pipeline_mode=once),
                  pl.BlockSpec((d, LANES), lambda i: (0, 0)),
                  pl.BlockSpec((d, LANES), lambda i: (0, 0)),
                  pl.BlockSpec((ROUTER_ROWS, LANES), lambda i: (0, 0))],
        out_specs=[pl.BlockSpec((tm, d), lambda i: (i, 0)),
                   pl.BlockSpec((tm * _pitch(d), LANES), lambda i: (i, 0)),
                   pl.BlockSpec((MOE_TOPK, tm), lambda i: (0, i)),
                   pl.BlockSpec((MOE_TOPK, tm), lambda i: (0, i))],
        out_shape=[jax.ShapeDtypeStruct((t, d), _F32),
                   jax.ShapeDtypeStruct((t * _pitch(d), LANES), _F32),
                   jax.ShapeDtypeStruct((MOE_TOPK, t), jnp.int32),
                   jax.ShapeDtypeStruct((MOE_TOPK, t), _F32)],
        compiler_params=_params(("arbitrary",)),
        name="mixout",
    )(attn, m_a, proj, x2, ada3, ada3, ada3, norm_g.reshape(1, d), w_moba_out, w_out, wr_hi, wr_lo, br)


def _plan_kernel(eid_ref, dest_ref, tab_ref, carry_scr, start_scr, tri_scr, *, n_exp, rb):
    phase = pl.program_id(0)
    first = pl.program_id(1) == 0
    tc = eid_ref.shape[1]
    nbp = tab_ref.shape[1]

    @pl.when((phase == 0) & first)
    def _():
        carry_scr[...] = jnp.zeros_like(carry_scr)
        earlier = lax.broadcasted_iota(jnp.int32, (tc, tc), 0) < lax.broadcasted_iota(jnp.int32, (tc, tc), 1)
        tri_scr[...] = earlier.astype(_BF16)

    eio = lax.broadcasted_iota(jnp.int32, (n_exp, tc), 0)
    oh0 = (eio == eid_ref[0:1, :]).astype(_F32)
    oh1 = (eio == eid_ref[1:2, :]).astype(_F32)
    oh = oh0 + oh1

    @pl.when((phase == 1) & first)
    def _():
        counts = carry_scr[...].astype(jnp.int32)
        blocks = lax.shift_right_logical(counts + (rb - 1), rb.bit_length() - 1).astype(_F32)
        upto = lax.broadcasted_iota(jnp.int32, (n_exp, n_exp), 0) >= lax.broadcasted_iota(jnp.int32, (n_exp, n_exp), 1)
        end_blk = jnp.dot(upto.astype(_BF16), blocks.astype(_BF16), preferred_element_type=_F32)
        start_scr[...] = (end_blk - blocks) * rb
        carry_scr[...] = jnp.zeros_like(carry_scr)
        blk_id = lax.broadcasted_iota(jnp.int32, (n_exp, nbp), 1).astype(_F32)
        owner = jnp.sum((end_blk[:, 0:1] <= blk_id).astype(_F32), axis=0, keepdims=True)
        owner = jnp.minimum(owner, n_exp - 1.0)
        used = jnp.broadcast_to(end_blk[n_exp - 1:n_exp, 0:1], (1, nbp))
        trow = lax.broadcasted_iota(jnp.int32, tab_ref.shape, 0)
        tab_ref[...] = jnp.where(trow == 0, owner, jnp.where(trow == 1, used, 0.0)).astype(jnp.int32)

    @pl.when(phase == 1)
    def _():
        before = jnp.dot(oh.astype(_BF16), tri_scr[...], preferred_element_type=_F32)
        base = start_scr[:, 0:1] + carry_scr[:, 0:1] + before
        dest_ref[0:1, :] = jnp.sum(oh0 * base, axis=0, keepdims=True).astype(jnp.int32)
        dest_ref[1:2, :] = jnp.sum(oh1 * base, axis=0, keepdims=True).astype(jnp.int32)

    carry_scr[...] = carry_scr[...] + jnp.sum(oh, axis=1, keepdims=True)


def _plan(eid, *, n_exp, tc, rb, n_blocks):
    t = eid.shape[1]
    assert rb & (rb - 1) == 0 and t * MOE_TOPK // rb + n_exp < 256
    nbp = -(-n_blocks // LANES) * LANES
    kern = functools.partial(_plan_kernel, n_exp=n_exp, rb=rb)
    return pl.pallas_call(
        kern,
        grid=(2, t // tc),
        in_specs=[pl.BlockSpec((MOE_TOPK, tc), lambda p, i: (0, i))],
        out_specs=[pl.BlockSpec((MOE_TOPK, tc), lambda p, i: (0, i * p)),
                   pl.BlockSpec((8, nbp), lambda p, i: (0, 0))],
        out_shape=[jax.ShapeDtypeStruct((MOE_TOPK, t), jnp.int32),
                   jax.ShapeDtypeStruct((8, nbp), jnp.int32)],
        scratch_shapes=[pltpu.VMEM((n_exp, LANES), _F32), pltpu.VMEM((n_exp, LANES), _F32),
                        pltpu.VMEM((tc, tc), _BF16)],
        compiler_params=_params(("arbitrary", "arbitrary")),
        name="plan",
    )(eid)


def _invert_kernel(dest_ref, pad_ref, inv_ref, sem):
    i = pl.program_id(0)
    tch = dest_ref.shape[1]

    @pl.when(i == 0)
    def _():
        fill = pltpu.make_async_copy(pad_ref, inv_ref, sem)
        fill.start()
        fill.wait()

    def body(t, c):
        a = (i * tch + t) * MOE_TOPK
        for k in range(MOE_TOPK):
            inv_ref[dest_ref[k, t]] = a + k
        return c

    lax.fori_loop(0, tch, body, 0, unroll=8)


def _invert(dest3, cap):
    steps, _, tch = dest3.shape
    pad = np.zeros((cap,), np.int32)
    return pl.pallas_call(
        _invert_kernel,
        grid=(steps,),
        in_specs=[pl.BlockSpec((None, MOE_TOPK, tch), lambda i: (i, 0, 0), memory_space=pltpu.SMEM),
                  pl.BlockSpec(memory_space=pl.ANY)],
        out_specs=pl.BlockSpec(pad.shape, lambda i: (0,), memory_space=pltpu.SMEM),
        out_shape=jax.ShapeDtypeStruct(pad.shape, jnp.int32),
        scratch_shapes=[pltpu.SemaphoreType.DMA(())],
        compiler_params=_params(("arbitrary",)),
        name="invert",
    )(dest3, jnp.asarray(pad))


def _aligned_rows(chunk, rows_per_chunk):
    row = chunk * rows_per_chunk
    return pl.multiple_of(row, 8) if rows_per_chunk % 8 == 0 else row


def _expert_kernel(be_ref, nu_ref, invc_ref, invn_ref, n2_ref, wg_ref, wu_ref, wd_ref, y_ref,
                   wg_s, wu_s, wd_s, xbuf0, xbuf1, gsem, *, rb, d):
    i = pl.program_id(0)
    last = nu_ref[0] - 1
    used = i <= last
    fresh = (i == 0) | (be_ref[i] != be_ref[jnp.maximum(i - 1, 0)])
    ch = d // LANES
    pitch = _pitch(d)
    xbuf = (xbuf0, xbuf1)

    def gather(inv_ref, p):
        for r in range(rb):
            tok = lax.shift_right_logical(inv_ref[0, r], 1)
            pltpu.make_async_copy(n2_ref.at[pl.ds(_aligned_rows(tok, pitch), ch)],
                                  xbuf[p].at[pl.ds(r * pitch, ch)], gsem.at[p]).start(priority=r % 2)

    def gather_wait(p):
        pltpu.make_async_copy(n2_ref.at[pl.ds(0, rb * ch)], xbuf[p].at[pl.ds(0, rb * ch)], gsem.at[p]).wait()

    @pl.when(i == 0)
    def _():
        gather(invc_ref, 0)

    @pl.when(used & fresh)
    def _():
        wg_s[...] = wg_ref[0].astype(_BF16)
        wu_s[...] = wu_ref[0].astype(_BF16)
        wd_s[...] = wd_ref[0].astype(_BF16)

    def step(p):
        gather_wait(p)
        gather(invn_ref, 1 - p)
        xb = _load_token_major(xbuf[p], rb, d).astype(_BF16)
        gate = jnp.dot(xb, wg_s[...], preferred_element_type=_F32)
        up = jnp.dot(xb, wu_s[...], preferred_element_type=_F32)
        hid = (jax.nn.silu(gate) * up).astype(_BF16)
        _store_token_major(y_ref, jnp.dot(hid, wd_s[...], preferred_element_type=_F32))

        @pl.when(i == last)
        def _():
            gather_wait(1 - p)

    for p in range(2):
        pl.when(used & (lax.rem(i, 2) == p))(functools.partial(step, p))

    @pl.when(jnp.logical_not(used))
    def _():
        y_ref[...] = jnp.zeros_like(y_ref)


def _experts(block_expert, n_used, inv3, n2, w_gate, w_up, w_down):
    nblk, _, rb = inv3.shape
    d, f = w_gate.shape[1:]
    ch = d // LANES

    def inv_spec(shift):
        return pl.BlockSpec((None, 1, rb), lambda i, be, nu: (jnp.clip(i + shift, 0, nu[0] - 1), 0, 0),
                            memory_space=pltpu.SMEM)

    grid_spec = pltpu.PrefetchScalarGridSpec(
        num_scalar_prefetch=2,
        grid=(nblk,),
        in_specs=[inv_spec(0), inv_spec(1),
                  pl.BlockSpec(memory_space=pl.ANY),
                  pl.BlockSpec((1, d, f), lambda i, be, nu: (be[i], 0, 0)),
                  pl.BlockSpec((1, d, f), lambda i, be, nu: (be[i], 0, 0)),
                  pl.BlockSpec((1, f, d), lambda i, be, nu: (be[i], 0, 0))],
        out_specs=pl.BlockSpec((rb * ch, LANES), lambda i, be, nu: (i, 0)),
        scratch_shapes=[pltpu.VMEM((d, f), _BF16), pltpu.VMEM((d, f), _BF16), pltpu.VMEM((f, d), _BF16),
                        pltpu.VMEM((rb * _pitch(d), LANES), _F32), pltpu.VMEM((rb * _pitch(d), LANES), _F32),
                        pltpu.SemaphoreType.DMA((2,))],
    )
    kern = functools.partial(_expert_kernel, rb=rb, d=d)
    return pl.pallas_call(
        kern,
        grid_spec=grid_spec,
        out_shape=jax.ShapeDtypeStruct((nblk * rb * ch, LANES), _F32),
        compiler_params=_params(("arbitrary",)),
        name="experts",
    )(block_expert, n_used, inv3, inv3, n2, w_gate, w_up, w_down)


def _combine_kernel(destc_ref, destn_ref, h_ref, wc_ref, gf_ref, ng_ref, y_ref, o_ref,
                    b00, b01, b10, b11, sem):
    i = pl.program_id(0)
    tk, d = h_ref.shape
    ch = d // LANES
    pitch = b00.shape[0] // tk
    bufs = ((b00, b01), (b10, b11))

    def gather(dest_ref, p):
        for r in range(tk):
            for k in range(MOE_TOPK):
                pltpu.make_async_copy(y_ref.at[pl.ds(_aligned_rows(dest_ref[k, r], ch), ch)],
                                      bufs[p][k].at[pl.ds(r * pitch, ch)], sem.at[p]).start(priority=r % 2)

    def gather_wait(p):
        for k in range(MOE_TOPK):
            pltpu.make_async_copy(y_ref.at[pl.ds(0, tk * ch)], bufs[p][k].at[pl.ds(0, tk * ch)], sem.at[p]).wait()

    @pl.when(i == 0)
    def _():
        gather(destc_ref, 0)

    def step(p):
        gather_wait(p)
        gather(destn_ref, 1 - p)
        y = [_load_token_major(bufs[p][k], tk, d) for k in range(MOE_TOPK)]
        moe = wc_ref[:, 0:1] * y[0] + wc_ref[:, 1:2] * y[1]
        h = h_ref[...] + gf_ref[0] * moe
        o_ref[...] = h * lax.rsqrt(jnp.mean(h * h, axis=-1, keepdims=True) + EPS) * ng_ref[...]

        @pl.when(i == pl.num_programs(0) - 1)
        def _():
            gather_wait(1 - p)

    for p in range(2):
        pl.when(lax.rem(i, 2) == p)(functools.partial(step, p))


def _combine(dest3, h1, w_cols, ada3, norm_g, y_rows, *, seq):
    t, d = h1.shape
    steps, _, tk = dest3.shape
    per_b = seq // tk

    def dest_spec(shift):
        return pl.BlockSpec((None, MOE_TOPK, tk), lambda i: (jnp.minimum(i + shift, steps - 1), 0, 0),
                            memory_space=pltpu.SMEM)

    return pl.pallas_call(
        _combine_kernel,
        grid=(steps,),
        in_specs=[dest_spec(0), dest_spec(1),
                  pl.BlockSpec((tk, d), lambda i: (i, 0)),
                  pl.BlockSpec((tk, LANES), lambda i: (i, 0)),
                  pl.BlockSpec((1, 1, d), lambda i: ((i // per_b) * 6 + 5, 0, 0)),
                  pl.BlockSpec((1, d), lambda i: (0, 0)),
                  pl.BlockSpec(memory_space=pl.ANY)],
        out_specs=pl.BlockSpec((tk, d), lambda i: (i, 0)),
        out_shape=jax.ShapeDtypeStruct((t, d), _F32),
        scratch_shapes=[*[pltpu.VMEM((tk * _pitch(d), LANES), _F32) for _ in range(2 * MOE_TOPK)],
                        pltpu.SemaphoreType.DMA((2,))],
        compiler_params=_params(("arbitrary",)),
        name="combine",
    )(dest3, dest3, h1, w_cols, ada3, norm_g.reshape(1, d), y_rows)


def _rotary_tables(positions, dh):
    inv_freq = ROPE_THETA ** (-jnp.arange(0, dh, 2, dtype=_F32) / dh)
    ang = positions.astype(_F32)[..., None] * inv_freq
    cos, sin = jnp.cos(ang), jnp.sin(ang)
    t = cos.shape[0] * cos.shape[1]
    cos = jnp.concatenate([cos, cos], axis=-1).reshape(t, dh)
    sin = jnp.concatenate([-sin, sin], axis=-1).reshape(t, dh)
    return cos, sin


def _tile(n, want):
    while n % want:
        want //= 2
    return want


def kernel(x, c, positions, w_ada, b_ada, norm_mix_g, w_in, sgu_ln_g, sgu_ln_b, sgu_w_s, sgu_b_s,
           w_sgu_out, w_moba_out, w_out, norm_ffn_g, w_route_group, b_route_group, w_route_expert,
           b_route_expert, w_exp_gate, w_exp_up, w_exp_down, norm_final_g):
    batch, seq, d = x.shape
    depth = w_ada.shape[0]
    t = batch * seq
    sw = sgu_ln_g.shape[1]
    mw = w_moba_out.shape[1]
    dh = mw // MOBA_HEADS
    groups = w_route_group.shape[2]
    n_exp = w_route_expert.shape[2]
    epg = n_exp // groups
    rb = EXPERT_ROW_BLOCK
    cap = (t * MOE_TOPK // rb + n_exp) * rb
    assert depth == 1, "the final RMSNorm is fused into the single layer's combine"
    assert dh == LANES and seq % MOBA_BLOCK == 0 and sw * 2 == d and mw % sw == 0
    assert groups + n_exp <= ROUTER_ROWS and (t * MOE_TOPK) % rb == 0

    cos, sin = _rotary_tables(positions, dh)
    c_pad = jnp.zeros((8, d), _F32).at[:batch].set(c)
    h = x.reshape(t, d)

    for l in range(depth):
        ada = _ada(c_pad, w_ada[l], b_ada[l], _tile(6 * d, 1024))
        ada3 = ada[:batch].reshape(batch * 6, 1, d)

        proj = _inproj(h, ada3, norm_mix_g[l], w_in[l].astype(_BF16), cos, sin, sgu_ln_g[l], sgu_ln_b[l],
                       seq=seq, sw=sw, mw=mw, dh=dh, tm=_tile(seq, 1024))
        bs_wide = jnp.repeat(sgu_b_s[l].T, sw // sgu_w_s.shape[1], axis=1)
        m_a = _sgu(proj, sgu_w_s[l], bs_wide, w_sgu_out[l].astype(_BF16), sw=sw, mw=mw, tm=_tile(seq, 512))
        attn = _moba(proj, batch=batch, seq=seq, sw=sw, mw=mw, dh=dh, hg=min(MOBA_HEAD_GROUP, MOBA_HEADS))

        wr = jnp.concatenate([w_route_group[l], w_route_expert[l]], axis=1)
        wr_hi, wr_lo = _split_bf16(jnp.zeros((d, LANES), _F32).at[:, :groups + n_exp].set(wr))
        br = jnp.concatenate([b_route_group[l], b_route_expert[l].reshape(-1)])
        br = jnp.broadcast_to(jnp.zeros((ROUTER_ROWS,), _F32).at[:groups + n_exp].set(br)[:, None],
                              (ROUTER_ROWS, LANES))
        h1, n2, eid, wt = _mixout(attn, m_a, proj, h, ada3, norm_ffn_g[l], w_moba_out[l].astype(_BF16),
                                  w_out[l].astype(_BF16), wr_hi, wr_lo, br,
                                  seq=seq, sw=sw, mw=mw, groups=groups, epg=epg, tm=_tile(seq, 256))

        dest, tab = _plan(eid, n_exp=n_exp, tc=_tile(t, 512), rb=rb, n_blocks=cap // rb)
        def token_blocks(size):
            return dest.reshape(MOE_TOPK, t // size, size).transpose(1, 0, 2)

        inv = _invert(token_blocks(_tile(t, 2048)), cap)
        y_rows = _experts(tab[0, :cap // rb], tab[1, :1], inv.reshape(cap // rb, 1, rb), n2,
                          w_exp_gate[l], w_exp_up[l], w_exp_down[l])
        w_cols = jnp.zeros((t, LANES), _F32).at[:, :MOE_TOPK].set(wt.T)
        h = _combine(token_blocks(_tile(seq, 256)), h1, w_cols, ada3, norm_final_g, y_rows, seq=seq)

    return h.reshape(batch, seq, d)
```

```python
import functools

import jax
import jax.numpy as jnp
import numpy as np
from jax import lax
from jax.experimental import pallas as pl
from jax.experimental.pallas import tpu as pltpu

MOBA_HEADS = 16
MOBA_BLOCK = 256
MOBA_TOPK = 3
SGU_CHUNK = 128
ROPE_THETA = 10000.0
MOE_GROUPS = 4
MOE_TOPK = 2
EPS = 1e-6
NEG_INF = -1e30
LOG2_E = 1.4426950408889634

LANES = 128
TOKEN_PAD_ROWS = 8
EXPERT_ROW_BLOCK = 256
INPROJ_CHUNK = 256
NORM_BANDS = 4
MOBA_HEAD_GROUP = 8
ROUTER_ROWS = 40
VMEM_LIMIT = 56 * 1024 * 1024

_F32 = jnp.float32
_BF16 = jnp.bfloat16
_NT = (((1,), (1,)), ((), ()))


def _params(semantics, vmem=VMEM_LIMIT):
    return pltpu.CompilerParams(dimension_semantics=semantics, vmem_limit_bytes=vmem)


def _pitch(d):
    return d // LANES + TOKEN_PAD_ROWS


def _store_token_major(ref, val, zero_pad=True):
    n, d = val.shape
    pitch = ref.shape[0] // n
    for c in range(pitch if zero_pad else d // LANES):
        piece = val[:, c * LANES:(c + 1) * LANES] if c < d // LANES else jnp.zeros((n, LANES), val.dtype)
        ref[pl.ds(c, n, stride=pitch), :] = piece


def _load_token_major(ref, n, d, first=0, pitch=None):
    pitch = pitch or ref.shape[0] // n
    return jnp.concatenate([ref[pl.ds(first + c, n, stride=pitch), :] for c in range(d // LANES)], axis=1)


def _split_bf16(a):
    hi = a.astype(_BF16)
    lo = (a - hi.astype(_F32)).astype(_BF16)
    return hi, lo


def _ada_kernel(c_ref, w_ref, b_ref, o_ref):
    ca_hi, ca_lo = _split_bf16(jax.nn.silu(c_ref[...]))
    w_hi, w_lo = _split_bf16(w_ref[...])
    acc = jnp.dot(ca_hi, w_hi, preferred_element_type=_F32)
    acc += jnp.dot(ca_hi, w_lo, preferred_element_type=_F32)
    acc += jnp.dot(ca_lo, w_hi, preferred_element_type=_F32)
    o_ref[...] = acc + b_ref[...]


def _ada(c_pad, w_ada, b_ada, tn):
    rows, d = c_pad.shape
    n = w_ada.shape[1]
    return pl.pallas_call(
        _ada_kernel,
        grid=(n // tn,),
        in_specs=[pl.BlockSpec((rows, d), lambda j: (0, 0)),
                  pl.BlockSpec((d, tn), lambda j: (0, j)),
                  pl.BlockSpec((1, tn), lambda j: (0, j))],
        out_specs=pl.BlockSpec((rows, tn), lambda j: (0, j)),
        out_shape=jax.ShapeDtypeStruct((rows, n), _F32),
        compiler_params=_params(("arbitrary",)),
        name="ada",
    )(c_pad, w_ada, b_ada.reshape(1, n))


def _rms_modulate(x, g, sc, sh):
    y = x * lax.rsqrt(jnp.mean(x * x, axis=-1, keepdims=True) + EPS) * g
    return y * (1.0 + sc) + sh


def _inproj_kernel(x_ref, g_ref, sc_ref, sh_ref, w_ref, cos_ref, sin_ref, lng_ref, lnb_ref,
                   o_ref, n_scr, z_scr, *, q0, k0, v0, dh, scale, cw):
    j = pl.program_id(1)
    tm, tn = o_ref.shape

    def chunk(c):
        return jnp.dot(n_scr[...], w_ref[:, c * cw:(c + 1) * cw], preferred_element_type=_F32)

    @pl.when(j == 0)
    def _():
        band = tm // NORM_BANDS
        for r in range(NORM_BANDS):
            rows = slice(r * band, (r + 1) * band)
            nb = _rms_modulate(x_ref[rows, :], g_ref[...], sc_ref[0], sh_ref[0]).astype(_BF16)
            n_scr[rows, :] = nb
            for c in range(tn // cw):
                cols = slice(c * cw, (c + 1) * cw)
                o_ref[rows, cols] = jax.nn.gelu(
                    jnp.dot(nb, w_ref[:, cols], preferred_element_type=_F32)).astype(_BF16)

    @pl.when(j == 1)
    def _():
        for c in range(tn // cw):
            z_scr[:, c * cw:(c + 1) * cw] = jax.nn.gelu(chunk(c))
        z = z_scr[...]
        zc = z - jnp.mean(z, axis=-1, keepdims=True)
        var = jnp.mean(zc * zc, axis=-1, keepdims=True)
        o_ref[...] = (zc * lax.rsqrt(var + EPS) * lng_ref[...] + lnb_ref[...]).astype(_BF16)

    @pl.when(j >= q0)
    def _():
        mult = jnp.where(j < k0, scale, 1.0)
        cos = jnp.where(j < v0, cos_ref[...], 1.0) * mult
        sin = jnp.where(j < v0, sin_ref[...], 0.0) * mult
        for c in range(tn // cw):
            acc = chunk(c)
            for h in range(cw // dh):
                xh = acc[:, h * dh:(h + 1) * dh]
                rot = xh * cos + pltpu.roll(xh, dh // 2, 1) * sin
                o_ref[:, c * cw + h * dh:c * cw + (h + 1) * dh] = rot.astype(_BF16)


def _inproj(x2, ada3, norm_g, w_in, cos, sin, ln_g, ln_b, *, seq, sw, mw, dh, tm):
    t, d = x2.shape
    n = w_in.shape[1]
    tn = sw
    per_b = seq // tm
    q0 = 2
    k0 = q0 + mw // tn
    v0 = k0 + mw // tn
    cw = _tile(tn, INPROJ_CHUNK)
    assert cw % dh == 0
    kern = functools.partial(_inproj_kernel, q0=q0, k0=k0, v0=v0, dh=dh, scale=dh ** -0.5 * LOG2_E, cw=cw)
    return pl.pallas_call(
        kern,
        grid=(t // tm, n // tn),
        in_specs=[pl.BlockSpec((tm, d), lambda i, j: (i, 0)),
                  pl.BlockSpec((1, d), lambda i, j: (0, 0)),
                  pl.BlockSpec((1, 1, d), lambda i, j: ((i // per_b) * 6 + 1, 0, 0)),
                  pl.BlockSpec((1, 1, d), lambda i, j: ((i // per_b) * 6 + 0, 0, 0)),
                  pl.BlockSpec((d, tn), lambda i, j: (0, j)),
                  pl.BlockSpec((tm, dh), lambda i, j: (i, 0)),
                  pl.BlockSpec((tm, dh), lambda i, j: (i, 0)),
                  pl.BlockSpec((1, sw), lambda i, j: (0, 0)),
                  pl.BlockSpec((1, sw), lambda i, j: (0, 0))],
        out_specs=pl.BlockSpec((tm, tn), lambda i, j: (i, j)),
        out_shape=jax.ShapeDtypeStruct((t, n), _BF16),
        scratch_shapes=[pltpu.VMEM((tm, d), _BF16), pltpu.VMEM((tm, tn), _F32)],
        compiler_params=_params(("arbitrary", "arbitrary")),
        name="inproj",
    )(x2, norm_g.reshape(1, d), ada3, ada3, w_in, cos, sin, ln_g.reshape(1, sw), ln_b.reshape(1, sw))


def _sgu_kernel(u_ref, v_ref, ga_ref, ws_ref, bs_ref, wo_ref, o_ref, gated_scr, *, groups, cg):
    c = SGU_CHUNK
    tri = lax.broadcasted_iota(jnp.int32, (c, c), 0) >= lax.broadcasted_iota(jnp.int32, (c, c), 1)
    for g in range(groups):
        wg = jnp.where(tri, ws_ref[g], 0.0).astype(_BF16)
        cols = slice(g * cg, (g + 1) * cg)
        for ci in range(u_ref.shape[0] // c):
            rows = slice(ci * c, (ci + 1) * c)
            sv = jnp.dot(wg, v_ref[rows, cols], preferred_element_type=_F32) + bs_ref[:, cols]
            gated_scr[rows, cols] = (u_ref[rows, cols].astype(_F32) * sv).astype(_BF16)
    ya = jnp.dot(gated_scr[...], wo_ref[...], preferred_element_type=_F32)
    o_ref[...] = (jax.nn.sigmoid(ga_ref[...].astype(_F32)) * ya).astype(_BF16)


def _sgu(proj, w_s, bs_wide, w_sgu_out, *, sw, mw, tm):
    t = proj.shape[0]
    d = w_sgu_out.shape[1]
    groups = w_s.shape[0]
    ga_blk = (2 * sw + 3 * mw) // d
    kern = functools.partial(_sgu_kernel, groups=groups, cg=sw // groups)
    return pl.pallas_call(
        kern,
        grid=(t // tm,),
        in_specs=[pl.BlockSpec((tm, sw), lambda i: (i, 0)),
                  pl.BlockSpec((tm, sw), lambda i: (i, 1)),
                  pl.BlockSpec((tm, d), lambda i: (i, ga_blk)),
                  pl.BlockSpec(w_s.shape, lambda i: (0, 0, 0)),
                  pl.BlockSpec(bs_wide.shape, lambda i: (0, 0)),
                  pl.BlockSpec(w_sgu_out.shape, lambda i: (0, 0))],
        out_specs=pl.BlockSpec((tm, d), lambda i: (i, 0)),
        out_shape=jax.ShapeDtypeStruct((t, d), _BF16),
        scratch_shapes=[pltpu.VMEM((tm, sw), _BF16)],
        compiler_params=_params(("arbitrary",)),
        name="sgu",
    )(proj, proj, proj, w_s, bs_wide, w_sgu_out)


def _moba_kernel(q_ref, k_ref, v_ref, o_ref, kmh_scr, kml_scr, vt_scr, sel_scr,
                 p_scr, acc_scr, m_scr, l_scr, a_scr, *, nb, topk, hg, dh):
    qi = pl.program_id(2)
    blk = MOBA_BLOCK
    tq = q_ref.shape[0]

    @pl.when(qi == 0)
    def _():
        for h in range(hg):
            cols = slice(h * dh, (h + 1) * dh)
            kf = k_ref[:, cols].astype(_F32).reshape(nb, blk, dh)
            km_hi, km_lo = _split_bf16(jnp.mean(kf, axis=1))
            kmh_scr[h * nb:(h + 1) * nb, :] = km_hi
            kml_scr[h * nb:(h + 1) * nb, :] = km_lo
            for c in range(nb):
                rows = slice(c * blk, (c + 1) * blk)
                vt_scr[cols, rows] = v_ref[rows, cols].astype(_F32).T.astype(_BF16)

    row = lax.broadcasted_iota(jnp.int32, (nb, tq), 0)
    past = row < qi
    kpos = lax.broadcasted_iota(jnp.int32, (blk, tq), 0)
    qpos = lax.broadcasted_iota(jnp.int32, (blk, tq), 1)
    own = pl.multiple_of(qi * blk, blk)

    def scores(start, h):
        cols = slice(h * dh, (h + 1) * dh)
        return lax.dot_general(k_ref[pl.ds(start, blk), cols], q_ref[:, cols], _NT,
                               preferred_element_type=_F32)

    def weighted_values(start, h, p):
        return jnp.dot(vt_scr[h * dh:(h + 1) * dh, pl.ds(start, blk)], p, preferred_element_type=_F32)

    for h in range(hg):
        hrows = slice(h * nb, (h + 1) * nb)
        q = q_ref[:, h * dh:(h + 1) * dh]
        gate = (lax.dot_general(kmh_scr[hrows, :], q, _NT, preferred_element_type=_F32)
                + lax.dot_general(kml_scr[hrows, :], q, _NT, preferred_element_type=_F32))
        gm = jnp.where(past, gate, NEG_INF)
        beaten = jnp.zeros((nb, tq), jnp.int32)
        for jp in range(nb):
            other = gm[jp:jp + 1, :]
            wins = (other > gm) | ((other == gm) & (jp < row))
            beaten += wins.astype(jnp.int32)
        sel_scr[hrows, :] = (past & (beaten < topk)).astype(_F32)

        s = jnp.where(kpos <= qpos, scores(own, h), NEG_INF)
        m0 = jnp.max(s, axis=0, keepdims=True)
        p = jnp.exp2(s - m0)
        m_scr[h] = m0
        l_scr[h] = jnp.sum(p, axis=0, keepdims=True)
        a_scr[h] = jnp.ones_like(m0)
        acc_scr[h] = jnp.zeros(acc_scr.shape[1:], _F32)
        p_scr[0, h] = p.astype(_BF16)

    def body(j, c):
        par = lax.rem(j, 2)
        prev = pl.multiple_of(jnp.where(j == 0, qi, j - 1) * blk, blk)
        for h in range(hg):
            acc_scr[h] = a_scr[h] * acc_scr[h] + weighted_values(prev, h, p_scr[par, h])
        cur = pl.multiple_of(j * blk, blk)
        for h in range(hg):
            s = jnp.where(sel_scr[pl.ds(h * nb + j, 1), :] > 0.0, scores(cur, h), NEG_INF)
            m = m_scr[h]
            m_new = jnp.maximum(m, jnp.max(s, axis=0, keepdims=True))
            alpha = jnp.exp2(m - m_new)
            p = jnp.exp2(s - m_new)
            l_scr[h] = alpha * l_scr[h] + jnp.sum(p, axis=0, keepdims=True)
            m_scr[h] = m_new
            a_scr[h] = alpha
            p_scr[1 - par, h] = p.astype(_BF16)
        return c

    lax.fori_loop(0, qi, body, 0)
    last = pl.multiple_of(jnp.where(qi == 0, qi, qi - 1) * blk, blk)
    for h in range(hg):
        acc = a_scr[h] * acc_scr[h] + weighted_values(last, h, p_scr[lax.rem(qi, 2), h])
        o_ref[:, h * dh:(h + 1) * dh] = (acc / l_scr[h]).T.astype(_BF16)


def _moba(proj, *, batch, seq, sw, mw, dh, hg):
    t = proj.shape[0]
    heads = mw // dh
    nb = seq // MOBA_BLOCK
    tq = MOBA_BLOCK
    nq = seq // tq
    gw = hg * dh
    qc = 2 * sw // gw
    kc = qc + heads // hg
    vc = kc + heads // hg
    kern = functools.partial(_moba_kernel, nb=nb, topk=MOBA_TOPK, hg=hg, dh=dh)
    return pl.pallas_call(
        kern,
        grid=(batch, heads // hg, nq),
        in_specs=[pl.BlockSpec((tq, gw), lambda b, h, i: (b * nq + i, qc + h)),
                  pl.BlockSpec((seq, gw), lambda b, h, i: (b, kc + h)),
                  pl.BlockSpec((seq, gw), lambda b, h, i: (b, vc + h))],
        out_specs=pl.BlockSpec((tq, gw), lambda b, h, i: (b * nq + i, h)),
        out_shape=jax.ShapeDtypeStruct((t, mw), _BF16),
        scratch_shapes=[pltpu.VMEM((hg * nb, dh), _BF16), pltpu.VMEM((hg * nb, dh), _BF16),
                        pltpu.VMEM((gw, seq), _BF16), pltpu.VMEM((hg * nb, tq), _F32),
                        pltpu.VMEM((2, hg, MOBA_BLOCK, tq), _BF16),
                        pltpu.VMEM((hg, dh, tq), _F32), pltpu.VMEM((hg, 1, tq), _F32),
                        pltpu.VMEM((hg, 1, tq), _F32), pltpu.VMEM((hg, 1, tq), _F32)],
        compiler_params=_params(("arbitrary", "arbitrary", "arbitrary")),
        name="moba",
    )(proj, proj, proj)


def _mixout_kernel(attn_ref, ma_ref, gb_ref, x_ref, gm_ref, shf_ref, scf_ref, ng_ref,
                   wmo_ref, wo_ref, wrh_ref, wrl_ref, br_ref,
                   h_ref, n2_ref, eid_ref, wt_ref, *, groups, epg):
    yb = jnp.dot(attn_ref[...], wmo_ref[...], preferred_element_type=_F32)
    merged = ma_ref[...].astype(_F32) + jax.nn.sigmoid(gb_ref[...].astype(_F32)) * yb
    mix = jnp.dot(merged.astype(_BF16), wo_ref[...], preferred_element_type=_F32)
    h = x_ref[...] + gm_ref[0] * mix
    h_ref[...] = h
    n2 = _rms_modulate(h, ng_ref[...], scf_ref[0], shf_ref[0])
    _store_token_major(n2_ref, n2)

    n_hi, n_lo = _split_bf16(n2)
    lg = (jnp.dot(n_hi, wrh_ref[...], preferred_element_type=_F32)
          + jnp.dot(n_lo, wrh_ref[...], preferred_element_type=_F32)
          + jnp.dot(n_hi, wrl_ref[...], preferred_element_type=_F32))
    lg = lg.T[:ROUTER_ROWS, :] + br_ref[:, 0:1]

    gl = [lg[g:g + 1, :] for g in range(groups)]
    gmax = functools.reduce(jnp.maximum, gl)
    denom = functools.reduce(jnp.add, [jnp.exp(v - gmax) for v in gl])
    p_group = 1.0 / denom
    g_sel = jnp.full(gmax.shape, groups - 1, jnp.int32)
    for g in range(groups - 2, -1, -1):
        g_sel = jnp.where(gl[g] == gmax, g, g_sel)

    el = []
    for e in range(epg):
        v = lg[groups + e:groups + e + 1, :]
        for g in range(1, groups):
            r = groups + g * epg + e
            v = jnp.where(g_sel == g, lg[r:r + 1, :], v)
        el.append(v)

    def top1(vals):
        vmax = functools.reduce(jnp.maximum, vals)
        idx = jnp.full(vmax.shape, epg - 1, jnp.int32)
        for e in range(epg - 2, -1, -1):
            idx = jnp.where(vals[e] == vmax, e, idx)
        return vmax, idx

    v1, i1 = top1(el)
    v2, i2 = top1([jnp.where(i1 == e, -jnp.inf, el[e]) for e in range(epg)])
    b = jnp.exp(v2 - v1)
    eid_ref[0:1, :] = g_sel * epg + i1
    eid_ref[1:2, :] = g_sel * epg + i2
    wt_ref[0:1, :] = (1.0 / (1.0 + b)) * p_group
    wt_ref[1:2, :] = (b / (1.0 + b)) * p_group


def _mixout(attn, m_a, proj, x2, ada3, norm_g, w_moba_out, w_out, wr_hi, wr_lo, br,
            *, seq, sw, mw, groups, epg, tm):
    t, d = x2.shape
    per_b = seq // tm
    gb_blk = (2 * sw + 3 * mw) // d + 1
    once = pl.Buffered(1)
    kern = functools.partial(_mixout_kernel, groups=groups, epg=epg)

    def ada_spec(k):
        return pl.BlockSpec((1, 1, d), lambda i: ((i // per_b) * 6 + k, 0, 0))

    return pl.pallas_call(
        kern,
        grid=(t // tm,),
        in_specs=[pl.BlockSpec((tm, mw), lambda i: (i, 0)),
                  pl.BlockSpec((tm, d), lambda i: (i, 0)),
                  pl.BlockSpec((tm, d), lambda i: (i, gb_blk)),
                  pl.BlockSpec((tm, d), lambda i: (i, 0)),
                  ada_spec(2), ada_spec(3), ada_spec(4),
                  pl.BlockSpec((1, d), lambda i: (0, 0)),
                  pl.BlockSpec((mw, d), lambda i: (0, 0), pipeline_mode=once),
                  pl.BlockSpec((d, d), lambda i: (0, 0), pipeline_mode=once),
                  pl.BlockSpec((d, LANES), lambda i: (0, 0)),
                  pl.BlockSpec((d, LANES), lambda i: (0, 0)),
                  pl.BlockSpec((ROUTER_ROWS, LANES), lambda i: (0, 0))],
        out_specs=[pl.BlockSpec((tm, d), lambda i: (i, 0)),
                   pl.BlockSpec((tm * _pitch(d), LANES), lambda i: (i, 0)),
                   pl.BlockSpec((MOE_TOPK, tm), lambda i: (0, i)),
                   pl.BlockSpec((MOE_TOPK, tm), lambda i: (0, i))],
        out_shape=[jax.ShapeDtypeStruct((t, d), _F32),
                   jax.ShapeDtypeStruct((t * _pitch(d), LANES), _F32),
                   jax.ShapeDtypeStruct((MOE_TOPK, t), jnp.int32),
                   jax.ShapeDtypeStruct((MOE_TOPK, t), _F32)],
        compiler_params=_params(("arbitrary",)),
        name="mixout",
    )(attn, m_a, proj, x2, ada3, ada3, ada3, norm_g.reshape(1, d), w_moba_out, w_out, wr_hi, wr_lo, br)


def _plan_kernel(eid_ref, dest_ref, tab_ref, carry_scr, start_scr, tri_scr, *, n_exp, rb):
    phase = pl.program_id(0)
    first = pl.program_id(1) == 0
    tc = eid_ref.shape[1]
    nbp = tab_ref.shape[1]

    @pl.when((phase == 0) & first)
    def _():
        carry_scr[...] = jnp.zeros_like(carry_scr)
        earlier = lax.broadcasted_iota(jnp.int32, (tc, tc), 0) < lax.broadcasted_iota(jnp.int32, (tc, tc), 1)
        tri_scr[...] = earlier.astype(_BF16)

    eio = lax.broadcasted_iota(jnp.int32, (n_exp, tc), 0)
    oh0 = (eio == eid_ref[0:1, :]).astype(_F32)
    oh1 = (eio == eid_ref[1:2, :]).astype(_F32)
    oh = oh0 + oh1

    @pl.when((phase == 1) & first)
    def _():
        counts = carry_scr[...].astype(jnp.int32)
        blocks = lax.shift_right_logical(counts + (rb - 1), rb.bit_length() - 1).astype(_F32)
        upto = lax.broadcasted_iota(jnp.int32, (n_exp, n_exp), 0) >= lax.broadcasted_iota(jnp.int32, (n_exp, n_exp), 1)
        end_blk = jnp.dot(upto.astype(_BF16), blocks.astype(_BF16), preferred_element_type=_F32)
        start_scr[...] = (end_blk - blocks) * rb
        carry_scr[...] = jnp.zeros_like(carry_scr)
        blk_id = lax.broadcasted_iota(jnp.int32, (n_exp, nbp), 1).astype(_F32)
        owner = jnp.sum((end_blk[:, 0:1] <= blk_id).astype(_F32), axis=0, keepdims=True)
        owner = jnp.minimum(owner, n_exp - 1.0)
        used = jnp.broadcast_to(end_blk[n_exp - 1:n_exp, 0:1], (1, nbp))
        trow = lax.broadcasted_iota(jnp.int32, tab_ref.shape, 0)
        tab_ref[...] = jnp.where(trow == 0, owner, jnp.where(trow == 1, used, 0.0)).astype(jnp.int32)

    @pl.when(phase == 1)
    def _():
        before = jnp.dot(oh.astype(_BF16), tri_scr[...], preferred_element_type=_F32)
        base = start_scr[:, 0:1] + carry_scr[:, 0:1] + before
        dest_ref[0:1, :] = jnp.sum(oh0 * base, axis=0, keepdims=True).astype(jnp.int32)
        dest_ref[1:2, :] = jnp.sum(oh1 * base, axis=0, keepdims=True).astype(jnp.int32)

    carry_scr[...] = carry_scr[...] + jnp.sum(oh, axis=1, keepdims=True)


def _plan(eid, *, n_exp, tc, rb, n_blocks):
    t = eid.shape[1]
    assert rb & (rb - 1) == 0 and t * MOE_TOPK // rb + n_exp < 256
    nbp = -(-n_blocks // LANES) * LANES
    kern = functools.partial(_plan_kernel, n_exp=n_exp, rb=rb)
    return pl.pallas_call(
        kern,
        grid=(2, t // tc),
        in_specs=[pl.BlockSpec((MOE_TOPK, tc), lambda p, i: (0, i))],
        out_specs=[pl.BlockSpec((MOE_TOPK, tc), lambda p, i: (0, i * p)),
                   pl.BlockSpec((8, nbp), lambda p, i: (0, 0))],
        out_shape=[jax.ShapeDtypeStruct((MOE_TOPK, t), jnp.int32),
                   jax.ShapeDtypeStruct((8, nbp), jnp.int32)],
        scratch_shapes=[pltpu.VMEM((n_exp, LANES), _F32), pltpu.VMEM((n_exp, LANES), _F32),
                        pltpu.VMEM((tc, tc), _BF16)],
        compiler_params=_params(("arbitrary", "arbitrary")),
        name="plan",
    )(eid)


def _invert_kernel(dest_ref, pad_ref, inv_ref, sem):
    i = pl.program_id(0)
    tch = dest_ref.shape[1]

    @pl.when(i == 0)
    def _():
        fill = pltpu.make_async_copy(pad_ref, inv_ref, sem)
        fill.start()
        fill.wait()

    def body(t, c):
        a = (i * tch + t) * MOE_TOPK
        for k in range(MOE_TOPK):
            inv_ref[dest_ref[k, t]] = a + k
        return c

    lax.fori_loop(0, tch, body, 0, unroll=8)


def _invert(dest3, cap):
    steps, _, tch = dest3.shape
    pad = np.zeros((cap,), np.int32)
    return pl.pallas_call(
        _invert_kernel,
        grid=(steps,),
        in_specs=[pl.BlockSpec((None, MOE_TOPK, tch), lambda i: (i, 0, 0), memory_space=pltpu.SMEM),
                  pl.BlockSpec(memory_space=pl.ANY)],
        out_specs=pl.BlockSpec(pad.shape, lambda i: (0,), memory_space=pltpu.SMEM),
        out_shape=jax.ShapeDtypeStruct(pad.shape, jnp.int32),
        scratch_shapes=[pltpu.SemaphoreType.DMA(())],
        compiler_params=_params(("arbitrary",)),
        name="invert",
    )(dest3, jnp.asarray(pad))


def _aligned_rows(chunk, rows_per_chunk):
    row = chunk * rows_per_chunk
    return pl.multiple_of(row, 8) if rows_per_chunk % 8 == 0 else row


def _expert_kernel(be_ref, nu_ref, invc_ref, invn_ref, n2_ref, wg_ref, wu_ref, wd_ref, y_ref,
                   wg_s, wu_s, wd_s, xbuf0, xbuf1, gsem, *, rb, d):
    i = pl.program_id(0)
    last = nu_ref[0] - 1
    used = i <= last
    fresh = (i == 0) | (be_ref[i] != be_ref[jnp.maximum(i - 1, 0)])
    ch = d // LANES
    pitch = _pitch(d)
    xbuf = (xbuf0, xbuf1)

    def gather(inv_ref, p):
        for r in range(rb):
            tok = lax.shift_right_logical(inv_ref[0, r], 1)
            pltpu.make_async_copy(n2_ref.at[pl.ds(_aligned_rows(tok, pitch), ch)],
                                  xbuf[p].at[pl.ds(r * pitch, ch)], gsem.at[p]).start(priority=r % 2)

    def gather_wait(p):
        pltpu.make_async_copy(n2_ref.at[pl.ds(0, rb * ch)], xbuf[p].at[pl.ds(0, rb * ch)], gsem.at[p]).wait()

    @pl.when(i == 0)
    def _():
        gather(invc_ref, 0)

    @pl.when(used & fresh)
    def _():
        wg_s[...] = wg_ref[0].astype(_BF16)
        wu_s[...] = wu_ref[0].astype(_BF16)
        wd_s[...] = wd_ref[0].astype(_BF16)

    def step(p):
        gather_wait(p)
        gather(invn_ref, 1 - p)
        xb = _load_token_major(xbuf[p], rb, d).astype(_BF16)
        gate = jnp.dot(xb, wg_s[...], preferred_element_type=_F32)
        up = jnp.dot(xb, wu_s[...], preferred_element_type=_F32)
        hid = (jax.nn.silu(gate) * up).astype(_BF16)
        _store_token_major(y_ref, jnp.dot(hid, wd_s[...], preferred_element_type=_F32))

        @pl.when(i == last)
        def _():
            gather_wait(1 - p)

    for p in range(2):
        pl.when(used & (lax.rem(i, 2) == p))(functools.partial(step, p))

    @pl.when(jnp.logical_not(used))
    def _():
        y_ref[...] = jnp.zeros_like(y_ref)


def _experts(block_expert, n_used, inv3, n2, w_gate, w_up, w_down):
    nblk, _, rb = inv3.shape
    d, f = w_gate.shape[1:]
    ch = d // LANES

    def inv_spec(shift):
        return pl.BlockSpec((None, 1, rb), lambda i, be, nu: (jnp.clip(i + shift, 0, nu[0] - 1), 0, 0),
                            memory_space=pltpu.SMEM)

    grid_spec = pltpu.PrefetchScalarGridSpec(
        num_scalar_prefetch=2,
        grid=(nblk,),
        in_specs=[inv_spec(0), inv_spec(1),
                  pl.BlockSpec(memory_space=pl.ANY),
                  pl.BlockSpec((1, d, f), lambda i, be, nu: (be[i], 0, 0)),
                  pl.BlockSpec((1, d, f), lambda i, be, nu: (be[i], 0, 0)),
                  pl.BlockSpec((1, f, d), lambda i, be, nu: (be[i], 0, 0))],
        out_specs=pl.BlockSpec((rb * ch, LANES), lambda i, be, nu: (i, 0)),
        scratch_shapes=[pltpu.VMEM((d, f), _BF16), pltpu.VMEM((d, f), _BF16), pltpu.VMEM((f, d), _BF16),
                        pltpu.VMEM((rb * _pitch(d), LANES), _F32), pltpu.VMEM((rb * _pitch(d), LANES), _F32),
                        pltpu.SemaphoreType.DMA((2,))],
    )
    kern = functools.partial(_expert_kernel, rb=rb, d=d)
    return pl.pallas_call(
        kern,
        grid_spec=grid_spec,
        out_shape=jax.ShapeDtypeStruct((nblk * rb * ch, LANES), _F32),
        compiler_params=_params(("arbitrary",)),
        name="experts",
    )(block_expert, n_used, inv3, inv3, n2, w_gate, w_up, w_down)


def _combine_kernel(destc_ref, destn_ref, h_ref, wc_ref, gf_ref, ng_ref, y_ref, o_ref,
                    b00, b01, b10, b11, sem):
    i = pl.program_id(0)
    tk, d = h_ref.shape
    ch = d // LANES
    pitch = b00.shape[0] // tk
    bufs = ((b00, b01), (b10, b11))

    def gather(dest_ref, p):
        for r in range(tk):
            for k in range(MOE_TOPK):
                pltpu.make_async_copy(y_ref.at[pl.ds(_aligned_rows(dest_ref[k, r], ch), ch)],
                                      bufs[p][k].at[pl.ds(r * pitch, ch)], sem.at[p]).start(priority=r % 2)

    def gather_wait(p):
        for k in range(MOE_TOPK):
            pltpu.make_async_copy(y_ref.at[pl.ds(0, tk * ch)], bufs[p][k].at[pl.ds(0, tk * ch)], sem.at[p]).wait()

    @pl.when(i == 0)
    def _():
        gather(destc_ref, 0)

    def step(p):
        gather_wait(p)
        gather(destn_ref, 1 - p)
        y = [_load_token_major(bufs[p][k], tk, d) for k in range(MOE_TOPK)]
        moe = wc_ref[:, 0:1] * y[0] + wc_ref[:, 1:2] * y[1]
        h = h_ref[...] + gf_ref[0] * moe
        o_ref[...] = h * lax.rsqrt(jnp.mean(h * h, axis=-1, keepdims=True) + EPS) * ng_ref[...]

        @pl.when(i == pl.num_programs(0) - 1)
        def _():
            gather_wait(1 - p)

    for p in range(2):
        pl.when(lax.rem(i, 2) == p)(functools.partial(step, p))


def _combine(dest3, h1, w_cols, ada3, norm_g, y_rows, *, seq):
    t, d = h1.shape
    steps, _, tk = dest3.shape
    per_b = seq // tk

    def dest_spec(shift):
        return pl.BlockSpec((None, MOE_TOPK, tk), lambda i: (jnp.minimum(i + shift, steps - 1), 0, 0),
                            memory_space=pltpu.SMEM)

    return pl.pallas_call(
        _combine_kernel,
        grid=(steps,),
        in_specs=[dest_spec(0), dest_spec(1),
                  pl.BlockSpec((tk, d), lambda i: (i, 0)),
                  pl.BlockSpec((tk, LANES), lambda i: (i, 0)),
                  pl.BlockSpec((1, 1, d), lambda i: ((i // per_b) * 6 + 5, 0, 0)),
                  pl.BlockSpec((1, d), lambda i: (0, 0)),
                  pl.BlockSpec(memory_space=pl.ANY)],
        out_specs=pl.BlockSpec((tk, d), lambda i: (i, 0)),
        out_shape=jax.ShapeDtypeStruct((t, d), _F32),
        scratch_shapes=[*[pltpu.VMEM((tk * _pitch(d), LANES), _F32) for _ in range(2 * MOE_TOPK)],
                        pltpu.SemaphoreType.DMA((2,))],
        compiler_params=_params(("arbitrary",)),
        name="combine",
    )(dest3, dest3, h1, w_cols, ada3, norm_g.reshape(1, d), y_rows)


def _rotary_tables(positions, dh):
    inv_freq = ROPE_THETA ** (-jnp.arange(0, dh, 2, dtype=_F32) / dh)
    ang = positions.astype(_F32)[..., None] * inv_freq
    cos, sin = jnp.cos(ang), jnp.sin(ang)
    t = cos.shape[0] * cos.shape[1]
    cos = jnp.concatenate([cos, cos], axis=-1).reshape(t, dh)
    sin = jnp.concatenate([-sin, sin], axis=-1).reshape(t, dh)
    return cos, sin


def _tile(n, want):
    while n % want:
        want //= 2
    return want


def kernel(x, c, positions, w_ada, b_ada, norm_mix_g, w_in, sgu_ln_g, sgu_ln_b, sgu_w_s, sgu_b_s,
           w_sgu_out, w_moba_out, w_out, norm_ffn_g, w_route_group, b_route_group, w_route_expert,
           b_route_expert, w_exp_gate, w_exp_up, w_exp_down, norm_final_g):
    batch, seq, d = x.shape
    depth = w_ada.shape[0]
    t = batch * seq
    sw = sgu_ln_g.shape[1]
    mw = w_moba_out.shape[1]
    dh = mw // MOBA_HEADS
    groups = w_route_group.shape[2]
    n_exp = w_route_expert.shape[2]
    epg = n_exp // groups
    rb = EXPERT_ROW_BLOCK
    cap = (t * MOE_TOPK // rb + n_exp) * rb
    assert depth == 1, "the final RMSNorm is fused into the single layer's combine"
    assert dh == LANES and seq % MOBA_BLOCK == 0 and sw * 2 == d and mw % sw == 0
    assert groups + n_exp <= ROUTER_ROWS and (t * MOE_TOPK) % rb == 0

    cos, sin = _rotary_tables(positions, dh)
    c_pad = jnp.zeros((8, d), _F32).at[:batch].set(c)
    h = x.reshape(t, d)

    for l in range(depth):
        ada = _ada(c_pad, w_ada[l], b_ada[l], _tile(6 * d, 1024))
        ada3 = ada[:batch].reshape(batch * 6, 1, d)

        proj = _inproj(h, ada3, norm_mix_g[l], w_in[l].astype(_BF16), cos, sin, sgu_ln_g[l], sgu_ln_b[l],
                       seq=seq, sw=sw, mw=mw, dh=dh, tm=_tile(seq, 1024))
        bs_wide = jnp.repeat(sgu_b_s[l].T, sw // sgu_w_s.shape[1], axis=1)
        m_a = _sgu(proj, sgu_w_s[l], bs_wide, w_sgu_out[l].astype(_BF16), sw=sw, mw=mw, tm=_tile(seq, 512))
        attn = _moba(proj, batch=batch, seq=seq, sw=sw, mw=mw, dh=dh, hg=min(MOBA_HEAD_GROUP, MOBA_HEADS))

        wr = jnp.concatenate([w_route_group[l], w_route_expert[l]], axis=1)
        wr_hi, wr_lo = _split_bf16(jnp.zeros((d, LANES), _F32).at[:, :groups + n_exp].set(wr))
        br = jnp.concatenate([b_route_group[l], b_route_expert[l].reshape(-1)])
        br = jnp.broadcast_to(jnp.zeros((ROUTER_ROWS,), _F32).at[:groups + n_exp].set(br)[:, None],
                              (ROUTER_ROWS, LANES))
        h1, n2, eid, wt = _mixout(attn, m_a, proj, h, ada3, norm_ffn_g[l], w_moba_out[l].astype(_BF16),
                                  w_out[l].astype(_BF16), wr_hi, wr_lo, br,
                                  seq=seq, sw=sw, mw=mw, groups=groups, epg=epg, tm=_tile(seq, 256))

        dest, tab = _plan(eid, n_exp=n_exp, tc=_tile(t, 512), rb=rb, n_blocks=cap // rb)
        def token_blocks(size):
            return dest.reshape(MOE_TOPK, t // size, size).transpose(1, 0, 2)

        inv = _invert(token_blocks(_tile(t, 2048)), cap)
        y_rows = _experts(tab[0, :cap // rb], tab[1, :1], inv.reshape(cap // rb, 1, rb), n2,
                          w_exp_gate[l], w_exp_up[l], w_exp_down[l])
        w_cols = jnp.zeros((t, LANES), _F32).at[:, :MOE_TOPK].set(wt.T)
        h = _combine(token_blocks(_tile(seq, 256)), h1, w_cols, ada3, norm_final_g, y_rows, seq=seq)

    return h.reshape(batch, seq, d)
```

```python
import functools

import jax
import jax.numpy as jnp
import numpy as np
from jax import lax
from jax.experimental import pallas as pl
from jax.experimental.pallas import tpu as pltpu

MOBA_HEADS = 16
MOBA_BLOCK = 256
MOBA_TOPK = 3
SGU_CHUNK = 128
ROPE_THETA = 10000.0
MOE_GROUPS = 4
MOE_TOPK = 2
EPS = 1e-6
NEG_INF = -1e30
LOG2_E = 1.4426950408889634

LANES = 128
TOKEN_PAD_ROWS = 8
EXPERT_ROW_BLOCK = 256
INPROJ_CHUNK = 256
NORM_BANDS = 4
MOBA_HEAD_GROUP = 8
ROUTER_ROWS = 40
VMEM_LIMIT = 56 * 1024 * 1024

_F32 = jnp.float32
_BF16 = jnp.bfloat16
_NT = (((1,), (1,)), ((), ()))


def _params(semantics, vmem=VMEM_LIMIT):
    return pltpu.CompilerParams(dimension_semantics=semantics, vmem_limit_bytes=vmem)


def _pitch(d):
    return d // LANES + TOKEN_PAD_ROWS


def _store_token_major(ref, val, zero_pad=True):
    n, d = val.shape
    pitch = ref.shape[0] // n
    for c in range(pitch if zero_pad else d // LANES):
        piece = val[:, c * LANES:(c + 1) * LANES] if c < d // LANES else jnp.zeros((n, LANES), val.dtype)
        ref[pl.ds(c, n, stride=pitch), :] = piece


def _load_token_major(ref, n, d, first=0, pitch=None):
    pitch = pitch or ref.shape[0] // n
    return jnp.concatenate([ref[pl.ds(first + c, n, stride=pitch), :] for c in range(d // LANES)], axis=1)


def _split_bf16(a):
    hi = a.astype(_BF16)
    lo = (a - hi.astype(_F32)).astype(_BF16)
    return hi, lo


def _ada_kernel(c_ref, w_ref, b_ref, o_ref):
    ca_hi, ca_lo = _split_bf16(jax.nn.silu(c_ref[...]))
    w_hi, w_lo = _split_bf16(w_ref[...])
    acc = jnp.dot(ca_hi, w_hi, preferred_element_type=_F32)
    acc += jnp.dot(ca_hi, w_lo, preferred_element_type=_F32)
    acc += jnp.dot(ca_lo, w_hi, preferred_element_type=_F32)
    o_ref[...] = acc + b_ref[...]


def _ada(c_pad, w_ada, b_ada, tn):
    rows, d = c_pad.shape
    n = w_ada.shape[1]
    return pl.pallas_call(
        _ada_kernel,
        grid=(n // tn,),
        in_specs=[pl.BlockSpec((rows, d), lambda j: (0, 0)),
                  pl.BlockSpec((d, tn), lambda j: (0, j)),
                  pl.BlockSpec((1, tn), lambda j: (0, j))],
        out_specs=pl.BlockSpec((rows, tn), lambda j: (0, j)),
        out_shape=jax.ShapeDtypeStruct((rows, n), _F32),
        compiler_params=_params(("arbitrary",)),
        name="ada",
    )(c_pad, w_ada, b_ada.reshape(1, n))


def _rms_modulate(x, g, sc, sh):
    y = x * lax.rsqrt(jnp.mean(x * x, axis=-1, keepdims=True) + EPS) * g
    return y * (1.0 + sc) + sh


def _inproj_kernel(x_ref, g_ref, sc_ref, sh_ref, w_ref, cos_ref, sin_ref, lng_ref, lnb_ref,
                   o_ref, n_scr, z_scr, *, q0, k0, v0, dh, scale, cw):
    j = pl.program_id(1)
    tm, tn = o_ref.shape

    def chunk(c):
        return jnp.dot(n_scr[...], w_ref[:, c * cw:(c + 1) * cw], preferred_element_type=_F32)

    @pl.when(j == 0)
    def _():
        band = tm // NORM_BANDS
        for r in range(NORM_BANDS):
            rows = slice(r * band, (r + 1) * band)
            nb = _rms_modulate(x_ref[rows, :], g_ref[...], sc_ref[0], sh_ref[0]).astype(_BF16)
            n_scr[rows, :] = nb
            for c in range(tn // cw):
                cols = slice(c * cw, (c + 1) * cw)
                o_ref[rows, cols] = jax.nn.gelu(
                    jnp.dot(nb, w_ref[:, cols], preferred_element_type=_F32)).astype(_BF16)

    @pl.when(j == 1)
    def _():
        for c in range(tn // cw):
            z_scr[:, c * cw:(c + 1) * cw] = jax.nn.gelu(chunk(c))
        z = z_scr[...]
        zc = z - jnp.mean(z, axis=-1, keepdims=True)
        var = jnp.mean(zc * zc, axis=-1, keepdims=True)
        o_ref[...] = (zc * lax.rsqrt(var + EPS) * lng_ref[...] + lnb_ref[...]).astype(_BF16)

    @pl.when(j >= q0)
    def _():
        mult = jnp.where(j < k0, scale, 1.0)
        cos = jnp.where(j < v0, cos_ref[...], 1.0) * mult
        sin = jnp.where(j < v0, sin_ref[...], 0.0) * mult
        for c in range(tn // cw):
            acc = chunk(c)
            for h in range(cw // dh):
                xh = acc[:, h * dh:(h + 1) * dh]
                rot = xh * cos + pltpu.roll(xh, dh // 2, 1) * sin
                o_ref[:, c * cw + h * dh:c * cw + (h + 1) * dh] = rot.astype(_BF16)


def _inproj(x2, ada3, norm_g, w_in, cos, sin, ln_g, ln_b, *, seq, sw, mw, dh, tm):
    t, d = x2.shape
    n = w_in.shape[1]
    tn = sw
    per_b = seq // tm
    q0 = 2
    k0 = q0 + mw // tn
    v0 = k0 + mw // tn
    cw = _tile(tn, INPROJ_CHUNK)
    assert cw % dh == 0
    kern = functools.partial(_inproj_kernel, q0=q0, k0=k0, v0=v0, dh=dh, scale=dh ** -0.5 * LOG2_E, cw=cw)
    return pl.pallas_call(
        kern,
        grid=(t // tm, n // tn),
        in_specs=[pl.BlockSpec((tm, d), lambda i, j: (i, 0)),
                  pl.BlockSpec((1, d), lambda i, j: (0, 0)),
                  pl.BlockSpec((1, 1, d), lambda i, j: ((i // per_b) * 6 + 1, 0, 0)),
                  pl.BlockSpec((1, 1, d), lambda i, j: ((i // per_b) * 6 + 0, 0, 0)),
                  pl.BlockSpec((d, tn), lambda i, j: (0, j)),
                  pl.BlockSpec((tm, dh), lambda i, j: (i, 0)),
                  pl.BlockSpec((tm, dh), lambda i, j: (i, 0)),
                  pl.BlockSpec((1, sw), lambda i, j: (0, 0)),
                  pl.BlockSpec((1, sw), lambda i, j: (0, 0))],
        out_specs=pl.BlockSpec((tm, tn), lambda i, j: (i, j)),
        out_shape=jax.ShapeDtypeStruct((t, n), _BF16),
        scratch_shapes=[pltpu.VMEM((tm, d), _BF16), pltpu.VMEM((tm, tn), _F32)],
        compiler_params=_params(("arbitrary", "arbitrary")),
        name="inproj",
    )(x2, norm_g.reshape(1, d), ada3, ada3, w_in, cos, sin, ln_g.reshape(1, sw), ln_b.reshape(1, sw))


def _sgu_kernel(u_ref, v_ref, ga_ref, ws_ref, bs_ref, wo_ref, o_ref, gated_scr, *, groups, cg):
    c = SGU_CHUNK
    tri = lax.broadcasted_iota(jnp.int32, (c, c), 0) >= lax.broadcasted_iota(jnp.int32, (c, c), 1)
    for g in range(groups):
        wg = jnp.where(tri, ws_ref[g], 0.0).astype(_BF16)
        cols = slice(g * cg, (g + 1) * cg)
        for ci in range(u_ref.shape[0] // c):
            rows = slice(ci * c, (ci + 1) * c)
            sv = jnp.dot(wg, v_ref[rows, cols], preferred_element_type=_F32) + bs_ref[:, cols]
            gated_scr[rows, cols] = (u_ref[rows, cols].astype(_F32) * sv).astype(_BF16)
    ya = jnp.dot(gated_scr[...], wo_ref[...], preferred_element_type=_F32)
    o_ref[...] = (jax.nn.sigmoid(ga_ref[...].astype(_F32)) * ya).astype(_BF16)


def _sgu(proj, w_s, bs_wide, w_sgu_out, *, sw, mw, tm):
    t = proj.shape[0]
    d = w_sgu_out.shape[1]
    groups = w_s.shape[0]
    ga_blk = (2 * sw + 3 * mw) // d
    kern = functools.partial(_sgu_kernel, groups=groups, cg=sw // groups)
    return pl.pallas_call(
        kern,
        grid=(t // tm,),
        in_specs=[pl.BlockSpec((tm, sw), lambda i: (i, 0)),
                  pl.BlockSpec((tm, sw), lambda i: (i, 1)),
                  pl.BlockSpec((tm, d), lambda i: (i, ga_blk)),
                  pl.BlockSpec(w_s.shape, lambda i: (0, 0, 0)),
                  pl.BlockSpec(bs_wide.shape, lambda i: (0, 0)),
                  pl.BlockSpec(w_sgu_out.shape, lambda i: (0, 0))],
        out_specs=pl.BlockSpec((tm, d), lambda i: (i, 0)),
        out_shape=jax.ShapeDtypeStruct((t, d), _BF16),
        scratch_shapes=[pltpu.VMEM((tm, sw), _BF16)],
        compiler_params=_params(("arbitrary",)),
        name="sgu",
    )(proj, proj, proj, w_s, bs_wide, w_sgu_out)


def _moba_kernel(q_ref, k_ref, v_ref, o_ref, kmh_scr, kml_scr, vt_scr, sel_scr,
                 p_scr, acc_scr, m_scr, l_scr, a_scr, *, nb, topk, hg, dh):
    qi = pl.program_id(2)
    blk = MOBA_BLOCK
    tq = q_ref.shape[0]

    @pl.when(qi == 0)
    def _():
        for h in range(hg):
            cols = slice(h * dh, (h + 1) * dh)
            kf = k_ref[:, cols].astype(_F32).reshape(nb, blk, dh)
            km_hi, km_lo = _split_bf16(jnp.mean(kf, axis=1))
            kmh_scr[h * nb:(h + 1) * nb, :] = km_hi
            kml_scr[h * nb:(h + 1) * nb, :] = km_lo
            for c in range(nb):
                rows = slice(c * blk, (c + 1) * blk)
                vt_scr[cols, rows] = v_ref[rows, cols].astype(_F32).T.astype(_BF16)

    row = lax.broadcasted_iota(jnp.int32, (nb, tq), 0)
    past = row < qi
    kpos = lax.broadcasted_iota(jnp.int32, (blk, tq), 0)
    qpos = lax.broadcasted_iota(jnp.int32, (blk, tq), 1)
    own = pl.multiple_of(qi * blk, blk)

    def scores(start, h):
        cols = slice(h * dh, (h + 1) * dh)
        return lax.dot_general(k_ref[pl.ds(start, blk), cols], q_ref[:, cols], _NT,
                               preferred_element_type=_F32)

    def weighted_values(start, h, p):
        return jnp.dot(vt_scr[h * dh:(h + 1) * dh, pl.ds(start, blk)], p, preferred_element_type=_F32)

    for h in range(hg):
        hrows = slice(h * nb, (h + 1) * nb)
        q = q_ref[:, h * dh:(h + 1) * dh]
        gate = (lax.dot_general(kmh_scr[hrows, :], q, _NT, preferred_element_type=_F32)
                + lax.dot_general(kml_scr[hrows, :], q, _NT, preferred_element_type=_F32))
        gm = jnp.where(past, gate, NEG_INF)
        beaten = jnp.zeros((nb, tq), jnp.int32)
        for jp in range(nb):
            other = gm[jp:jp + 1, :]
            wins = (other > gm) | ((other == gm) & (jp < row))
            beaten += wins.astype(jnp.int32)
        sel_scr[hrows, :] = (past & (beaten < topk)).astype(_F32)

        s = jnp.where(kpos <= qpos, scores(own, h), NEG_INF)
        m0 = jnp.max(s, axis=0, keepdims=True)
        p = jnp.exp2(s - m0)
        m_scr[h] = m0
        l_scr[h] = jnp.sum(p, axis=0, keepdims=True)
        a_scr[h] = jnp.ones_like(m0)
        acc_scr[h] = jnp.zeros(acc_scr.shape[1:], _F32)
        p_scr[0, h] = p.astype(_BF16)

    def body(j, c):
        par = lax.rem(j, 2)
        prev = pl.multiple_of(jnp.where(j == 0, qi, j - 1) * blk, blk)
        for h in range(hg):
            acc_scr[h] = a_scr[h] * acc_scr[h] + weighted_values(prev, h, p_scr[par, h])
        cur = pl.multiple_of(j * blk, blk)
        for h in range(hg):
            s = jnp.where(sel_scr[pl.ds(h * nb + j, 1), :] > 0.0, scores(cur, h), NEG_INF)
            m = m_scr[h]
            m_new = jnp.maximum(m, jnp.max(s, axis=0, keepdims=True))
            alpha = jnp.exp2(m - m_new)
            p = jnp.exp2(s - m_new)
            l_scr[h] = alpha * l_scr[h] + jnp.sum(p, axis=0, keepdims=True)
            m_scr[h] = m_new
            a_scr[h] = alpha
            p_scr[1 - par, h] = p.astype(_BF16)
        return c

    lax.fori_loop(0, qi, body, 0)
    last = pl.multiple_of(jnp.where(qi == 0, qi, qi - 1) * blk, blk)
    for h in range(hg):
        acc = a_scr[h] * acc_scr[h] + weighted_values(last, h, p_scr[lax.rem(qi, 2), h])
        o_ref[:, h * dh:(h + 1) * dh] = (acc / l_scr[h]).T.astype(_BF16)


def _moba(proj, *, batch, seq, sw, mw, dh, hg):
    t = proj.shape[0]
    heads = mw // dh
    nb = seq // MOBA_BLOCK
    tq = MOBA_BLOCK
    nq = seq // tq
    gw = hg * dh
    qc = 2 * sw // gw
    kc = qc + heads // hg
    vc = kc + heads // hg
    kern = functools.partial(_moba_kernel, nb=nb, topk=MOBA_TOPK, hg=hg, dh=dh)
    return pl.pallas_call(
        kern,
        grid=(batch, heads // hg, nq),
        in_specs=[pl.BlockSpec((tq, gw), lambda b, h, i: (b * nq + i, qc + h)),
                  pl.BlockSpec((seq, gw), lambda b, h, i: (b, kc + h)),
                  pl.BlockSpec((seq, gw), lambda b, h, i: (b, vc + h))],
        out_specs=pl.BlockSpec((tq, gw), lambda b, h, i: (b * nq + i, h)),
        out_shape=jax.ShapeDtypeStruct((t, mw), _BF16),
        scratch_shapes=[pltpu.VMEM((hg * nb, dh), _BF16), pltpu.VMEM((hg * nb, dh), _BF16),
                        pltpu.VMEM((gw, seq), _BF16), pltpu.VMEM((hg * nb, tq), _F32),
                        pltpu.VMEM((2, hg, MOBA_BLOCK, tq), _BF16),
                        pltpu.VMEM((hg, dh, tq), _F32), pltpu.VMEM((hg, 1, tq), _F32),
                        pltpu.VMEM((hg, 1, tq), _F32), pltpu.VMEM((hg, 1, tq), _F32)],
        compiler_params=_params(("arbitrary", "arbitrary", "arbitrary")),
        name="moba",
    )(proj, proj, proj)


def _mixout_kernel(attn_ref, ma_ref, gb_ref, x_ref, gm_ref, shf_ref, scf_ref, ng_ref,
                   wmo_ref, wo_ref, wrh_ref, wrl_ref, br_ref,
                   h_ref, n2_ref, eid_ref, wt_ref, *, groups, epg):
    yb = jnp.dot(attn_ref[...], wmo_ref[...], preferred_element_type=_F32)
    merged = ma_ref[...].astype(_F32) + jax.nn.sigmoid(gb_ref[...].astype(_F32)) * yb
    mix = jnp.dot(merged.astype(_BF16), wo_ref[...], preferred_element_type=_F32)
    h = x_ref[...] + gm_ref[0] * mix
    h_ref[...] = h
    n2 = _rms_modulate(h, ng_ref[...], scf_ref[0], shf_ref[0])
    _store_token_major(n2_ref, n2)

    n_hi, n_lo = _split_bf16(n2)
    lg = (jnp.dot(n_hi, wrh_ref[...], preferred_element_type=_F32)
          + jnp.dot(n_lo, wrh_ref[...], preferred_element_type=_F32)
          + jnp.dot(n_hi, wrl_ref[...], preferred_element_type=_F32))
    lg = lg.T[:ROUTER_ROWS, :] + br_ref[:, 0:1]

    gl = [lg[g:g + 1, :] for g in range(groups)]
    gmax = functools.reduce(jnp.maximum, gl)
    denom = functools.reduce(jnp.add, [jnp.exp(v - gmax) for v in gl])
    p_group = 1.0 / denom
    g_sel = jnp.full(gmax.shape, groups - 1, jnp.int32)
    for g in range(groups - 2, -1, -1):
        g_sel = jnp.where(gl[g] == gmax, g, g_sel)

    el = []
    for e in range(epg):
        v = lg[groups + e:groups + e + 1, :]
        for g in range(1, groups):
            r = groups + g * epg + e
            v = jnp.where(g_sel == g, lg[r:r + 1, :], v)
        el.append(v)

    def top1(vals):
        vmax = functools.reduce(jnp.maximum, vals)
        idx = jnp.full(vmax.shape, epg - 1, jnp.int32)
        for e in range(epg - 2, -1, -1):
            idx = jnp.where(vals[e] == vmax, e, idx)
        return vmax, idx

    v1, i1 = top1(el)
    v2, i2 = top1([jnp.where(i1 == e, -jnp.inf, el[e]) for e in range(epg)])
    b = jnp.exp(v2 - v1)
    eid_ref[0:1, :] = g_sel * epg + i1
    eid_ref[1:2, :] = g_sel * epg + i2
    wt_ref[0:1, :] = (1.0 / (1.0 + b)) * p_group
    wt_ref[1:2, :] = (b / (1.0 + b)) * p_group


def _mixout(attn, m_a, proj, x2, ada3, norm_g, w_moba_out, w_out, wr_hi, wr_lo, br,
            *, seq, sw, mw, groups, epg, tm):
    t, d = x2.shape
    per_b = seq // tm
    gb_blk = (2 * sw + 3 * mw) // d + 1
    once = pl.Buffered(1)
    kern = functools.partial(_mixout_kernel, groups=groups, epg=epg)

    def ada_spec(k):
        return pl.BlockSpec((1, 1, d), lambda i: ((i // per_b) * 6 + k, 0, 0))

    return pl.pallas_call(
        kern,
        grid=(t // tm,),
        in_specs=[pl.BlockSpec((tm, mw), lambda i: (i, 0)),
                  pl.BlockSpec((tm, d), lambda i: (i, 0)),
                  pl.BlockSpec((tm, d), lambda i: (i, gb_blk)),
                  pl.BlockSpec((tm, d), lambda i: (i, 0)),
                  ada_spec(2), ada_spec(3), ada_spec(4),
                  pl.BlockSpec((1, d), lambda i: (0, 0)),
                  pl.BlockSpec((mw, d), lambda i: (0, 0), pipeline_mode=once),
                  pl.BlockSpec((d, d), lambda i: (0, 0), pipeline_mode=once),
                  pl.BlockSpec((d, LANES), lambda i: (0, 0)),
                  pl.BlockSpec((d, LANES), lambda i: (0, 0)),
                  pl.BlockSpec((ROUTER_ROWS, LANES), lambda i: (0, 0))],
        out_specs=[pl.BlockSpec((tm, d), lambda i: (i, 0)),
                   pl.BlockSpec((tm * _pitch(d), LANES), lambda i: (i, 0)),
                   pl.BlockSpec((MOE_TOPK, tm), lambda i: (0, i)),
                   pl.BlockSpec((MOE_TOPK, tm), lambda i: (0, i))],
        out_shape=[jax.ShapeDtypeStruct((t, d), _F32),
                   jax.ShapeDtypeStruct((t * _pitch(d), LANES), _F32),
                   jax.ShapeDtypeStruct((MOE_TOPK, t), jnp.int32),
                   jax.ShapeDtypeStruct((MOE_TOPK, t), _F32)],
        compiler_params=_params(("arbitrary",)),
        name="mixout",
    )(attn, m_a, proj, x2, ada3, ada3, ada3, norm_g.reshape(1, d), w_moba_out, w_out, wr_hi, wr_lo, br)


def _plan_kernel(eid_ref, dest_ref, tab_ref, carry_scr, start_scr, tri_scr, *, n_exp, rb):
    phase = pl.program_id(0)
    first = pl.program_id(1) == 0
    tc = eid_ref.shape[1]
    nbp = tab_ref.shape[1]

    @pl.when((phase == 0) & first)
    def _():
        carry_scr[...] = jnp.zeros_like(carry_scr)
        earlier = lax.broadcasted_iota(jnp.int32, (tc, tc), 0) < lax.broadcasted_iota(jnp.int32, (tc, tc), 1)
        tri_scr[...] = earlier.astype(_BF16)

    eio = lax.broadcasted_iota(jnp.int32, (n_exp, tc), 0)
    oh0 = (eio == eid_ref[0:1, :]).astype(_F32)
    oh1 = (eio == eid_ref[1:2, :]).astype(_F32)
    oh = oh0 + oh1

    @pl.when((phase == 1) & first)
    def _():
        counts = carry_scr[...].astype(jnp.int32)
        blocks = lax.shift_right_logical(counts + (rb - 1), rb.bit_length() - 1).astype(_F32)
        upto = lax.broadcasted_iota(jnp.int32, (n_exp, n_exp), 0) >= lax.broadcasted_iota(jnp.int32, (n_exp, n_exp), 1)
        end_blk = jnp.dot(upto.astype(_BF16), blocks.astype(_BF16), preferred_element_type=_F32)
        start_scr[...] = (end_blk - blocks) * rb
        carry_scr[...] = jnp.zeros_like(carry_scr)
        blk_id = lax.broadcasted_iota(jnp.int32, (n_exp, nbp), 1).astype(_F32)
        owner = jnp.sum((end_blk[:, 0:1] <= blk_id).astype(_F32), axis=0, keepdims=True)
        owner = jnp.minimum(owner, n_exp - 1.0)
        used = jnp.broadcast_to(end_blk[n_exp - 1:n_exp, 0:1], (1, nbp))
        trow = lax.broadcasted_iota(jnp.int32, tab_ref.shape, 0)
        tab_ref[...] = jnp.where(trow == 0, owner, jnp.where(trow == 1, used, 0.0)).astype(jnp.int32)

    @pl.when(phase == 1)
    def _():
        before = jnp.dot(oh.astype(_BF16), tri_scr[...], preferred_element_type=_F32)
        base = start_scr[:, 0:1] + carry_scr[:, 0:1] + before
        dest_ref[0:1, :] = jnp.sum(oh0 * base, axis=0, keepdims=True).astype(jnp.int32)
        dest_ref[1:2, :] = jnp.sum(oh1 * base, axis=0, keepdims=True).astype(jnp.int32)

    carry_scr[...] = carry_scr[...] + jnp.sum(oh, axis=1, keepdims=True)


def _plan(eid, *, n_exp, tc, rb, n_blocks):
    t = eid.shape[1]
    assert rb & (rb - 1) == 0 and t * MOE_TOPK // rb + n_exp < 256
    nbp = -(-n_blocks // LANES) * LANES
    kern = functools.partial(_plan_kernel, n_exp=n_exp, rb=rb)
    return pl.pallas_call(
        kern,
        grid=(2, t // tc),
        in_specs=[pl.BlockSpec((MOE_TOPK, tc), lambda p, i: (0, i))],
        out_specs=[pl.BlockSpec((MOE_TOPK, tc), lambda p, i: (0, i * p)),
                   pl.BlockSpec((8, nbp), lambda p, i: (0, 0))],
        out_shape=[jax.ShapeDtypeStruct((MOE_TOPK, t), jnp.int32),
                   jax.ShapeDtypeStruct((8, nbp), jnp.int32)],
        scratch_shapes=[pltpu.VMEM((n_exp, LANES), _F32), pltpu.VMEM((n_exp, LANES), _F32),
                        pltpu.VMEM((tc, tc), _BF16)],
        compiler_params=_params(("arbitrary", "arbitrary")),
        name="plan",
    )(eid)


def _invert_kernel(dest_ref, pad_ref, inv_ref, sem):
    i = pl.program_id(0)
    tch = dest_ref.shape[1]

    @pl.when(i == 0)
    def _():
        fill = pltpu.make_async_copy(pad_ref, inv_ref, sem)
        fill.start()
        fill.wait()

    def body(t, c):
        a = (i * tch + t) * MOE_TOPK
        for k in range(MOE_TOPK):
            inv_ref[dest_ref[k, t]] = a + k
        return c

    lax.fori_loop(0, tch, body, 0, unroll=8)


def _invert(dest3, cap):
    steps, _, tch = dest3.shape
    pad = np.zeros((cap,), np.int32)
    return pl.pallas_call(
        _invert_kernel,
        grid=(steps,),
        in_specs=[pl.BlockSpec((None, MOE_TOPK, tch), lambda i: (i, 0, 0), memory_space=pltpu.SMEM),
                  pl.BlockSpec(memory_space=pl.ANY)],
        out_specs=pl.BlockSpec(pad.shape, lambda i: (0,), memory_space=pltpu.SMEM),
        out_shape=jax.ShapeDtypeStruct(pad.shape, jnp.int32),
        scratch_shapes=[pltpu.SemaphoreType.DMA(())],
        compiler_params=_params(("arbitrary",)),
        name="invert",
    )(dest3, jnp.asarray(pad))


def _aligned_rows(chunk, rows_per_chunk):
    row = chunk * rows_per_chunk
    return pl.multiple_of(row, 8) if rows_per_chunk % 8 == 0 else row


def _expert_kernel(be_ref, nu_ref, invc_ref, invn_ref, n2_ref, wg_ref, wu_ref, wd_ref, y_ref,
                   wg_s, wu_s, wd_s, wg_f, wu_f, wd_f, xbuf0, xbuf1, run_ref, gsem, wsem, *, rb, d):
    i = pl.program_id(0)
    last = nu_ref[0] - 1
    used = i <= last
    fresh = (i == 0) | (be_ref[i] != be_ref[jnp.maximum(i - 1, 0)])
    ch = d // LANES
    pitch = _pitch(d)
    xbuf = (xbuf0, xbuf1)

    def gather(inv_ref, p):
        for r in range(rb):
            tok = lax.shift_right_logical(inv_ref[0, r], 1)
            pltpu.make_async_copy(n2_ref.at[pl.ds(_aligned_rows(tok, pitch), ch)],
                                  xbuf[p].at[pl.ds(r * pitch, ch)], gsem.at[p]).start(priority=r % 2)

    def gather_wait(p):
        pltpu.make_async_copy(n2_ref.at[pl.ds(0, rb * ch)], xbuf[p].at[pl.ds(0, rb * ch)], gsem.at[p]).wait()

    @pl.when(i == 0)
    def _():
        gather(invc_ref, 0)
        run_ref[0] = 0

    def fetch(expert, slot):
        return [pltpu.make_async_copy(w_ref.at[expert], stage.at[slot], wsem.at[slot])
                for w_ref, stage in ((wg_ref, wg_f), (wu_ref, wu_f), (wd_ref, wd_f))]

    @pl.when(used & fresh)
    def _():
        slot = lax.rem(run_ref[0], 2)

        @pl.when(i == 0)
        def _():
            for copy in fetch(be_ref[0], 0):
                copy.start()

        for copy in fetch(be_ref[i], slot):
            copy.wait()
        wg_s[...] = wg_f[slot].astype(_BF16)
        wu_s[...] = wu_f[slot].astype(_BF16)
        wd_s[...] = wd_f[slot].astype(_BF16)

        nxt = lax.while_loop(lambda j: (j <= last) & (be_ref[jnp.minimum(j, last)] == be_ref[i]),
                             lambda j: j + 1, i + 1)

        @pl.when(nxt <= last)
        def _():
            for copy in fetch(be_ref[jnp.minimum(nxt, last)], 1 - slot):
                copy.start()

        run_ref[0] = run_ref[0] + 1

    def step(p):
        gather_wait(p)
        gather(invn_ref, 1 - p)
        xb = _load_token_major(xbuf[p], rb, d).astype(_BF16)
        gate = jnp.dot(xb, wg_s[...], preferred_element_type=_F32)
        up = jnp.dot(xb, wu_s[...], preferred_element_type=_F32)
        hid = (jax.nn.silu(gate) * up).astype(_BF16)
        _store_token_major(y_ref, jnp.dot(hid, wd_s[...], preferred_element_type=_F32))

        @pl.when(i == last)
        def _():
            gather_wait(1 - p)

    for p in range(2):
        pl.when(used & (lax.rem(i, 2) == p))(functools.partial(step, p))

    @pl.when(jnp.logical_not(used))
    def _():
        y_ref[...] = jnp.zeros_like(y_ref)


def _experts(block_expert, n_used, inv3, n2, w_gate, w_up, w_down):
    nblk, _, rb = inv3.shape
    d, f = w_gate.shape[1:]
    ch = d // LANES

    def inv_spec(shift):
        return pl.BlockSpec((None, 1, rb), lambda i, be, nu: (jnp.clip(i + shift, 0, nu[0] - 1), 0, 0),
                            memory_space=pltpu.SMEM)

    grid_spec = pltpu.PrefetchScalarGridSpec(
        num_scalar_prefetch=2,
        grid=(nblk,),
        in_specs=[inv_spec(0), inv_spec(1),
                  *[pl.BlockSpec(memory_space=pl.ANY) for _ in range(4)]],
        out_specs=pl.BlockSpec((rb * ch, LANES), lambda i, be, nu: (i, 0)),
        scratch_shapes=[pltpu.VMEM((d, f), _BF16), pltpu.VMEM((d, f), _BF16), pltpu.VMEM((f, d), _BF16),
                        pltpu.VMEM((2, d, f), _F32), pltpu.VMEM((2, d, f), _F32), pltpu.VMEM((2, f, d), _F32),
                        pltpu.VMEM((rb * _pitch(d), LANES), _F32), pltpu.VMEM((rb * _pitch(d), LANES), _F32),
                        pltpu.SMEM((1,), jnp.int32),
                        pltpu.SemaphoreType.DMA((2,)), pltpu.SemaphoreType.DMA((2,))],
    )
    kern = functools.partial(_expert_kernel, rb=rb, d=d)
    return pl.pallas_call(
        kern,
        grid_spec=grid_spec,
        out_shape=jax.ShapeDtypeStruct((nblk * rb * ch, LANES), _F32),
        compiler_params=_params(("arbitrary",)),
        name="experts",
    )(block_expert, n_used, inv3, inv3, n2, w_gate, w_up, w_down)


def _combine_kernel(destc_ref, destn_ref, h_ref, wc_ref, gf_ref, ng_ref, y_ref, o_ref,
                    b00, b01, b10, b11, sem):
    i = pl.program_id(0)
    tk, d = h_ref.shape
    ch = d // LANES
    pitch = b00.shape[0] // tk
    bufs = ((b00, b01), (b10, b11))

    def gather(dest_ref, p):
        for r in range(tk):
            for k in range(MOE_TOPK):
                pltpu.make_async_copy(y_ref.at[pl.ds(_aligned_rows(dest_ref[k, r], ch), ch)],
                                      bufs[p][k].at[pl.ds(r * pitch, ch)], sem.at[p]).start(priority=r % 2)

    def gather_wait(p):
        for k in range(MOE_TOPK):
            pltpu.make_async_copy(y_ref.at[pl.ds(0, tk * ch)], bufs[p][k].at[pl.ds(0, tk * ch)], sem.at[p]).wait()

    @pl.when(i == 0)
    def _():
        gather(destc_ref, 0)

    def step(p):
        gather_wait(p)
        gather(destn_ref, 1 - p)
        y = [_load_token_major(bufs[p][k], tk, d) for k in range(MOE_TOPK)]
        moe = wc_ref[:, 0:1] * y[0] + wc_ref[:, 1:2] * y[1]
        h = h_ref[...] + gf_ref[0] * moe
        o_ref[...] = h * lax.rsqrt(jnp.mean(h * h, axis=-1, keepdims=True) + EPS) * ng_ref[...]

        @pl.when(i == pl.num_programs(0) - 1)
        def _():
            gather_wait(1 - p)

    for p in range(2):
        pl.when(lax.rem(i, 2) == p)(functools.partial(step, p))


def _combine(dest3, h1, w_cols, ada3, norm_g, y_rows, *, seq):
    t, d = h1.shape
    steps, _, tk = dest3.shape
    per_b = seq // tk

    def dest_spec(shift):
        return pl.BlockSpec((None, MOE_TOPK, tk), lambda i: (jnp.minimum(i + shift, steps - 1), 0, 0),
                            memory_space=pltpu.SMEM)

    return pl.pallas_call(
        _combine_kernel,
        grid=(steps,),
        in_specs=[dest_spec(0), dest_spec(1),
                  pl.BlockSpec((tk, d), lambda i: (i, 0)),
                  pl.BlockSpec((tk, LANES), lambda i: (i, 0)),
                  pl.BlockSpec((1, 1, d), lambda i: ((i // per_b) * 6 + 5, 0, 0)),
                  pl.BlockSpec((1, d), lambda i: (0, 0)),
                  pl.BlockSpec(memory_space=pl.ANY)],
        out_specs=pl.BlockSpec((tk, d), lambda i: (i, 0)),
        out_shape=jax.ShapeDtypeStruct((t, d), _F32),
        scratch_shapes=[*[pltpu.VMEM((tk * _pitch(d), LANES), _F32) for _ in range(2 * MOE_TOPK)],
                        pltpu.SemaphoreType.DMA((2,))],
        compiler_params=_params(("arbitrary",)),
        name="combine",
    )(dest3, dest3, h1, w_cols, ada3, norm_g.reshape(1, d), y_rows)


def _rotary_tables(positions, dh):
    inv_freq = ROPE_THETA ** (-jnp.arange(0, dh, 2, dtype=_F32) / dh)
    ang = positions.astype(_F32)[..., None] * inv_freq
    cos, sin = jnp.cos(ang), jnp.sin(ang)
    t = cos.shape[0] * cos.shape[1]
    cos = jnp.concatenate([cos, cos], axis=-1).reshape(t, dh)
    sin = jnp.concatenate([-sin, sin], axis=-1).reshape(t, dh)
    return cos, sin


def _tile(n, want):
    while n % want:
        want //= 2
    return want


def kernel(x, c, positions, w_ada, b_ada, norm_mix_g, w_in, sgu_ln_g, sgu_ln_b, sgu_w_s, sgu_b_s,
           w_sgu_out, w_moba_out, w_out, norm_ffn_g, w_route_group, b_route_group, w_route_expert,
           b_route_expert, w_exp_gate, w_exp_up, w_exp_down, norm_final_g):
    batch, seq, d = x.shape
    depth = w_ada.shape[0]
    t = batch * seq
    sw = sgu_ln_g.shape[1]
    mw = w_moba_out.shape[1]
    dh = mw // MOBA_HEADS
    groups = w_route_group.shape[2]
    n_exp = w_route_expert.shape[2]
    epg = n_exp // groups
    rb = EXPERT_ROW_BLOCK
    cap = (t * MOE_TOPK // rb + n_exp) * rb
    assert depth == 1, "the final RMSNorm is fused into the single layer's combine"
    assert dh == LANES and seq % MOBA_BLOCK == 0 and sw * 2 == d and mw % sw == 0
    assert groups + n_exp <= ROUTER_ROWS and (t * MOE_TOPK) % rb == 0

    cos, sin = _rotary_tables(positions, dh)
    c_pad = jnp.zeros((8, d), _F32).at[:batch].set(c)
    h = x.reshape(t, d)

    for l in range(depth):
        ada = _ada(c_pad, w_ada[l], b_ada[l], _tile(6 * d, 1024))
        ada3 = ada[:batch].reshape(batch * 6, 1, d)

        proj = _inproj(h, ada3, norm_mix_g[l], w_in[l].astype(_BF16), cos, sin, sgu_ln_g[l], sgu_ln_b[l],
                       seq=seq, sw=sw, mw=mw, dh=dh, tm=_tile(seq, 1024))
        bs_wide = jnp.repeat(sgu_b_s[l].T, sw // sgu_w_s.shape[1], axis=1)
        m_a = _sgu(proj, sgu_w_s[l], bs_wide, w_sgu_out[l].astype(_BF16), sw=sw, mw=mw, tm=_tile(seq, 512))
        attn = _moba(proj, batch=batch, seq=seq, sw=sw, mw=mw, dh=dh, hg=min(MOBA_HEAD_GROUP, MOBA_HEADS))

        wr = jnp.concatenate([w_route_group[l], w_route_expert[l]], axis=1)
        wr_hi, wr_lo = _split_bf16(jnp.zeros((d, LANES), _F32).at[:, :groups + n_exp].set(wr))
        br = jnp.concatenate([b_route_group[l], b_route_expert[l].reshape(-1)])
        br = jnp.broadcast_to(jnp.zeros((ROUTER_ROWS,), _F32).at[:groups + n_exp].set(br)[:, None],
                              (ROUTER_ROWS, LANES))
        h1, n2, eid, wt = _mixout(attn, m_a, proj, h, ada3, norm_ffn_g[l], w_moba_out[l].astype(_BF16),
                                  w_out[l].astype(_BF16), wr_hi, wr_lo, br,
                                  seq=seq, sw=sw, mw=mw, groups=groups, epg=epg, tm=_tile(seq, 256))

        dest, tab = _plan(eid, n_exp=n_exp, tc=_tile(t, 512), rb=rb, n_blocks=cap // rb)
        def token_blocks(size):
            return dest.reshape(MOE_TOPK, t // size, size).transpose(1, 0, 2)

        inv = _invert(token_blocks(_tile(t, 2048)), cap)
        y_rows = _experts(tab[0, :cap // rb], tab[1, :1], inv.reshape(cap // rb, 1, rb), n2,
                          w_exp_gate[l], w_exp_up[l], w_exp_down[l])
        w_cols = jnp.zeros((t, LANES), _F32).at[:, :MOE_TOPK].set(wt.T)
        h = _combine(token_blocks(_tile(seq, 256)), h1, w_cols, ada3, norm_final_g, y_rows, seq=seq)

    return h.reshape(batch, seq, d)
```

```python
import functools

import jax
import jax.numpy as jnp
import numpy as np
from jax import lax
from jax.experimental import pallas as pl
from jax.experimental.pallas import tpu as pltpu

MOBA_HEADS = 16
MOBA_BLOCK = 256
MOBA_TOPK = 3
SGU_CHUNK = 128
ROPE_THETA = 10000.0
MOE_TOPK = 2
EPS = 1e-6
NEG_INF = -1e30
LOG2_E = 1.4426950408889634

LANES = 128
TOKEN_PAD_ROWS = 8
EXPERT_ROW_BLOCK = 256
INPROJ_CHUNK = 256
NORM_BANDS = 4
MOBA_HEAD_GROUP = 8
ROUTER_ROWS = 40
VMEM_LIMIT = 56 * 1024 * 1024

_F32 = jnp.float32
_BF16 = jnp.bfloat16
_NT = (((1,), (1,)), ((), ()))


def _params(semantics, vmem=VMEM_LIMIT):
    return pltpu.CompilerParams(dimension_semantics=semantics, vmem_limit_bytes=vmem)


def _pitch(d):
    return d // LANES + TOKEN_PAD_ROWS


def _store_token_major(ref, val):
    n, d = val.shape
    pitch = ref.shape[0] // n
    for c in range(pitch):
        piece = val[:, c * LANES:(c + 1) * LANES] if c < d // LANES else jnp.zeros((n, LANES), val.dtype)
        ref[pl.ds(c, n, stride=pitch), :] = piece


def _load_token_major(ref, n, d, first=0, pitch=None):
    pitch = pitch or ref.shape[0] // n
    return jnp.concatenate([ref[pl.ds(first + c, n, stride=pitch), :] for c in range(d // LANES)], axis=1)


def _split_bf16(a):
    hi = a.astype(_BF16)
    lo = (a - hi.astype(_F32)).astype(_BF16)
    return hi, lo


def _ada_kernel(c_ref, w_ref, b_ref, o_ref):
    ca_hi, ca_lo = _split_bf16(jax.nn.silu(c_ref[...]))
    w_hi, w_lo = _split_bf16(w_ref[...])
    acc = jnp.dot(ca_hi, w_hi, preferred_element_type=_F32)
    acc += jnp.dot(ca_hi, w_lo, preferred_element_type=_F32)
    acc += jnp.dot(ca_lo, w_hi, preferred_element_type=_F32)
    o_ref[...] = acc + b_ref[...]


def _ada(c_pad, w_ada, b_ada, tn):
    rows, d = c_pad.shape
    n = w_ada.shape[1]
    return pl.pallas_call(
        _ada_kernel,
        grid=(n // tn,),
        in_specs=[pl.BlockSpec((rows, d), lambda j: (0, 0)),
                  pl.BlockSpec((d, tn), lambda j: (0, j)),
                  pl.BlockSpec((1, tn), lambda j: (0, j))],
        out_specs=pl.BlockSpec((rows, tn), lambda j: (0, j)),
        out_shape=jax.ShapeDtypeStruct((rows, n), _F32),
        compiler_params=_params(("arbitrary",)),
        name="ada",
    )(c_pad, w_ada, b_ada.reshape(1, n))


def _rms_modulate(x, g, sc, sh):
    y = x * lax.rsqrt(jnp.mean(x * x, axis=-1, keepdims=True) + EPS) * g
    return y * (1.0 + sc) + sh


def _inproj_kernel(x_ref, g_ref, sc_ref, sh_ref, w_ref, cos_ref, sin_ref, lng_ref, lnb_ref,
                   o_ref, n_scr, z_scr, *, q0, k0, v0, dh, scale, cw):
    j = pl.program_id(1)
    tm, tn = o_ref.shape

    def chunk(c):
        return jnp.dot(n_scr[...], w_ref[:, c * cw:(c + 1) * cw], preferred_element_type=_F32)

    @pl.when(j == 0)
    def _():
        band = tm // NORM_BANDS
        for r in range(NORM_BANDS):
            rows = slice(r * band, (r + 1) * band)
            nb = _rms_modulate(x_ref[rows, :], g_ref[...], sc_ref[0], sh_ref[0]).astype(_BF16)
            n_scr[rows, :] = nb
            for c in range(tn // cw):
                cols = slice(c * cw, (c + 1) * cw)
                o_ref[rows, cols] = jax.nn.gelu(
                    jnp.dot(nb, w_ref[:, cols], preferred_element_type=_F32)).astype(_BF16)

    @pl.when(j == 1)
    def _():
        for c in range(tn // cw):
            z_scr[:, c * cw:(c + 1) * cw] = jax.nn.gelu(chunk(c))
        z = z_scr[...]
        zc = z - jnp.mean(z, axis=-1, keepdims=True)
        var = jnp.mean(zc * zc, axis=-1, keepdims=True)
        o_ref[...] = (zc * lax.rsqrt(var + EPS) * lng_ref[...] + lnb_ref[...]).astype(_BF16)

    @pl.when(j >= q0)
    def _():
        mult = jnp.where(j < k0, scale, 1.0)
        cos = jnp.where(j < v0, cos_ref[...], 1.0) * mult
        sin = jnp.where(j < v0, sin_ref[...], 0.0) * mult
        for c in range(tn // cw):
            acc = chunk(c)
            for h in range(cw // dh):
                xh = acc[:, h * dh:(h + 1) * dh]
                rot = xh * cos + pltpu.roll(xh, dh // 2, 1) * sin
                o_ref[:, c * cw + h * dh:c * cw + (h + 1) * dh] = rot.astype(_BF16)


def _inproj(x2, ada3, norm_g, w_in, cos, sin, ln_g, ln_b, *, seq, sw, mw, dh, tm):
    t, d = x2.shape
    n = w_in.shape[1]
    tn = sw
    per_b = seq // tm
    q0 = 2
    k0 = q0 + mw // tn
    v0 = k0 + mw // tn
    cw = _tile(tn, INPROJ_CHUNK)
    assert cw % dh == 0
    kern = functools.partial(_inproj_kernel, q0=q0, k0=k0, v0=v0, dh=dh, scale=dh ** -0.5 * LOG2_E, cw=cw)
    return pl.pallas_call(
        kern,
        grid=(t // tm, n // tn),
        in_specs=[pl.BlockSpec((tm, d), lambda i, j: (i, 0)),
                  pl.BlockSpec((1, d), lambda i, j: (0, 0)),
                  pl.BlockSpec((1, 1, d), lambda i, j: ((i // per_b) * 6 + 1, 0, 0)),
                  pl.BlockSpec((1, 1, d), lambda i, j: ((i // per_b) * 6 + 0, 0, 0)),
                  pl.BlockSpec((d, tn), lambda i, j: (0, j)),
                  pl.BlockSpec((tm, dh), lambda i, j: (i, 0)),
                  pl.BlockSpec((tm, dh), lambda i, j: (i, 0)),
                  pl.BlockSpec((1, sw), lambda i, j: (0, 0)),
                  pl.BlockSpec((1, sw), lambda i, j: (0, 0))],
        out_specs=pl.BlockSpec((tm, tn), lambda i, j: (i, j)),
        out_shape=jax.ShapeDtypeStruct((t, n), _BF16),
        scratch_shapes=[pltpu.VMEM((tm, d), _BF16), pltpu.VMEM((tm, tn), _F32)],
        compiler_params=_params(("arbitrary", "arbitrary")),
        name="inproj",
    )(x2, norm_g.reshape(1, d), ada3, ada3, w_in, cos, sin, ln_g.reshape(1, sw), ln_b.reshape(1, sw))


def _sgu_kernel(u_ref, v_ref, ga_ref, ws_ref, bs_ref, wo_ref, o_ref, gated_scr, *, groups, cg):
    c = SGU_CHUNK
    tri = lax.broadcasted_iota(jnp.int32, (c, c), 0) >= lax.broadcasted_iota(jnp.int32, (c, c), 1)
    for g in range(groups):
        wg = jnp.where(tri, ws_ref[g], 0.0).astype(_BF16)
        cols = slice(g * cg, (g + 1) * cg)
        for ci in range(u_ref.shape[0] // c):
            rows = slice(ci * c, (ci + 1) * c)
            sv = jnp.dot(wg, v_ref[rows, cols], preferred_element_type=_F32) + bs_ref[:, cols]
            gated_scr[rows, cols] = (u_ref[rows, cols].astype(_F32) * sv).astype(_BF16)
    ya = jnp.dot(gated_scr[...], wo_ref[...], preferred_element_type=_F32)
    o_ref[...] = (jax.nn.sigmoid(ga_ref[...].astype(_F32)) * ya).astype(_BF16)


def _sgu(proj, w_s, bs_wide, w_sgu_out, *, sw, mw, tm):
    t = proj.shape[0]
    d = w_sgu_out.shape[1]
    groups = w_s.shape[0]
    ga_blk = (2 * sw + 3 * mw) // d
    kern = functools.partial(_sgu_kernel, groups=groups, cg=sw // groups)
    return pl.pallas_call(
        kern,
        grid=(t // tm,),
        in_specs=[pl.BlockSpec((tm, sw), lambda i: (i, 0)),
                  pl.BlockSpec((tm, sw), lambda i: (i, 1)),
                  pl.BlockSpec((tm, d), lambda i: (i, ga_blk)),
                  pl.BlockSpec(w_s.shape, lambda i: (0, 0, 0)),
                  pl.BlockSpec(bs_wide.shape, lambda i: (0, 0)),
                  pl.BlockSpec(w_sgu_out.shape, lambda i: (0, 0))],
        out_specs=pl.BlockSpec((tm, d), lambda i: (i, 0)),
        out_shape=jax.ShapeDtypeStruct((t, d), _BF16),
        scratch_shapes=[pltpu.VMEM((tm, sw), _BF16)],
        compiler_params=_params(("arbitrary",)),
        name="sgu",
    )(proj, proj, proj, w_s, bs_wide, w_sgu_out)


def _moba_kernel(q_ref, k_ref, v_ref, o_ref, kmh_scr, kml_scr, vt_scr, sel_scr,
                 p_scr, acc_scr, m_scr, l_scr, a_scr, *, nb, topk, hg, dh):
    qi = pl.program_id(2)
    blk = MOBA_BLOCK
    tq = q_ref.shape[0]

    @pl.when(qi == 0)
    def _():
        for h in range(hg):
            cols = slice(h * dh, (h + 1) * dh)
            kf = k_ref[:, cols].astype(_F32).reshape(nb, blk, dh)
            km_hi, km_lo = _split_bf16(jnp.mean(kf, axis=1))
            kmh_scr[h * nb:(h + 1) * nb, :] = km_hi
            kml_scr[h * nb:(h + 1) * nb, :] = km_lo
            for c in range(nb):
                rows = slice(c * blk, (c + 1) * blk)
                vt_scr[cols, rows] = v_ref[rows, cols].astype(_F32).T.astype(_BF16)

    row = lax.broadcasted_iota(jnp.int32, (nb, tq), 0)
    past = row < qi
    kpos = lax.broadcasted_iota(jnp.int32, (blk, tq), 0)
    qpos = lax.broadcasted_iota(jnp.int32, (blk, tq), 1)
    own = pl.multiple_of(qi * blk, blk)

    def scores(start, h):
        cols = slice(h * dh, (h + 1) * dh)
        return lax.dot_general(k_ref[pl.ds(start, blk), cols], q_ref[:, cols], _NT,
                               preferred_element_type=_F32)

    def weighted_values(start, h, p):
        return jnp.dot(vt_scr[h * dh:(h + 1) * dh, pl.ds(start, blk)], p, preferred_element_type=_F32)

    for h in range(hg):
        hrows = slice(h * nb, (h + 1) * nb)
        q = q_ref[:, h * dh:(h + 1) * dh]
        gate = (lax.dot_general(kmh_scr[hrows, :], q, _NT, preferred_element_type=_F32)
                + lax.dot_general(kml_scr[hrows, :], q, _NT, preferred_element_type=_F32))
        gm = jnp.where(past, gate, NEG_INF)
        beaten = jnp.zeros((nb, tq), jnp.int32)
        for jp in range(nb):
            other = gm[jp:jp + 1, :]
            wins = (other > gm) | ((other == gm) & (jp < row))
            beaten += wins.astype(jnp.int32)
        sel_scr[hrows, :] = (past & (beaten < topk)).astype(_F32)

        s = jnp.where(kpos <= qpos, scores(own, h), NEG_INF)
        m0 = jnp.max(s, axis=0, keepdims=True)
        p = jnp.exp2(s - m0)
        m_scr[h] = m0
        l_scr[h] = jnp.sum(p, axis=0, keepdims=True)
        a_scr[h] = jnp.ones_like(m0)
        acc_scr[h] = jnp.zeros(acc_scr.shape[1:], _F32)
        p_scr[0, h] = p.astype(_BF16)

    def body(j, c):
        par = lax.rem(j, 2)
        prev = pl.multiple_of(jnp.where(j == 0, qi, j - 1) * blk, blk)
        for h in range(hg):
            acc_scr[h] = a_scr[h] * acc_scr[h] + weighted_values(prev, h, p_scr[par, h])
        cur = pl.multiple_of(j * blk, blk)
        for h in range(hg):
            s = jnp.where(sel_scr[pl.ds(h * nb + j, 1), :] > 0.0, scores(cur, h), NEG_INF)
            m = m_scr[h]
            m_new = jnp.maximum(m, jnp.max(s, axis=0, keepdims=True))
            alpha = jnp.exp2(m - m_new)
            p = jnp.exp2(s - m_new)
            l_scr[h] = alpha * l_scr[h] + jnp.sum(p, axis=0, keepdims=True)
            m_scr[h] = m_new
            a_scr[h] = alpha
            p_scr[1 - par, h] = p.astype(_BF16)
        return c

    lax.fori_loop(0, qi, body, 0)
    last = pl.multiple_of(jnp.where(qi == 0, qi, qi - 1) * blk, blk)
    for h in range(hg):
        acc = a_scr[h] * acc_scr[h] + weighted_values(last, h, p_scr[lax.rem(qi, 2), h])
        o_ref[:, h * dh:(h + 1) * dh] = (acc / l_scr[h]).T.astype(_BF16)


def _moba(proj, *, batch, seq, sw, mw, dh, hg):
    t = proj.shape[0]
    heads = mw // dh
    nb = seq // MOBA_BLOCK
    tq = MOBA_BLOCK
    nq = seq // tq
    gw = hg * dh
    qc = 2 * sw // gw
    kc = qc + heads // hg
    vc = kc + heads // hg
    kern = functools.partial(_moba_kernel, nb=nb, topk=MOBA_TOPK, hg=hg, dh=dh)
    return pl.pallas_call(
        kern,
        grid=(batch, heads // hg, nq),
        in_specs=[pl.BlockSpec((tq, gw), lambda b, h, i: (b * nq + i, qc + h)),
                  pl.BlockSpec((seq, gw), lambda b, h, i: (b, kc + h)),
                  pl.BlockSpec((seq, gw), lambda b, h, i: (b, vc + h))],
        out_specs=pl.BlockSpec((tq, gw), lambda b, h, i: (b * nq + i, h)),
        out_shape=jax.ShapeDtypeStruct((t, mw), _BF16),
        scratch_shapes=[pltpu.VMEM((hg * nb, dh), _BF16), pltpu.VMEM((hg * nb, dh), _BF16),
                        pltpu.VMEM((gw, seq), _BF16), pltpu.VMEM((hg * nb, tq), _F32),
                        pltpu.VMEM((2, hg, MOBA_BLOCK, tq), _BF16),
                        pltpu.VMEM((hg, dh, tq), _F32), pltpu.VMEM((hg, 1, tq), _F32),
                        pltpu.VMEM((hg, 1, tq), _F32), pltpu.VMEM((hg, 1, tq), _F32)],
        compiler_params=_params(("arbitrary", "arbitrary", "arbitrary")),
        name="moba",
    )(proj, proj, proj)


def _mixout_kernel(attn_ref, ma_ref, gb_ref, x_ref, gm_ref, shf_ref, scf_ref, ng_ref,
                   wmo_ref, wo_ref, wrh_ref, wrl_ref, br_ref,
                   h_ref, n2_ref, eid_ref, wt_ref, *, groups, epg):
    yb = jnp.dot(attn_ref[...], wmo_ref[...], preferred_element_type=_F32)
    merged = ma_ref[...].astype(_F32) + jax.nn.sigmoid(gb_ref[...].astype(_F32)) * yb
    mix = jnp.dot(merged.astype(_BF16), wo_ref[...], preferred_element_type=_F32)
    h = x_ref[...] + gm_ref[0] * mix
    h_ref[...] = h
    n2 = _rms_modulate(h, ng_ref[...], scf_ref[0], shf_ref[0])
    _store_token_major(n2_ref, n2)

    n_hi, n_lo = _split_bf16(n2)
    lg = (jnp.dot(n_hi, wrh_ref[...], preferred_element_type=_F32)
          + jnp.dot(n_lo, wrh_ref[...], preferred_element_type=_F32)
          + jnp.dot(n_hi, wrl_ref[...], preferred_element_type=_F32))
    lg = lg.T[:ROUTER_ROWS, :] + br_ref[:, 0:1]

    gl = [lg[g:g + 1, :] for g in range(groups)]
    gmax = functools.reduce(jnp.maximum, gl)
    denom = functools.reduce(jnp.add, [jnp.exp(v - gmax) for v in gl])
    p_group = 1.0 / denom
    g_sel = jnp.full(gmax.shape, groups - 1, jnp.int32)
    for g in range(groups - 2, -1, -1):
        g_sel = jnp.where(gl[g] == gmax, g, g_sel)

    el = []
    for e in range(epg):
        v = lg[groups + e:groups + e + 1, :]
        for g in range(1, groups):
            r = groups + g * epg + e
            v = jnp.where(g_sel == g, lg[r:r + 1, :], v)
        el.append(v)

    def top1(vals):
        vmax = functools.reduce(jnp.maximum, vals)
        idx = jnp.full(vmax.shape, epg - 1, jnp.int32)
        for e in range(epg - 2, -1, -1):
            idx = jnp.where(vals[e] == vmax, e, idx)
        return vmax, idx

    v1, i1 = top1(el)
    v2, i2 = top1([jnp.where(i1 == e, -jnp.inf, el[e]) for e in range(epg)])
    b = jnp.exp(v2 - v1)
    eid_ref[0:1, :] = g_sel * epg + i1
    eid_ref[1:2, :] = g_sel * epg + i2
    wt_ref[0:1, :] = (1.0 / (1.0 + b)) * p_group
    wt_ref[1:2, :] = (b / (1.0 + b)) * p_group


def _mixout(attn, m_a, proj, x2, ada3, norm_g, w_moba_out, w_out, wr_hi, wr_lo, br,
            *, seq, sw, mw, groups, epg, tm):
    t, d = x2.shape
    per_b = seq // tm
    gb_blk = (2 * sw + 3 * mw) // d + 1
    once = pl.Buffered(1)
    kern = functools.partial(_mixout_kernel, groups=groups, epg=epg)

    def ada_spec(k):
        return pl.BlockSpec((1, 1, d), lambda i: ((i // per_b) * 6 + k, 0, 0))

    return pl.pallas_call(
        kern,
        grid=(t // tm,),
        in_specs=[pl.BlockSpec((tm, mw), lambda i: (i, 0)),
                  pl.BlockSpec((tm, d), lambda i: (i, 0)),
                  pl.BlockSpec((tm, d), lambda i: (i, gb_blk)),
                  pl.BlockSpec((tm, d), lambda i: (i, 0)),
                  ada_spec(2), ada_spec(3), ada_spec(4),
                  pl.BlockSpec((1, d), lambda i: (0, 0)),
                  pl.BlockSpec((mw, d), lambda i: (0, 0), pipeline_mode=once),
                  pl.BlockSpec((d, d), lambda i: (0, 0), pipeline_mode=once),
                  pl.BlockSpec((d, LANES), lambda i: (0, 0)),
                  pl.BlockSpec((d, LANES), lambda i: (0, 0)),
                  pl.BlockSpec((ROUTER_ROWS, LANES), lambda i: (0, 0))],
        out_specs=[pl.BlockSpec((tm, d), lambda i: (i, 0)),
                   pl.BlockSpec((tm * _pitch(d), LANES), lambda i: (i, 0)),
                   pl.BlockSpec((MOE_TOPK, tm), lambda i: (0, i)),
                   pl.BlockSpec((MOE_TOPK, tm), lambda i: (0, i))],
        out_shape=[jax.ShapeDtypeStruct((t, d), _F32),
                   jax.ShapeDtypeStruct((t * _pitch(d), LANES), _F32),
                   jax.ShapeDtypeStruct((MOE_TOPK, t), jnp.int32),
                   jax.ShapeDtypeStruct((MOE_TOPK, t), _F32)],
        compiler_params=_params(("arbitrary",)),
        name="mixout",
    )(attn, m_a, proj, x2, ada3, ada3, ada3, norm_g.reshape(1, d), w_moba_out, w_out, wr_hi, wr_lo, br)


def _plan_kernel(eid_ref, dest_ref, tab_ref, carry_scr, start_scr, tri_scr, *, n_exp, rb):
    phase = pl.program_id(0)
    first = pl.program_id(1) == 0
    tc = eid_ref.shape[1]
    nbp = tab_ref.shape[1]

    @pl.when((phase == 0) & first)
    def _():
        carry_scr[...] = jnp.zeros_like(carry_scr)
        earlier = lax.broadcasted_iota(jnp.int32, (tc, tc), 0) < lax.broadcasted_iota(jnp.int32, (tc, tc), 1)
        tri_scr[...] = earlier.astype(_BF16)

    eio = lax.broadcasted_iota(jnp.int32, (n_exp, tc), 0)
    oh0 = (eio == eid_ref[0:1, :]).astype(_F32)
    oh1 = (eio == eid_ref[1:2, :]).astype(_F32)
    oh = oh0 + oh1

    @pl.when((phase == 1) & first)
    def _():
        counts = carry_scr[...].astype(jnp.int32)
        blocks = lax.shift_right_logical(counts + (rb - 1), rb.bit_length() - 1).astype(_F32)
        upto = lax.broadcasted_iota(jnp.int32, (n_exp, n_exp), 0) >= lax.broadcasted_iota(jnp.int32, (n_exp, n_exp), 1)
        end_blk = jnp.dot(upto.astype(_BF16), blocks.astype(_BF16), preferred_element_type=_F32)
        start_scr[...] = (end_blk - blocks) * rb
        carry_scr[...] = jnp.zeros_like(carry_scr)
        blk_id = lax.broadcasted_iota(jnp.int32, (n_exp, nbp), 1).astype(_F32)
        owner = jnp.sum((end_blk[:, 0:1] <= blk_id).astype(_F32), axis=0, keepdims=True)
        owner = jnp.minimum(owner, n_exp - 1.0)
        used = jnp.broadcast_to(end_blk[n_exp - 1:n_exp, 0:1], (1, nbp))
        trow = lax.broadcasted_iota(jnp.int32, tab_ref.shape, 0)
        tab_ref[...] = jnp.where(trow == 0, owner, jnp.where(trow == 1, used, 0.0)).astype(jnp.int32)

    @pl.when(phase == 1)
    def _():
        before = jnp.dot(oh.astype(_BF16), tri_scr[...], preferred_element_type=_F32)
        base = start_scr[:, 0:1] + carry_scr[:, 0:1] + before
        dest_ref[0:1, :] = jnp.sum(oh0 * base, axis=0, keepdims=True).astype(jnp.int32)
        dest_ref[1:2, :] = jnp.sum(oh1 * base, axis=0, keepdims=True).astype(jnp.int32)

    carry_scr[...] = carry_scr[...] + jnp.sum(oh, axis=1, keepdims=True)


def _plan(eid, *, n_exp, tc, rb, n_blocks):
    t = eid.shape[1]
    assert rb & (rb - 1) == 0 and t * MOE_TOPK // rb + n_exp < 256
    nbp = -(-n_blocks // LANES) * LANES
    kern = functools.partial(_plan_kernel, n_exp=n_exp, rb=rb)
    return pl.pallas_call(
        kern,
        grid=(2, t // tc),
        in_specs=[pl.BlockSpec((MOE_TOPK, tc), lambda p, i: (0, i))],
        out_specs=[pl.BlockSpec((MOE_TOPK, tc), lambda p, i: (0, i * p)),
                   pl.BlockSpec((8, nbp), lambda p, i: (0, 0))],
        out_shape=[jax.ShapeDtypeStruct((MOE_TOPK, t), jnp.int32),
                   jax.ShapeDtypeStruct((8, nbp), jnp.int32)],
        scratch_shapes=[pltpu.VMEM((n_exp, LANES), _F32), pltpu.VMEM((n_exp, LANES), _F32),
                        pltpu.VMEM((tc, tc), _BF16)],
        compiler_params=_params(("arbitrary", "arbitrary")),
        name="plan",
    )(eid)


def _invert_kernel(dest_ref, pad_ref, inv_ref, sem):
    i = pl.program_id(0)
    tch = dest_ref.shape[1]

    @pl.when(i == 0)
    def _():
        fill = pltpu.make_async_copy(pad_ref, inv_ref, sem)
        fill.start()
        fill.wait()

    def body(t, c):
        a = (i * tch + t) * MOE_TOPK
        for k in range(MOE_TOPK):
            inv_ref[dest_ref[k, t]] = a + k
        return c

    lax.fori_loop(0, tch, body, 0, unroll=8)


def _invert(dest3, cap):
    steps, _, tch = dest3.shape
    pad = np.zeros((cap,), np.int32)
    return pl.pallas_call(
        _invert_kernel,
        grid=(steps,),
        in_specs=[pl.BlockSpec((None, MOE_TOPK, tch), lambda i: (i, 0, 0), memory_space=pltpu.SMEM),
                  pl.BlockSpec(memory_space=pl.ANY)],
        out_specs=pl.BlockSpec(pad.shape, lambda i: (0,), memory_space=pltpu.SMEM),
        out_shape=jax.ShapeDtypeStruct(pad.shape, jnp.int32),
        scratch_shapes=[pltpu.SemaphoreType.DMA(())],
        compiler_params=_params(("arbitrary",)),
        name="invert",
    )(dest3, jnp.asarray(pad))


def _aligned_rows(chunk, rows_per_chunk):
    row = chunk * rows_per_chunk
    return pl.multiple_of(row, 8) if rows_per_chunk % 8 == 0 else row


def _expert_kernel(be_ref, nu_ref, invc_ref, invn_ref, n2_ref, wg_ref, wu_ref, wd_ref, y_ref,
                   wg_s, wu_s, wd_s, xbuf0, xbuf1, gsem, *, rb, d):
    i = pl.program_id(0)
    last = nu_ref[0] - 1
    used = i <= last
    fresh = (i == 0) | (be_ref[i] != be_ref[jnp.maximum(i - 1, 0)])
    ch = d // LANES
    pitch = _pitch(d)
    xbuf = (xbuf0, xbuf1)

    def gather(inv_ref, p):
        for r in range(rb):
            tok = lax.shift_right_logical(inv_ref[0, r], 1)
            pltpu.make_async_copy(n2_ref.at[pl.ds(_aligned_rows(tok, pitch), ch)],
                                  xbuf[p].at[pl.ds(r * pitch, ch)], gsem.at[p]).start(priority=r % 2)

    def gather_wait(p):
        pltpu.make_async_copy(n2_ref.at[pl.ds(0, rb * ch)], xbuf[p].at[pl.ds(0, rb * ch)], gsem.at[p]).wait()

    @pl.when(i == 0)
    def _():
        gather(invc_ref, 0)

    @pl.when(used & fresh)
    def _():
        wg_s[...] = wg_ref[0].astype(_BF16)
        wu_s[...] = wu_ref[0].astype(_BF16)
        wd_s[...] = wd_ref[0].astype(_BF16)

    def step(p):
        gather_wait(p)
        gather(invn_ref, 1 - p)
        xb = _load_token_major(xbuf[p], rb, d).astype(_BF16)
        gate = jnp.dot(xb, wg_s[...], preferred_element_type=_F32)
        up = jnp.dot(xb, wu_s[...], preferred_element_type=_F32)
        hid = (jax.nn.silu(gate) * up).astype(_BF16)
        _store_token_major(y_ref, jnp.dot(hid, wd_s[...], preferred_element_type=_F32))

        @pl.when(i == last)
        def _():
            gather_wait(1 - p)

    for p in range(2):
        pl.when(used & (lax.rem(i, 2) == p))(functools.partial(step, p))

    @pl.when(jnp.logical_not(used))
    def _():
        y_ref[...] = jnp.zeros_like(y_ref)


def _experts(block_expert, n_used, inv3, n2, w_gate, w_up, w_down):
    nblk, _, rb = inv3.shape
    d, f = w_gate.shape[1:]
    ch = d // LANES

    def inv_spec(shift):
        return pl.BlockSpec((None, 1, rb), lambda i, be, nu: (jnp.clip(i + shift, 0, nu[0] - 1), 0, 0),
                            memory_space=pltpu.SMEM)

    grid_spec = pltpu.PrefetchScalarGridSpec(
        num_scalar_prefetch=2,
        grid=(nblk,),
        in_specs=[inv_spec(0), inv_spec(1),
                  pl.BlockSpec(memory_space=pl.ANY),
                  pl.BlockSpec((1, d, f), lambda i, be, nu: (be[i], 0, 0)),
                  pl.BlockSpec((1, d, f), lambda i, be, nu: (be[i], 0, 0)),
                  pl.BlockSpec((1, f, d), lambda i, be, nu: (be[i], 0, 0))],
        out_specs=pl.BlockSpec((rb * ch, LANES), lambda i, be, nu: (i, 0)),
        scratch_shapes=[pltpu.VMEM((d, f), _BF16), pltpu.VMEM((d, f), _BF16), pltpu.VMEM((f, d), _BF16),
                        pltpu.VMEM((rb * _pitch(d), LANES), _F32), pltpu.VMEM((rb * _pitch(d), LANES), _F32),
                        pltpu.SemaphoreType.DMA((2,))],
    )
    kern = functools.partial(_expert_kernel, rb=rb, d=d)
    return pl.pallas_call(
        kern,
        grid_spec=grid_spec,
        out_shape=jax.ShapeDtypeStruct((nblk * rb * ch, LANES), _F32),
        compiler_params=_params(("arbitrary",)),
        name="experts",
    )(block_expert, n_used, inv3, inv3, n2, w_gate, w_up, w_down)


def _combine_kernel(destc_ref, destn_ref, h_ref, wc_ref, gf_ref, ng_ref, y_ref, o_ref,
                    b00, b01, b10, b11, sem):
    i = pl.program_id(0)
    tk, d = h_ref.shape
    ch = d // LANES
    pitch = b00.shape[0] // tk
    bufs = ((b00, b01), (b10, b11))

    def gather(dest_ref, p):
        for r in range(tk):
            for k in range(MOE_TOPK):
                pltpu.make_async_copy(y_ref.at[pl.ds(_aligned_rows(dest_ref[k, r], ch), ch)],
                                      bufs[p][k].at[pl.ds(r * pitch, ch)], sem.at[p]).start(priority=r % 2)

    def gather_wait(p):
        for k in range(MOE_TOPK):
            pltpu.make_async_copy(y_ref.at[pl.ds(0, tk * ch)], bufs[p][k].at[pl.ds(0, tk * ch)], sem.at[p]).wait()

    @pl.when(i == 0)
    def _():
        gather(destc_ref, 0)

    def step(p):
        gather_wait(p)
        gather(destn_ref, 1 - p)
        y = [_load_token_major(bufs[p][k], tk, d) for k in range(MOE_TOPK)]
        moe = wc_ref[:, 0:1] * y[0] + wc_ref[:, 1:2] * y[1]
        h = h_ref[...] + gf_ref[0] * moe
        o_ref[...] = h * lax.rsqrt(jnp.mean(h * h, axis=-1, keepdims=True) + EPS) * ng_ref[...]

        @pl.when(i == pl.num_programs(0) - 1)
        def _():
            gather_wait(1 - p)

    for p in range(2):
        pl.when(lax.rem(i, 2) == p)(functools.partial(step, p))


def _combine(dest3, h1, w_cols, ada3, norm_g, y_rows, *, seq):
    t, d = h1.shape
    steps, _, tk = dest3.shape
    per_b = seq // tk

    def dest_spec(shift):
        return pl.BlockSpec((None, MOE_TOPK, tk), lambda i: (jnp.minimum(i + shift, steps - 1), 0, 0),
                            memory_space=pltpu.SMEM)

    return pl.pallas_call(
        _combine_kernel,
        grid=(steps,),
        in_specs=[dest_spec(0), dest_spec(1),
                  pl.BlockSpec((tk, d), lambda i: (i, 0)),
                  pl.BlockSpec((tk, LANES), lambda i: (i, 0)),
                  pl.BlockSpec((1, 1, d), lambda i: ((i // per_b) * 6 + 5, 0, 0)),
                  pl.BlockSpec((1, d), lambda i: (0, 0)),
                  pl.BlockSpec(memory_space=pl.ANY)],
        out_specs=pl.BlockSpec((tk, d), lambda i: (i, 0)),
        out_shape=jax.ShapeDtypeStruct((t, d), _F32),
        scratch_shapes=[*[pltpu.VMEM((tk * _pitch(d), LANES), _F32) for _ in range(2 * MOE_TOPK)],
                        pltpu.SemaphoreType.DMA((2,))],
        compiler_params=_params(("arbitrary",)),
        name="combine",
    )(dest3, dest3, h1, w_cols, ada3, norm_g.reshape(1, d), y_rows)


def _rotary_tables(positions, dh):
    inv_freq = ROPE_THETA ** (-jnp.arange(0, dh, 2, dtype=_F32) / dh)
    ang = positions.astype(_F32)[..., None] * inv_freq
    cos, sin = jnp.cos(ang), jnp.sin(ang)
    t = cos.shape[0] * cos.shape[1]
    cos = jnp.concatenate([cos, cos], axis=-1).reshape(t, dh)
    sin = jnp.concatenate([-sin, sin], axis=-1).reshape(t, dh)
    return cos, sin


def _tile(n, want):
    while n % want:
        want //= 2
    return want


def kernel(x, c, positions, w_ada, b_ada, norm_mix_g, w_in, sgu_ln_g, sgu_ln_b, sgu_w_s, sgu_b_s,
           w_sgu_out, w_moba_out, w_out, norm_ffn_g, w_route_group, b_route_group, w_route_expert,
           b_route_expert, w_exp_gate, w_exp_up, w_exp_down, norm_final_g):
    batch, seq, d = x.shape
    depth = w_ada.shape[0]
    t = batch * seq
    sw = sgu_ln_g.shape[1]
    mw = w_moba_out.shape[1]
    dh = mw // MOBA_HEADS
    groups = w_route_group.shape[2]
    n_exp = w_route_expert.shape[2]
    epg = n_exp // groups
    rb = EXPERT_ROW_BLOCK
    cap = (t * MOE_TOPK // rb + n_exp) * rb
    assert depth == 1, "the final RMSNorm is fused into the single layer's combine"
    assert dh == LANES and seq % MOBA_BLOCK == 0 and sw * 2 == d and mw % sw == 0
    assert groups + n_exp <= ROUTER_ROWS and (t * MOE_TOPK) % rb == 0

    cos, sin = _rotary_tables(positions, dh)
    c_pad = jnp.zeros((8, d), _F32).at[:batch].set(c)
    h = x.reshape(t, d)

    for l in range(depth):
        ada = _ada(c_pad, w_ada[l], b_ada[l], _tile(6 * d, 1024))
        ada3 = ada[:batch].reshape(batch * 6, 1, d)

        proj = _inproj(h, ada3, norm_mix_g[l], w_in[l].astype(_BF16), cos, sin, sgu_ln_g[l], sgu_ln_b[l],
                       seq=seq, sw=sw, mw=mw, dh=dh, tm=_tile(seq, 1024))
        bs_wide = jnp.repeat(sgu_b_s[l].T, sw // sgu_w_s.shape[1], axis=1)
        m_a = _sgu(proj, sgu_w_s[l], bs_wide, w_sgu_out[l].astype(_BF16), sw=sw, mw=mw, tm=_tile(seq, 512))
        attn = _moba(proj, batch=batch, seq=seq, sw=sw, mw=mw, dh=dh, hg=min(MOBA_HEAD_GROUP, MOBA_HEADS))

        wr = jnp.concatenate([w_route_group[l], w_route_expert[l]], axis=1)
        wr_hi, wr_lo = _split_bf16(jnp.zeros((d, LANES), _F32).at[:, :groups + n_exp].set(wr))
        br = jnp.concatenate([b_route_group[l], b_route_expert[l].reshape(-1)])
        br = jnp.broadcast_to(jnp.zeros((ROUTER_ROWS,), _F32).at[:groups + n_exp].set(br)[:, None],
                              (ROUTER_ROWS, LANES))
        h1, n2, eid, wt = _mixout(attn, m_a, proj, h, ada3, norm_ffn_g[l], w_moba_out[l].astype(_BF16),
                                  w_out[l].astype(_BF16), wr_hi, wr_lo, br,
                                  seq=seq, sw=sw, mw=mw, groups=groups, epg=epg, tm=_tile(seq, 256))

        dest, tab = _plan(eid, n_exp=n_exp, tc=_tile(t, 512), rb=rb, n_blocks=cap // rb)
        def token_blocks(size):
            return dest.reshape(MOE_TOPK, t // size, size).transpose(1, 0, 2)

        inv = _invert(token_blocks(_tile(t, 2048)), cap)
        y_rows = _experts(tab[0, :cap // rb], tab[1, :1], inv.reshape(cap // rb, 1, rb), n2,
                          w_exp_gate[l], w_exp_up[l], w_exp_down[l])
        w_cols = jnp.zeros((t, LANES), _F32).at[:, :MOE_TOPK].set(wt.T)
        h = _combine(token_blocks(_tile(seq, 256)), h1, w_cols, ada3, norm_final_g, y_rows, seq=seq)

    return h.reshape(batch, seq, d)
```

```python
import functools

import jax
import jax.numpy as jnp
import numpy as np
from jax import lax
from jax.experimental import pallas as pl
from jax.experimental.pallas import tpu as pltpu

MOBA_HEADS = 16
MOBA_BLOCK = 256
MOBA_TOPK = 3
SGU_CHUNK = 128
ROPE_THETA = 10000.0
MOE_TOPK = 2
EPS = 1e-6
NEG_INF = -1e30
LOG2_E = 1.4426950408889634

LANES = 128
TOKEN_PAD_ROWS = 8
EXPERT_ROW_BLOCK = 256
INPROJ_CHUNK = 256
NORM_BANDS = 4
MOBA_HEAD_GROUP = 8
ROUTER_ROWS = 40
VMEM_LIMIT = 56 * 1024 * 1024

_F32 = jnp.float32
_BF16 = jnp.bfloat16
_NT = (((1,), (1,)), ((), ()))


def _params(semantics, vmem=VMEM_LIMIT):
    return pltpu.CompilerParams(dimension_semantics=semantics, vmem_limit_bytes=vmem)


def _pitch(d):
    return d // LANES + TOKEN_PAD_ROWS


def _store_token_major(ref, val):
    n, d = val.shape
    pitch = ref.shape[0] // n
    for c in range(pitch):
        piece = val[:, c * LANES:(c + 1) * LANES] if c < d // LANES else jnp.zeros((n, LANES), val.dtype)
        ref[pl.ds(c, n, stride=pitch), :] = piece


def _load_token_major(ref, n, d, first=0, pitch=None):
    pitch = pitch or ref.shape[0] // n
    return jnp.concatenate([ref[pl.ds(first + c, n, stride=pitch), :] for c in range(d // LANES)], axis=1)


def _split_bf16(a):
    hi = a.astype(_BF16)
    lo = (a - hi.astype(_F32)).astype(_BF16)
    return hi, lo


def _ada_kernel(c_ref, w_ref, b_ref, o_ref):
    ca_hi, ca_lo = _split_bf16(jax.nn.silu(c_ref[...]))
    w_hi, w_lo = _split_bf16(w_ref[...])
    acc = jnp.dot(ca_hi, w_hi, preferred_element_type=_F32)
    acc += jnp.dot(ca_hi, w_lo, preferred_element_type=_F32)
    acc += jnp.dot(ca_lo, w_hi, preferred_element_type=_F32)
    o_ref[...] = acc + b_ref[...]


def _ada(c_pad, w_ada, b_ada, tn):
    rows, d = c_pad.shape
    n = w_ada.shape[1]
    return pl.pallas_call(
        _ada_kernel,
        grid=(n // tn,),
        in_specs=[pl.BlockSpec((rows, d), lambda j: (0, 0)),
                  pl.BlockSpec((d, tn), lambda j: (0, j)),
                  pl.BlockSpec((1, tn), lambda j: (0, j))],
        out_specs=pl.BlockSpec((rows, tn), lambda j: (0, j)),
        out_shape=jax.ShapeDtypeStruct((rows, n), _F32),
        compiler_params=_params(("arbitrary",)),
        name="ada",
    )(c_pad, w_ada, b_ada.reshape(1, n))


def _rms_modulate(x, g, sc, sh):
    y = x * lax.rsqrt(jnp.mean(x * x, axis=-1, keepdims=True) + EPS) * g
    return y * (1.0 + sc) + sh


def _inproj_kernel(x_ref, g_ref, sc_ref, sh_ref, w_ref, cos_ref, sin_ref, lng_ref, lnb_ref,
                   o_ref, n_scr, z_scr, *, q0, k0, v0, dh, scale, cw):
    j = pl.program_id(1)
    tm, tn = o_ref.shape

    def chunk(c):
        return jnp.dot(n_scr[...], w_ref[:, c * cw:(c + 1) * cw], preferred_element_type=_F32)

    @pl.when(j == 0)
    def _():
        band = tm // NORM_BANDS
        for r in range(NORM_BANDS):
            rows = slice(r * band, (r + 1) * band)
            nb = _rms_modulate(x_ref[rows, :], g_ref[...], sc_ref[0], sh_ref[0]).astype(_BF16)
            n_scr[rows, :] = nb
            for c in range(tn // cw):
                cols = slice(c * cw, (c + 1) * cw)
                o_ref[rows, cols] = jax.nn.gelu(
                    jnp.dot(nb, w_ref[:, cols], preferred_element_type=_F32)).astype(_BF16)

    @pl.when(j == 1)
    def _():
        for c in range(tn // cw):
            z_scr[:, c * cw:(c + 1) * cw] = jax.nn.gelu(chunk(c))
        z = z_scr[...]
        zc = z - jnp.mean(z, axis=-1, keepdims=True)
        var = jnp.mean(zc * zc, axis=-1, keepdims=True)
        o_ref[...] = (zc * lax.rsqrt(var + EPS) * lng_ref[...] + lnb_ref[...]).astype(_BF16)

    @pl.when(j >= q0)
    def _():
        mult = jnp.where(j < k0, scale, 1.0)
        cos = jnp.where(j < v0, cos_ref[...], 1.0) * mult
        sin = jnp.where(j < v0, sin_ref[...], 0.0) * mult
        for c in range(tn // cw):
            acc = chunk(c)
            for h in range(cw // dh):
                xh = acc[:, h * dh:(h + 1) * dh]
                rot = xh * cos + pltpu.roll(xh, dh // 2, 1) * sin
                o_ref[:, c * cw + h * dh:c * cw + (h + 1) * dh] = rot.astype(_BF16)


def _inproj(x2, ada3, norm_g, w_in, cos, sin, ln_g, ln_b, *, seq, sw, mw, dh, tm):
    t, d = x2.shape
    n = w_in.shape[1]
    tn = sw
    per_b = seq // tm
    q0 = 2
    k0 = q0 + mw // tn
    v0 = k0 + mw // tn
    cw = _tile(tn, INPROJ_CHUNK)
    assert cw % dh == 0
    kern = functools.partial(_inproj_kernel, q0=q0, k0=k0, v0=v0, dh=dh, scale=dh ** -0.5 * LOG2_E, cw=cw)
    return pl.pallas_call(
        kern,
        grid=(t // tm, n // tn),
        in_specs=[pl.BlockSpec((tm, d), lambda i, j: (i, 0)),
                  pl.BlockSpec((1, d), lambda i, j: (0, 0)),
                  pl.BlockSpec((1, 1, d), lambda i, j: ((i // per_b) * 6 + 1, 0, 0)),
                  pl.BlockSpec((1, 1, d), lambda i, j: ((i // per_b) * 6 + 0, 0, 0)),
                  pl.BlockSpec((d, tn), lambda i, j: (0, j)),
                  pl.BlockSpec((tm, dh), lambda i, j: (i, 0)),
                  pl.BlockSpec((tm, dh), lambda i, j: (i, 0)),
                  pl.BlockSpec((1, sw), lambda i, j: (0, 0)),
                  pl.BlockSpec((1, sw), lambda i, j: (0, 0))],
        out_specs=pl.BlockSpec((tm, tn), lambda i, j: (i, j)),
        out_shape=jax.ShapeDtypeStruct((t, n), _BF16),
        scratch_shapes=[pltpu.VMEM((tm, d), _BF16), pltpu.VMEM((tm, tn), _F32)],
        compiler_params=_params(("arbitrary", "arbitrary")),
        name="inproj",
    )(x2, norm_g.reshape(1, d), ada3, ada3, w_in, cos, sin, ln_g.reshape(1, sw), ln_b.reshape(1, sw))


def _sgu_kernel(u_ref, v_ref, ga_ref, ws_ref, bs_ref, wo_ref, o_ref, gated_scr, *, groups, cg):
    c = SGU_CHUNK
    tri = lax.broadcasted_iota(jnp.int32, (c, c), 0) >= lax.broadcasted_iota(jnp.int32, (c, c), 1)
    for g in range(groups):
        wg = jnp.where(tri, ws_ref[g], 0.0).astype(_BF16)
        cols = slice(g * cg, (g + 1) * cg)
        for ci in range(u_ref.shape[0] // c):
            rows = slice(ci * c, (ci + 1) * c)
            sv = jnp.dot(wg, v_ref[rows, cols], preferred_element_type=_F32) + bs_ref[:, cols]
            gated_scr[rows, cols] = (u_ref[rows, cols].astype(_F32) * sv).astype(_BF16)
    ya = jnp.dot(gated_scr[...], wo_ref[...], preferred_element_type=_F32)
    o_ref[...] = (jax.nn.sigmoid(ga_ref[...].astype(_F32)) * ya).astype(_BF16)


def _sgu(proj, w_s, bs_wide, w_sgu_out, *, sw, mw, tm):
    t = proj.shape[0]
    d = w_sgu_out.shape[1]
    groups = w_s.shape[0]
    ga_blk = (2 * sw + 3 * mw) // d
    kern = functools.partial(_sgu_kernel, groups=groups, cg=sw // groups)
    return pl.pallas_call(
        kern,
        grid=(t // tm,),
        in_specs=[pl.BlockSpec((tm, sw), lambda i: (i, 0)),
                  pl.BlockSpec((tm, sw), lambda i: (i, 1)),
                  pl.BlockSpec((tm, d), lambda i: (i, ga_blk)),
                  pl.BlockSpec(w_s.shape, lambda i: (0, 0, 0)),
                  pl.BlockSpec(bs_wide.shape, lambda i: (0, 0)),
                  pl.BlockSpec(w_sgu_out.shape, lambda i: (0, 0))],
        out_specs=pl.BlockSpec((tm, d), lambda i: (i, 0)),
        out_shape=jax.ShapeDtypeStruct((t, d), _BF16),
        scratch_shapes=[pltpu.VMEM((tm, sw), _BF16)],
        compiler_params=_params(("arbitrary",)),
        name="sgu",
    )(proj, proj, proj, w_s, bs_wide, w_sgu_out)


def _moba_kernel(q_ref, k_ref, v_ref, o_ref, kmh_scr, kml_scr, vt_scr, sel_scr,
                 p_scr, acc_scr, m_scr, l_scr, a_scr, *, nb, topk, hg, dh):
    qi = pl.program_id(2)
    blk = MOBA_BLOCK
    tq = q_ref.shape[0]

    @pl.when(qi == 0)
    def _():
        for h in range(hg):
            cols = slice(h * dh, (h + 1) * dh)
            kf = k_ref[:, cols].astype(_F32).reshape(nb, blk, dh)
            km_hi, km_lo = _split_bf16(jnp.mean(kf, axis=1))
            kmh_scr[h * nb:(h + 1) * nb, :] = km_hi
            kml_scr[h * nb:(h + 1) * nb, :] = km_lo
            for c in range(nb):
                rows = slice(c * blk, (c + 1) * blk)
                vt_scr[cols, rows] = v_ref[rows, cols].astype(_F32).T.astype(_BF16)

    row = lax.broadcasted_iota(jnp.int32, (nb, tq), 0)
    past = row < qi
    kpos = lax.broadcasted_iota(jnp.int32, (blk, tq), 0)
    qpos = lax.broadcasted_iota(jnp.int32, (blk, tq), 1)
    own = pl.multiple_of(qi * blk, blk)

    def scores(start, h):
        cols = slice(h * dh, (h + 1) * dh)
        return lax.dot_general(k_ref[pl.ds(start, blk), cols], q_ref[:, cols], _NT,
                               preferred_element_type=_F32)

    def weighted_values(start, h, p):
        return jnp.dot(vt_scr[h * dh:(h + 1) * dh, pl.ds(start, blk)], p, preferred_element_type=_F32)

    for h in range(hg):
        hrows = slice(h * nb, (h + 1) * nb)
        q = q_ref[:, h * dh:(h + 1) * dh]
        gate = (lax.dot_general(kmh_scr[hrows, :], q, _NT, preferred_element_type=_F32)
                + lax.dot_general(kml_scr[hrows, :], q, _NT, preferred_element_type=_F32))
        gm = jnp.where(past, gate, NEG_INF)
        beaten = jnp.zeros((nb, tq), jnp.int32)
        for jp in range(nb):
            other = gm[jp:jp + 1, :]
            wins = (other > gm) | ((other == gm) & (jp < row))
            beaten += wins.astype(jnp.int32)
        sel_scr[hrows, :] = (past & (beaten < topk)).astype(_F32)

        s = jnp.where(kpos <= qpos, scores(own, h), NEG_INF)
        m0 = jnp.max(s, axis=0, keepdims=True)
        p = jnp.exp2(s - m0)
        m_scr[h] = m0
        l_scr[h] = jnp.sum(p, axis=0, keepdims=True)
        a_scr[h] = jnp.ones_like(m0)
        acc_scr[h] = jnp.zeros(acc_scr.shape[1:], _F32)
        p_scr[0, h] = p.astype(_BF16)

    def body(j, c):
        par = lax.rem(j, 2)
        prev = pl.multiple_of(jnp.where(j == 0, qi, j - 1) * blk, blk)
        for h in range(hg):
            acc_scr[h] = a_scr[h] * acc_scr[h] + weighted_values(prev, h, p_scr[par, h])
        cur = pl.multiple_of(j * blk, blk)
        for h in range(hg):
            s = jnp.where(sel_scr[pl.ds(h * nb + j, 1), :] > 0.0, scores(cur, h), NEG_INF)
            m = m_scr[h]
            m_new = jnp.maximum(m, jnp.max(s, axis=0, keepdims=True))
            alpha = jnp.exp2(m - m_new)
            p = jnp.exp2(s - m_new)
            l_scr[h] = alpha * l_scr[h] + jnp.sum(p, axis=0, keepdims=True)
            m_scr[h] = m_new
            a_scr[h] = alpha
            p_scr[1 - par, h] = p.astype(_BF16)
        return c

    lax.fori_loop(0, qi, body, 0)
    last = pl.multiple_of(jnp.where(qi == 0, qi, qi - 1) * blk, blk)
    for h in range(hg):
        acc = a_scr[h] * acc_scr[h] + weighted_values(last, h, p_scr[lax.rem(qi, 2), h])
        o_ref[:, h * dh:(h + 1) * dh] = (acc / l_scr[h]).T.astype(_BF16)


def _moba(proj, *, batch, seq, sw, mw, dh, hg):
    t = proj.shape[0]
    heads = mw // dh
    nb = seq // MOBA_BLOCK
    tq = MOBA_BLOCK
    nq = seq // tq
    gw = hg * dh
    qc = 2 * sw // gw
    kc = qc + heads // hg
    vc = kc + heads // hg
    kern = functools.partial(_moba_kernel, nb=nb, topk=MOBA_TOPK, hg=hg, dh=dh)
    return pl.pallas_call(
        kern,
        grid=(batch, heads // hg, nq),
        in_specs=[pl.BlockSpec((tq, gw), lambda b, h, i: (b * nq + i, qc + h)),
                  pl.BlockSpec((seq, gw), lambda b, h, i: (b, kc + h)),
                  pl.BlockSpec((seq, gw), lambda b, h, i: (b, vc + h))],
        out_specs=pl.BlockSpec((tq, gw), lambda b, h, i: (b * nq + i, h)),
        out_shape=jax.ShapeDtypeStruct((t, mw), _BF16),
        scratch_shapes=[pltpu.VMEM((hg * nb, dh), _BF16), pltpu.VMEM((hg * nb, dh), _BF16),
                        pltpu.VMEM((gw, seq), _BF16), pltpu.VMEM((hg * nb, tq), _F32),
                        pltpu.VMEM((2, hg, MOBA_BLOCK, tq), _BF16),
                        pltpu.VMEM((hg, dh, tq), _F32), pltpu.VMEM((hg, 1, tq), _F32),
                        pltpu.VMEM((hg, 1, tq), _F32), pltpu.VMEM((hg, 1, tq), _F32)],
        compiler_params=_params(("arbitrary", "arbitrary", "arbitrary")),
        name="moba",
    )(proj, proj, proj)


def _mixout_kernel(attn_ref, ma_ref, gb_ref, x_ref, gm_ref, shf_ref, scf_ref, ng_ref,
                   wmo_ref, wo_ref, wrh_ref, wrl_ref, br_ref,
                   h_ref, n2_ref, eid_ref, wt_ref, *, groups, epg):
    yb = jnp.dot(attn_ref[...], wmo_ref[...], preferred_element_type=_F32)
    merged = ma_ref[...].astype(_F32) + jax.nn.sigmoid(gb_ref[...].astype(_F32)) * yb
    mix = jnp.dot(merged.astype(_BF16), wo_ref[...], preferred_element_type=_F32)
    h = x_ref[...] + gm_ref[0] * mix
    h_ref[...] = h
    n2 = _rms_modulate(h, ng_ref[...], scf_ref[0], shf_ref[0])
    _store_token_major(n2_ref, n2)

    n_hi, n_lo = _split_bf16(n2)
    lg = (jnp.dot(n_hi, wrh_ref[...], preferred_element_type=_F32)
          + jnp.dot(n_lo, wrh_ref[...], preferred_element_type=_F32)
          + jnp.dot(n_hi, wrl_ref[...], preferred_element_type=_F32))
    lg = lg.T[:ROUTER_ROWS, :] + br_ref[:, 0:1]

    gl = [lg[g:g + 1, :] for g in range(groups)]
    gmax = functools.reduce(jnp.maximum, gl)
    denom = functools.reduce(jnp.add, [jnp.exp(v - gmax) for v in gl])
    p_group = 1.0 / denom
    g_sel = jnp.full(gmax.shape, groups - 1, jnp.int32)
    for g in range(groups - 2, -1, -1):
        g_sel = jnp.where(gl[g] == gmax, g, g_sel)

    el = []
    for e in range(epg):
        v = lg[groups + e:groups + e + 1, :]
        for g in range(1, groups):
            r = groups + g * epg + e
            v = jnp.where(g_sel == g, lg[r:r + 1, :], v)
        el.append(v)

    def top1(vals):
        vmax = functools.reduce(jnp.maximum, vals)
        idx = jnp.full(vmax.shape, epg - 1, jnp.int32)
        for e in range(epg - 2, -1, -1):
            idx = jnp.where(vals[e] == vmax, e, idx)
        return vmax, idx

    v1, i1 = top1(el)
    v2, i2 = top1([jnp.where(i1 == e, -jnp.inf, el[e]) for e in range(epg)])
    b = jnp.exp(v2 - v1)
    eid_ref[0:1, :] = g_sel * epg + i1
    eid_ref[1:2, :] = g_sel * epg + i2
    wt_ref[0:1, :] = (1.0 / (1.0 + b)) * p_group
    wt_ref[1:2, :] = (b / (1.0 + b)) * p_group


def _mixout(attn, m_a, proj, x2, ada3, norm_g, w_moba_out, w_out, wr_hi, wr_lo, br,
            *, seq, sw, mw, groups, epg, tm):
    t, d = x2.shape
    per_b = seq // tm
    gb_blk = (2 * sw + 3 * mw) // d + 1
    once = pl.Buffered(1)
    kern = functools.partial(_mixout_kernel, groups=groups, epg=epg)

    def ada_spec(k):
        return pl.BlockSpec((1, 1, d), lambda i: ((i // per_b) * 6 + k, 0, 0))

    return pl.pallas_call(
        kern,
        grid=(t // tm,),
        in_specs=[pl.BlockSpec((tm, mw), lambda i: (i, 0)),
                  pl.BlockSpec((tm, d), lambda i: (i, 0)),
                  pl.BlockSpec((tm, d), lambda i: (i, gb_blk)),
                  pl.BlockSpec((tm, d), lambda i: (i, 0)),
                  ada_spec(2), ada_spec(3), ada_spec(4),
                  pl.BlockSpec((1, d), lambda i: (0, 0)),
                  pl.BlockSpec((mw, d), lambda i: (0, 0), pipeline_mode=once),
                  pl.BlockSpec((d, d), lambda i: (0, 0), pipeline_mode=once),
                  pl.BlockSpec((d, LANES), lambda i: (0, 0)),
                  pl.BlockSpec((d, LANES), lambda i: (0, 0)),
                  pl.BlockSpec((ROUTER_ROWS, LANES), lambda i: (0, 0))],
        out_specs=[pl.BlockSpec((tm, d), lambda i: (i, 0)),
                   pl.BlockSpec((tm * _pitch(d), LANES), lambda i: (i, 0)),
                   pl.BlockSpec((MOE_TOPK, tm), lambda i: (0, i)),
                   pl.BlockSpec((MOE_TOPK, tm), lambda i: (0, i))],
        out_shape=[jax.ShapeDtypeStruct((t, d), _F32),
                   jax.ShapeDtypeStruct((t * _pitch(d), LANES), _F32),
                   jax.ShapeDtypeStruct((MOE_TOPK, t), jnp.int32),
                   jax.ShapeDtypeStruct((MOE_TOPK, t), _F32)],
        compiler_params=_params(("arbitrary",)),
        name="mixout",
    )(attn, m_a, proj, x2, ada3, ada3, ada3, norm_g.reshape(1, d), w_moba_out, w_out, wr_hi, wr_lo, br)


def _plan_kernel(eid_ref, dest_ref, tab_ref, carry_scr, start_scr, tri_scr, *, n_exp, rb):
    phase = pl.program_id(0)
    first = pl.program_id(1) == 0
    tc = eid_ref.shape[1]
    nbp = tab_ref.shape[1]

    @pl.when((phase == 0) & first)
    def _():
        carry_scr[...] = jnp.zeros_like(carry_scr)
        earlier = lax.broadcasted_iota(jnp.int32, (tc, tc), 0) < lax.broadcasted_iota(jnp.int32, (tc, tc), 1)
        tri_scr[...] = earlier.astype(_BF16)

    eio = lax.broadcasted_iota(jnp.int32, (n_exp, tc), 0)
    oh0 = (eio == eid_ref[0:1, :]).astype(_F32)
    oh1 = (eio == eid_ref[1:2, :]).astype(_F32)
    oh = oh0 + oh1

    @pl.when((phase == 1) & first)
    def _():
        counts = carry_scr[...].astype(jnp.int32)
        blocks = lax.shift_right_logical(counts + (rb - 1), rb.bit_length() - 1).astype(_F32)
        upto = lax.broadcasted_iota(jnp.int32, (n_exp, n_exp), 0) >= lax.broadcasted_iota(jnp.int32, (n_exp, n_exp), 1)
        end_blk = jnp.dot(upto.astype(_BF16), blocks.astype(_BF16), preferred_element_type=_F32)
        start_scr[...] = (end_blk - blocks) * rb
        carry_scr[...] = jnp.zeros_like(carry_scr)
        blk_id = lax.broadcasted_iota(jnp.int32, (n_exp, nbp), 1).astype(_F32)
        owner = jnp.sum((end_blk[:, 0:1] <= blk_id).astype(_F32), axis=0, keepdims=True)
        owner = jnp.minimum(owner, n_exp - 1.0)
        used = jnp.broadcast_to(end_blk[n_exp - 1:n_exp, 0:1], (1, nbp))
        trow = lax.broadcasted_iota(jnp.int32, tab_ref.shape, 0)
        tab_ref[...] = jnp.where(trow == 0, owner, jnp.where(trow == 1, used, 0.0)).astype(jnp.int32)

    @pl.when(phase == 1)
    def _():
        before = jnp.dot(oh.astype(_BF16), tri_scr[...], preferred_element_type=_F32)
        base = start_scr[:, 0:1] + carry_scr[:, 0:1] + before
        dest_ref[0:1, :] = jnp.sum(oh0 * base, axis=0, keepdims=True).astype(jnp.int32)
        dest_ref[1:2, :] = jnp.sum(oh1 * base, axis=0, keepdims=True).astype(jnp.int32)

    carry_scr[...] = carry_scr[...] + jnp.sum(oh, axis=1, keepdims=True)


def _plan(eid, *, n_exp, tc, rb, n_blocks):
    t = eid.shape[1]
    assert rb & (rb - 1) == 0 and t * MOE_TOPK // rb + n_exp < 256
    nbp = -(-n_blocks // LANES) * LANES
    kern = functools.partial(_plan_kernel, n_exp=n_exp, rb=rb)
    return pl.pallas_call(
        kern,
        grid=(2, t // tc),
        in_specs=[pl.BlockSpec((MOE_TOPK, tc), lambda p, i: (0, i))],
        out_specs=[pl.BlockSpec((MOE_TOPK, tc), lambda p, i: (0, i * p)),
                   pl.BlockSpec((8, nbp), lambda p, i: (0, 0))],
        out_shape=[jax.ShapeDtypeStruct((MOE_TOPK, t), jnp.int32),
                   jax.ShapeDtypeStruct((8, nbp), jnp.int32)],
        scratch_shapes=[pltpu.VMEM((n_exp, LANES), _F32), pltpu.VMEM((n_exp, LANES), _F32),
                        pltpu.VMEM((tc, tc), _BF16)],
        compiler_params=_params(("arbitrary", "arbitrary")),
        name="plan",
    )(eid)


def _invert_kernel(dest_ref, pad_ref, inv_ref, sem):
    i = pl.program_id(0)
    tch = dest_ref.shape[1]

    @pl.when(i == 0)
    def _():
        fill = pltpu.make_async_copy(pad_ref, inv_ref, sem)
        fill.start()
        fill.wait()

    def body(t, c):
        a = (i * tch + t) * MOE_TOPK
        for k in range(MOE_TOPK):
            inv_ref[dest_ref[k, t]] = a + k
        return c

    lax.fori_loop(0, tch, body, 0, unroll=8)


def _invert(dest3, cap):
    steps, _, tch = dest3.shape
    pad = np.zeros((cap,), np.int32)
    return pl.pallas_call(
        _invert_kernel,
        grid=(steps,),
        in_specs=[pl.BlockSpec((None, MOE_TOPK, tch), lambda i: (i, 0, 0), memory_space=pltpu.SMEM),
                  pl.BlockSpec(memory_space=pl.ANY)],
        out_specs=pl.BlockSpec(pad.shape, lambda i: (0,), memory_space=pltpu.SMEM),
        out_shape=jax.ShapeDtypeStruct(pad.shape, jnp.int32),
        scratch_shapes=[pltpu.SemaphoreType.DMA(())],
        compiler_params=_params(("arbitrary",)),
        name="invert",
    )(dest3, jnp.asarray(pad))


def _aligned_rows(chunk, rows_per_chunk):
    row = chunk * rows_per_chunk
    return pl.multiple_of(row, 8) if rows_per_chunk % 8 == 0 else row


def _expert_kernel(be_ref, nu_ref, invc_ref, invn_ref, n2_ref, wg_ref, wu_ref, wd_ref, y_ref,
                   wg_s, wu_s, wd_s, xbuf0, xbuf1, gsem, *, rb, d):
    i = pl.program_id(0)
    last = nu_ref[0] - 1
    used = i <= last
    fresh = (i == 0) | (be_ref[i] != be_ref[jnp.maximum(i - 1, 0)])
    ch = d // LANES
    pitch = _pitch(d)
    xbuf = (xbuf0, xbuf1)

    def gather(inv_ref, p):
        for r in range(rb):
            tok = lax.shift_right_logical(inv_ref[0, r], 1)
            pltpu.make_async_copy(n2_ref.at[pl.ds(_aligned_rows(tok, pitch), ch)],
                                  xbuf[p].at[pl.ds(r * pitch, ch)], gsem.at[p]).start(priority=r % 2)

    def gather_wait(p):
        pltpu.make_async_copy(n2_ref.at[pl.ds(0, rb * ch)], xbuf[p].at[pl.ds(0, rb * ch)], gsem.at[p]).wait()

    @pl.when(i == 0)
    def _():
        gather(invc_ref, 0)

    @pl.when(used & fresh)
    def _():
        wg_s[...] = wg_ref[0].astype(_BF16)
        wu_s[...] = wu_ref[0].astype(_BF16)
        wd_s[...] = wd_ref[0].astype(_BF16)

    def step(p):
        gather_wait(p)
        gather(invn_ref, 1 - p)
        xb = _load_token_major(xbuf[p], rb, d).astype(_BF16)
        gate = jnp.dot(xb, wg_s[...], preferred_element_type=_F32)
        up = jnp.dot(xb, wu_s[...], preferred_element_type=_F32)
        hid = (jax.nn.silu(gate) * up).astype(_BF16)
        _store_token_major(y_ref, jnp.dot(hid, wd_s[...], preferred_element_type=_F32))

        @pl.when(i == last)
        def _():
            gather_wait(1 - p)

    for p in range(2):
        pl.when(used & (lax.rem(i, 2) == p))(functools.partial(step, p))

    @pl.when(jnp.logical_not(used))
    def _():
        y_ref[...] = jnp.zeros_like(y_ref)


def _experts(block_expert, n_used, inv3, n2, w_gate, w_up, w_down):
    nblk, _, rb = inv3.shape
    d, f = w_gate.shape[1:]
    ch = d // LANES

    def inv_spec(shift):
        return pl.BlockSpec((None, 1, rb), lambda i, be, nu: (jnp.clip(i + shift, 0, nu[0] - 1), 0, 0),
                            memory_space=pltpu.SMEM)

    grid_spec = pltpu.PrefetchScalarGridSpec(
        num_scalar_prefetch=2,
        grid=(nblk,),
        in_specs=[inv_spec(0), inv_spec(1),
                  pl.BlockSpec(memory_space=pl.ANY),
                  pl.BlockSpec((1, d, f), lambda i, be, nu: (be[i], 0, 0)),
                  pl.BlockSpec((1, d, f), lambda i, be, nu: (be[i], 0, 0)),
                  pl.BlockSpec((1, f, d), lambda i, be, nu: (be[i], 0, 0))],
        out_specs=pl.BlockSpec((rb * ch, LANES), lambda i, be, nu: (i, 0)),
        scratch_shapes=[pltpu.VMEM((d, f), _BF16), pltpu.VMEM((d, f), _BF16), pltpu.VMEM((f, d), _BF16),
                        pltpu.VMEM((rb * _pitch(d), LANES), _F32), pltpu.VMEM((rb * _pitch(d), LANES), _F32),
                        pltpu.SemaphoreType.DMA((2,))],
    )
    kern = functools.partial(_expert_kernel, rb=rb, d=d)
    return pl.pallas_call(
        kern,
        grid_spec=grid_spec,
        out_shape=jax.ShapeDtypeStruct((nblk * rb * ch, LANES), _F32),
        compiler_params=_params(("arbitrary",)),
        name="experts",
    )(block_expert, n_used, inv3, inv3, n2, w_gate, w_up, w_down)


def _combine_kernel(destc_ref, destn_ref, h_ref, wc_ref, gf_ref, ng_ref, y_ref, o_ref,
                    b00, b01, b10, b11, sem):
    i = pl.program_id(0)
    tk, d = h_ref.shape
    ch = d // LANES
    pitch = b00.shape[0] // tk
    bufs = ((b00, b01), (b10, b11))

    def gather(dest_ref, p):
        for r in range(tk):
            for k in range(MOE_TOPK):
                pltpu.make_async_copy(y_ref.at[pl.ds(_aligned_rows(dest_ref[k, r], ch), ch)],
                                      bufs[p][k].at[pl.ds(r * pitch, ch)], sem.at[p]).start(priority=r % 2)

    def gather_wait(p):
        for k in range(MOE_TOPK):
            pltpu.make_async_copy(y_ref.at[pl.ds(0, tk * ch)], bufs[p][k].at[pl.ds(0, tk * ch)], sem.at[p]).wait()

    @pl.when(i == 0)
    def _():
        gather(destc_ref, 0)

    def step(p):
        gather_wait(p)
        gather(destn_ref, 1 - p)
        y = [_load_token_major(bufs[p][k], tk, d) for k in range(MOE_TOPK)]
        moe = wc_ref[:, 0:1] * y[0] + wc_ref[:, 1:2] * y[1]
        h = h_ref[...] + gf_ref[0] * moe
        o_ref[...] = h * lax.rsqrt(jnp.mean(h * h, axis=-1, keepdims=True) + EPS) * ng_ref[...]

        @pl.when(i == pl.num_programs(0) - 1)
        def _():
            gather_wait(1 - p)

    for p in range(2):
        pl.when(lax.rem(i, 2) == p)(functools.partial(step, p))


def _combine(dest3, h1, w_cols, ada3, norm_g, y_rows, *, seq):
    t, d = h1.shape
    steps, _, tk = dest3.shape
    per_b = seq // tk

    def dest_spec(shift):
        return pl.BlockSpec((None, MOE_TOPK, tk), lambda i: (jnp.minimum(i + shift, steps - 1), 0, 0),
                            memory_space=pltpu.SMEM)

    return pl.pallas_call(
        _combine_kernel,
        grid=(steps,),
        in_specs=[dest_spec(0), dest_spec(1),
                  pl.BlockSpec((tk, d), lambda i: (i, 0)),
                  pl.BlockSpec((tk, LANES), lambda i: (i, 0)),
                  pl.BlockSpec((1, 1, d), lambda i: ((i // per_b) * 6 + 5, 0, 0)),
                  pl.BlockSpec((1, d), lambda i: (0, 0)),
                  pl.BlockSpec(memory_space=pl.ANY)],
        out_specs=pl.BlockSpec((tk, d), lambda i: (i, 0)),
        out_shape=jax.ShapeDtypeStruct((t, d), _F32),
        scratch_shapes=[*[pltpu.VMEM((tk * _pitch(d), LANES), _F32) for _ in range(2 * MOE_TOPK)],
                        pltpu.SemaphoreType.DMA((2,))],
        compiler_params=_params(("arbitrary",)),
        name="combine",
    )(dest3, dest3, h1, w_cols, ada3, norm_g.reshape(1, d), y_rows)


def _rotary_tables(positions, dh):
    inv_freq = ROPE_THETA ** (-jnp.arange(0, dh, 2, dtype=_F32) / dh)
    ang = positions.astype(_F32)[..., None] * inv_freq
    cos, sin = jnp.cos(ang), jnp.sin(ang)
    t = cos.shape[0] * cos.shape[1]
    cos = jnp.concatenate([cos, cos], axis=-1).reshape(t, dh)
    sin = jnp.concatenate([-sin, sin], axis=-1).reshape(t, dh)
    return cos, sin


def _tile(n, want):
    while n % want:
        want //= 2
    return want


def kernel(x, c, positions, w_ada, b_ada, norm_mix_g, w_in, sgu_ln_g, sgu_ln_b, sgu_w_s, sgu_b_s,
           w_sgu_out, w_moba_out, w_out, norm_ffn_g, w_route_group, b_route_group, w_route_expert,
           b_route_expert, w_exp_gate, w_exp_up, w_exp_down, norm_final_g):
    batch, seq, d = x.shape
    depth = w_ada.shape[0]
    t = batch * seq
    sw = sgu_ln_g.shape[1]
    mw = w_moba_out.shape[1]
    dh = mw // MOBA_HEADS
    groups = w_route_group.shape[2]
    n_exp = w_route_expert.shape[2]
    epg = n_exp // groups
    rb = EXPERT_ROW_BLOCK
    cap = (t * MOE_TOPK // rb + n_exp) * rb
    assert depth == 1, "the final RMSNorm is fused into the single layer's combine"
    assert dh == LANES and seq % MOBA_BLOCK == 0 and sw * 2 == d and mw % sw == 0
    assert groups + n_exp <= ROUTER_ROWS and (t * MOE_TOPK) % rb == 0

    cos, sin = _rotary_tables(positions, dh)
    c_pad = jnp.zeros((8, d), _F32).at[:batch].set(c)
    h = x.reshape(t, d)

    for l in range(depth):
        ada = _ada(c_pad, w_ada[l], b_ada[l], _tile(6 * d, 1024))
        ada3 = ada[:batch].reshape(batch * 6, 1, d)

        proj = _inproj(h, ada3, norm_mix_g[l], w_in[l].astype(_BF16), cos, sin, sgu_ln_g[l], sgu_ln_b[l],
                       seq=seq, sw=sw, mw=mw, dh=dh, tm=_tile(seq, 1024))
        bs_wide = jnp.repeat(sgu_b_s[l].T, sw // sgu_w_s.shape[1], axis=1)
        m_a = _sgu(proj, sgu_w_s[l], bs_wide, w_sgu_out[l].astype(_BF16), sw=sw, mw=mw, tm=_tile(seq, 1024))
        attn = _moba(proj, batch=batch, seq=seq, sw=sw, mw=mw, dh=dh, hg=min(MOBA_HEAD_GROUP, MOBA_HEADS))

        wr = jnp.concatenate([w_route_group[l], w_route_expert[l]], axis=1)
        wr_hi, wr_lo = _split_bf16(jnp.zeros((d, LANES), _F32).at[:, :groups + n_exp].set(wr))
        br = jnp.concatenate([b_route_group[l], b_route_expert[l].reshape(-1)])
        br = jnp.broadcast_to(jnp.zeros((ROUTER_ROWS,), _F32).at[:groups + n_exp].set(br)[:, None],
                              (ROUTER_ROWS, LANES))
        h1, n2, eid, wt = _mixout(attn, m_a, proj, h, ada3, norm_ffn_g[l], w_moba_out[l].astype(_BF16),
                                  w_out[l].astype(_BF16), wr_hi, wr_lo, br,
                                  seq=seq, sw=sw, mw=mw, groups=groups, epg=epg, tm=_tile(seq, 256))

        dest, tab = _plan(eid, n_exp=n_exp, tc=_tile(t, 512), rb=rb, n_blocks=cap // rb)
        def token_blocks(size):
            return dest.reshape(MOE_TOPK, t // size, size).transpose(1, 0, 2)

        inv = _invert(token_blocks(_tile(t, 2048)), cap)
        y_rows = _experts(tab[0, :cap // rb], tab[1, :1], inv.reshape(cap // rb, 1, rb), n2,
                          w_exp_gate[l], w_exp_up[l], w_exp_down[l])
        w_cols = jnp.zeros((t, LANES), _F32).at[:, :MOE_TOPK].set(wt.T)
        h = _combine(token_blocks(_tile(seq, 512)), h1, w_cols, ada3, norm_final_g, y_rows, seq=seq)

    return h.reshape(batch, seq, d)
```
